```python
import jax, jax.numpy as jnp
from jax import lax
import numpy as np

D_MODEL = 2048
BATCH = 2
SEQ = 4096
DEPTH = 1

N_HEADS = 16
HEAD_DIM = 64
D_ATTN = N_HEADS * HEAD_DIM
Q_BLOCK = 128
D_CONV = D_MODEL // 2
CONV_WIDTH = 3
N_BRANCH = 2
N_EXPERTS = 32
TOP_K = 4
D_EXPERT = D_MODEL
SWIGLU_LIMIT = 7.0
SWIGLU_ALPHA = 1.702
EXPERT_BLOCK = 128
RMS_EPS = 1e-6
N_MOD = 6
N_IN = 3 * D_CONV + 3 * D_ATTN + N_HEADS + N_BRANCH * D_MODEL

kernel_name = "hybrid_conv_fox_moe_adaln_block"


def _rmsnorm(x, g):
    xf = x.astype(jnp.float32)
    r = lax.rsqrt(jnp.mean(xf * xf, axis=-1, keepdims=True) + RMS_EPS)
    return (xf * r).astype(x.dtype) * g


def _short_conv(u, w):
    return lax.conv_general_dilated(
        u, w[:, None, :], window_strides=(1,), padding=[(CONV_WIDTH - 1, 0)],
        dimension_numbers=("NWC", "WIO", "NWC"), feature_group_count=u.shape[-1])


def _forgetting_attention(q, k, v, log_f):
    b, h, s, dh = q.shape
    F = jnp.cumsum(log_f, axis=-1)
    key_pos = jnp.arange(s)
    scale = HEAD_DIM ** -0.5

    def block(i):
        start = i * Q_BLOCK
        qb = lax.dynamic_slice_in_dim(q, start, Q_BLOCK, axis=2)
        Fq = lax.dynamic_slice_in_dim(F, start, Q_BLOCK, axis=2)
        q_pos = start + jnp.arange(Q_BLOCK)
        logits = jnp.einsum("bhqd,bhkd->bhqk", qb, k).astype(jnp.float32) * scale
        logits = logits + (Fq[..., :, None] - F[..., None, :])
        logits = jnp.where(key_pos[None, :] <= q_pos[:, None], logits, -jnp.inf)
        p = jax.nn.softmax(logits, axis=-1)
        return jnp.einsum("bhqk,bhkd->bhqd", p.astype(v.dtype), v)

    out = lax.map(block, jnp.arange(s // Q_BLOCK))
    return out.transpose(1, 2, 0, 3, 4).reshape(b, h, s, dh)


def _moe(h, router_w, router_b, w_gate, b_gate, w_up, b_up, w_down, b_down):
    bsz, s, d = h.shape
    n = bsz * s
    hf = h.reshape(n, d)
    logits = (hf @ router_w + router_b).astype(jnp.float32)
    top_vals, top_idx = lax.top_k(logits, TOP_K)
    probs = jax.nn.softmax(top_vals, axis=-1)
    expert_ids = top_idx.reshape(-1)
    token_ids = jnp.repeat(jnp.arange(n, dtype=jnp.int32), TOP_K)
    gates = probs.reshape(-1)
    order = jnp.argsort(expert_ids)
    s_exp, s_tok, s_gate = expert_ids[order], token_ids[order], gates[order]
    counts = jnp.zeros((N_EXPERTS,), jnp.int32).at[expert_ids].add(1)
    padded = ((counts + EXPERT_BLOCK - 1) // EXPERT_BLOCK) * EXPERT_BLOCK
    pad_end = jnp.cumsum(padded)
    pad_start = pad_end - padded
    grp_start = jnp.cumsum(counts) - counts
    rank = jnp.arange(n * TOP_K, dtype=jnp.int32) - grp_start[s_exp]
    dest = pad_start[s_exp] + rank
    n_rows = n * TOP_K + N_EXPERTS * EXPERT_BLOCK
    n_blocks = n_rows // EXPERT_BLOCK
    row_tok = jnp.full((n_rows,), n, jnp.int32).at[dest].set(s_tok)
    row_gate = jnp.zeros((n_rows,), jnp.float32).at[dest].set(s_gate)
    block_exp = jnp.minimum(
        jnp.searchsorted(pad_end, jnp.arange(n_blocks, dtype=jnp.int32) * EXPERT_BLOCK, side="right"),
        N_EXPERTS - 1)
    xin = jnp.take(hf, jnp.minimum(row_tok, n - 1), axis=0).reshape(n_blocks, EXPERT_BLOCK, d)

    def expert_block(args):
        xb, e = args
        g = xb @ w_gate[e] + b_gate[e]
        u = xb @ w_up[e] + b_up[e]
        g = jnp.minimum(g, SWIGLU_LIMIT)
        u = jnp.clip(u, -SWIGLU_LIMIT, SWIGLU_LIMIT)
        act = (u + 1.0) * g * jax.nn.sigmoid(SWIGLU_ALPHA * g)
        return act @ w_down[e] + b_down[e]

    yb = lax.map(expert_block, (xin, block_exp)).reshape(n_rows, d)
    out = jnp.zeros((n, d), h.dtype).at[row_tok].add(
        (yb * row_gate[:, None]).astype(h.dtype), mode="drop")
    return out.reshape(bsz, s, d)


def setup_inputs(seed: int = 0) -> dict:
    key = jax.random.key(seed)
    ks = jax.random.split(key, 24)
    L = DEPTH

    def nrm(k, shape, scale):
        return scale * jax.random.normal(k, shape, jnp.float32)

    return {
        "x": nrm(ks[0], (BATCH, SEQ, D_MODEL), 1.0),
        "c": nrm(ks[1], (BATCH, D_MODEL), 1.0),
        "ada_w": nrm(ks[2], (L, D_MODEL, N_MOD * D_MODEL), 0.3 * D_MODEL ** -0.5),
        "ada_b": nrm(ks[3], (L, N_MOD * D_MODEL), 0.02),
        "norm_mix_g": 1.0 + nrm(ks[4], (L, D_MODEL), 0.02),
        "w_in": nrm(ks[5], (L, D_MODEL, N_IN), D_MODEL ** -0.5),
        "b_forget": 3.0 + nrm(ks[6], (L, N_HEADS), 0.5),
        "conv_w": nrm(ks[7], (L, CONV_WIDTH, D_CONV), CONV_WIDTH ** -0.5),
        "w_conv_out": nrm(ks[8], (L, D_CONV, D_MODEL), D_CONV ** -0.5),
        "w_attn_out": nrm(ks[9], (L, D_ATTN, D_MODEL), D_ATTN ** -0.5),
        "w_out": nrm(ks[10], (L, D_MODEL, D_MODEL), D_MODEL ** -0.5),
        "norm_ffn_g": 1.0 + nrm(ks[11], (L, D_MODEL), 0.02),
        "router_w": nrm(ks[12], (L, D_MODEL, N_EXPERTS), D_MODEL ** -0.5),
        "router_b": nrm(ks[13], (L, N_EXPERTS), 0.01),
        "exp_w_gate": nrm(ks[14], (L, N_EXPERTS, D_MODEL, D_EXPERT), D_MODEL ** -0.5),
        "exp_b_gate": nrm(ks[15], (L, N_EXPERTS, D_EXPERT), 0.02),
        "exp_w_up": nrm(ks[16], (L, N_EXPERTS, D_MODEL, D_EXPERT), D_MODEL ** -0.5),
        "exp_b_up": nrm(ks[17], (L, N_EXPERTS, D_EXPERT), 0.02),
        "exp_w_down": nrm(ks[18], (L, N_EXPERTS, D_EXPERT, D_MODEL), D_EXPERT ** -0.5),
        "exp_b_down": nrm(ks[19], (L, N_EXPERTS, D_MODEL), 0.02),
        "final_norm_g": 1.0 + nrm(ks[20], (D_MODEL,), 0.02),
    }


def reference(x, c, ada_w, ada_b, norm_mix_g, w_in, b_forget, conv_w, w_conv_out, w_attn_out,
              w_out, norm_ffn_g, router_w, router_b, exp_w_gate, exp_b_gate, exp_w_up, exp_b_up,
              exp_w_down, exp_b_down, final_norm_g):
    bsz, s, _ = x.shape
    split_at = [D_CONV, 2 * D_CONV, 3 * D_CONV,
                3 * D_CONV + D_ATTN, 3 * D_CONV + 2 * D_ATTN, 3 * D_CONV + 3 * D_ATTN,
                3 * D_CONV + 3 * D_ATTN + N_HEADS,
                3 * D_CONV + 3 * D_ATTN + N_HEADS + D_MODEL]
    c_act = jax.nn.silu(c)
    for l in range(DEPTH):
        mod = c_act @ ada_w[l] + ada_b[l]
        shift_m, scale_m, gate_m, shift_f, scale_f, gate_f = [
            m[:, None, :] for m in jnp.split(mod, N_MOD, axis=-1)]

        h = _rmsnorm(x, norm_mix_g[l]) * (1.0 + scale_m) + shift_m
        proj = h @ w_in[l]
        xc, cb, cc, q, k, v, f_logit, g_conv, g_attn = jnp.split(proj, split_at, axis=-1)

        y_conv = (cb * _short_conv(cc * xc, conv_w[l])) @ w_conv_out[l]

        to_heads = lambda t: t.reshape(bsz, s, N_HEADS, HEAD_DIM).transpose(0, 2, 1, 3)
        log_f = jax.nn.log_sigmoid((f_logit + b_forget[l]).astype(jnp.float32)).transpose(0, 2, 1)
        att = _forgetting_attention(to_heads(q), to_heads(k), to_heads(v), log_f)
        y_attn = att.transpose(0, 2, 1, 3).reshape(bsz, s, D_ATTN) @ w_attn_out[l]

        merged = jax.nn.sigmoid(g_conv) * y_conv + jax.nn.sigmoid(g_attn) * y_attn
        x = x + gate_m * (merged @ w_out[l])

        h2 = _rmsnorm(x, norm_ffn_g[l]) * (1.0 + scale_f) + shift_f
        y_ffn = _moe(h2, router_w[l], router_b[l], exp_w_gate[l], exp_b_gate[l],
                     exp_w_up[l], exp_b_up[l], exp_w_down[l], exp_b_down[l])
        x = x + gate_f * y_ffn
    return _rmsnorm(x, final_norm_g)
```

```python
import functools

import jax
import jax.numpy as jnp
from jax import lax
from jax.experimental import pallas as pl
from jax.experimental.pallas import tpu as pltpu

F32 = jnp.float32
BF16 = jnp.bfloat16

N_HEADS = 16
HEAD_DIM = 64
N_EXPERTS = 32
TOP_K = 4
SWIGLU_LIMIT = 7.0
SWIGLU_ALPHA = 1.702
RMS_EPS = 1e-6
CONV_WIDTH = 3

LANES = 128
VMEM_LIMIT = 58 * 1024 * 1024

SB_ROWS = 1280
EXP_TM = 256
EXP_TH = 256
DISP_ROWS = 256
COMB_ROWS = 128


def _cparams(sem, vmem=VMEM_LIMIT):
    return pltpu.CompilerParams(dimension_semantics=sem, vmem_limit_bytes=vmem)


def _mod_kernel(cb_ref, w_ref, b_ref, o_ref):
    w = w_ref[...]
    tn = w.shape[1]
    rows = []
    for b in range(cb_ref.shape[0]):
        cv = cb_ref[b]
        ca = cv * jax.nn.sigmoid(cv)
        cols = [jnp.sum(w[:, j * LANES:(j + 1) * LANES] * ca, axis=0, keepdims=True)
                for j in range(tn // LANES)]
        rows.append(jnp.concatenate(cols, axis=1))
    o_ref[...] = jnp.concatenate(rows, axis=0) + b_ref[...]


def _mod(c, w, bias):
    bsz, d = c.shape
    n_out = w.shape[1]
    tn = 1024
    cb = jnp.broadcast_to(c[:, :, None], (bsz, d, LANES))
    return pl.pallas_call(
        _mod_kernel,
        grid=(n_out // tn,),
        in_specs=[pl.BlockSpec((bsz, d, LANES), lambda j: (0, 0, 0)),
                  pl.BlockSpec((d, tn), lambda j: (0, j)),
                  pl.BlockSpec((1, tn), lambda j: (0, j))],
        out_specs=pl.BlockSpec((bsz, tn), lambda j: (0, j)),
        out_shape=jax.ShapeDtypeStruct((bsz, n_out), F32),
        compiler_params=_cparams(("arbitrary",)),
        name="mod",
    )(cb, w, bias.reshape(1, n_out))


def _cast_kernel(x_ref, o_ref):
    o_ref[...] = x_ref[...].astype(o_ref.dtype)


def _cast_bf16(w):
    r, c = w.shape
    tr = 512
    return pl.pallas_call(
        _cast_kernel,
        grid=(r // tr,),
        in_specs=[pl.BlockSpec((tr, c), lambda i: (i, 0))],
        out_specs=pl.BlockSpec((tr, c), lambda i: (i, 0)),
        out_shape=jax.ShapeDtypeStruct((r, c), BF16),
        compiler_params=_cparams(("arbitrary",)),
        name="cast_bf16",
    )(w)


def _rmsnorm_mod(x, g, scale, shift):
    r = lax.rsqrt(jnp.mean(x * x, axis=-1, keepdims=True) + RMS_EPS)
    return (x * r) * g * (1.0 + scale) + shift


def _inproj_kernel(nj_main, x_ref, mod_ref, g_ref, wm_ref, wg_ref, wf_ref,
                   om_ref, og_ref, of_ref, h_ref):
    j = pl.program_id(1)

    @pl.when(j == 0)
    def _():
        wf = wf_ref[...].astype(BF16)
        rows = 256

        def body(rb, carry):
            r0 = pl.multiple_of(rb * rows, rows)
            h = _rmsnorm_mod(x_ref[pl.ds(r0, rows), :], g_ref[...], mod_ref[0, 1:2, :],
                             mod_ref[0, 0:1, :])
            hb = h.astype(BF16)
            h_ref[pl.ds(r0, rows), :] = hb
            of_ref[pl.ds(r0, rows), :] = jnp.dot(hb, wf, preferred_element_type=F32)
            return carry

        lax.fori_loop(0, x_ref.shape[0] // rows, body, 0)

    @pl.when(j < nj_main)
    def _():
        om_ref[...] = jnp.dot(h_ref[...], wm_ref[...].astype(BF16),
                              preferred_element_type=F32).astype(om_ref.dtype)

    @pl.when(j >= nj_main)
    def _():
        og_ref[...] = jnp.dot(h_ref[...], wg_ref[...].astype(BF16),
                              preferred_element_type=F32).astype(og_ref.dtype)


def _inproj(x2, mod3, g, w_in, w_gates, w_f, n_main, seq):
    n, d = x2.shape
    tm, tn = 1024, 512
    nj_main = n_main // tn
    nj_g = w_gates.shape[1] // tn
    tiles_per_seq = seq // tm
    return pl.pallas_call(
        functools.partial(_inproj_kernel, nj_main),
        grid=(n // tm, nj_main + nj_g),
        in_specs=[pl.BlockSpec((tm, d), lambda i, j: (i, 0)),
                  pl.BlockSpec((1, 6, d), lambda i, j: (i // tiles_per_seq, 0, 0)),
                  pl.BlockSpec((1, d), lambda i, j: (0, 0)),
                  pl.BlockSpec((d, tn), lambda i, j: (0, jnp.minimum(j, nj_main - 1))),
                  pl.BlockSpec((d, tn), lambda i, j: (0, jnp.maximum(j - nj_main, 0))),
                  pl.BlockSpec((d, LANES), lambda i, j: (0, 0))],
        out_specs=[pl.BlockSpec((tm, tn), lambda i, j: (i, jnp.minimum(j, nj_main - 1))),
                   pl.BlockSpec((tm, tn), lambda i, j: (i, jnp.maximum(j - nj_main, 0))),
                   pl.BlockSpec((tm, LANES), lambda i, j: (i, 0))],
        out_shape=[jax.ShapeDtypeStruct((n, n_main), BF16),
                   jax.ShapeDtypeStruct((n, w_gates.shape[1]), BF16),
                   jax.ShapeDtypeStruct((n, LANES), F32)],
        scratch_shapes=[pltpu.VMEM((tm, d), BF16)],
        compiler_params=_cparams(("arbitrary", "arbitrary")),
        name="inproj",
    )(x2, mod3, g.reshape(1, d), w_in, w_gates, w_f)


def _cumf_kernel(blocks_per_seq, f_ref, bf_ref, fc_ref, fr_ref, carry_ref):
    i = pl.program_id(0)

    @pl.when(i % blocks_per_seq == 0)
    def _():
        carry_ref[...] = jnp.zeros_like(carry_ref)

    z = f_ref[...] + bf_ref[...]
    lf = -(jnp.maximum(-z, 0.0) + jnp.log1p(jnp.exp(-jnp.abs(z))))
    t = z.shape[0]
    row = lax.broadcasted_iota(jnp.int32, (t, t), 0)
    col = lax.broadcasted_iota(jnp.int32, (t, t), 1)
    tri = (col <= row).astype(F32)
    fb = jnp.dot(tri, lf, precision=lax.Precision.HIGHEST,
                 preferred_element_type=F32) + carry_ref[...]
    fc_ref[...] = fb
    fr_ref[...] = fb.T
    carry_ref[...] = fb[t - 1:t, :]


def _cumf(f_logit, b_forget, seq):
    n = f_logit.shape[0]
    t = 256
    bf = jnp.pad(b_forget, (0, LANES - b_forget.shape[0])).reshape(1, LANES)
    return pl.pallas_call(
        functools.partial(_cumf_kernel, seq // t),
        grid=(n // t,),
        in_specs=[pl.BlockSpec((t, LANES), lambda i: (i, 0)),
                  pl.BlockSpec((1, LANES), lambda i: (0, 0))],
        out_specs=[pl.BlockSpec((t, LANES), lambda i: (i, 0)),
                   pl.BlockSpec((LANES, t), lambda i: (0, i))],
        out_shape=[jax.ShapeDtypeStruct((n, LANES), F32),
                   jax.ShapeDtypeStruct((LANES, n), F32)],
        scratch_shapes=[pltpu.VMEM((1, LANES), F32)],
        compiler_params=_cparams(("arbitrary",)),
        name="cumf",
    )(f_logit, bf)


def _dot_nt(a, b):
    return lax.dot_general(a, b, (((1,), (1,)), ((), ())), preferred_element_type=F32)


def _attn_kernel(tq, tk, q_ref, k_ref, v_ref, fc_ref, fr_ref, o_ref, m_ref, l_ref, acc_ref):
    hp = pl.program_id(1)
    qi = pl.program_id(2)
    lane = lax.broadcasted_iota(jnp.int32, (1, LANES), 1)
    first_half = lane < HEAD_DIM
    q = q_ref[...] * jnp.asarray(HEAD_DIM ** -0.5, BF16)
    qh = (jnp.where(first_half, q, jnp.zeros_like(q)), jnp.where(first_half, jnp.zeros_like(q), q))
    fblk = fc_ref[...]
    lane_f = lax.broadcasted_iota(jnp.int32, fblk.shape, 1)
    fq = tuple(jnp.sum(jnp.where(lane_f == 2 * hp + h, fblk, 0.0), axis=1, keepdims=True)
               for h in range(2))

    m_ref[...] = jnp.full(m_ref.shape, -jnp.inf, F32)
    l_ref[...] = jnp.zeros(l_ref.shape, F32)
    acc_ref[...] = jnp.zeros(acc_ref.shape, F32)

    def tile(kt, masked):
        k0 = pl.multiple_of(kt * tk, tk)
        k = k_ref[pl.ds(k0, tk), :]
        v = v_ref[pl.ds(k0, tk), :]
        pvs, alphas = [], []
        for h in range(2):
            fk = fr_ref[pl.ds(2 * hp + h, 1), pl.ds(k0, tk)]
            t = _dot_nt(qh[h], k) - fk
            if masked:
                r = lax.broadcasted_iota(jnp.int32, (tq, tk), 0)
                c = lax.broadcasted_iota(jnp.int32, (tq, tk), 1)
                t = jnp.where(c <= r, t, -jnp.inf)
            m_old = m_ref[h]
            m_new = jnp.maximum(m_old, jnp.max(t, axis=1, keepdims=True) + fq[h])
            alpha = jnp.exp(m_old - m_new)
            p = jnp.exp(t + (fq[h] - m_new))
            l_ref[h] = l_ref[h] * alpha + jnp.sum(p, axis=1, keepdims=True)
            m_ref[h] = m_new
            pvs.append(jnp.dot(p.astype(BF16), v, preferred_element_type=F32))
            alphas.append(alpha)
        acc_ref[...] = (acc_ref[...] * jnp.where(first_half, alphas[0], alphas[1])
                        + jnp.where(first_half, pvs[0], pvs[1]))

    def body(kt, carry):
        tile(kt, False)
        return carry

    lax.fori_loop(0, qi, body, 0)
    tile(qi, True)
    o_ref[...] = (acc_ref[...] / jnp.where(first_half, l_ref[0], l_ref[1])).astype(o_ref.dtype)


def _attn(proj, fcol, frow, bsz, seq, q_col0, k_col0, v_col0):
    tq = tk = 512
    nq = seq // tq
    n_pairs = N_HEADS * HEAD_DIM // LANES
    qb, kb, vb = q_col0 // LANES, k_col0 // LANES, v_col0 // LANES
    return pl.pallas_call(
        functools.partial(_attn_kernel, tq, tk),
        grid=(bsz, n_pairs, nq),
        in_specs=[pl.BlockSpec((tq, LANES), lambda b, hp, qi: (b * nq + qi, qb + hp)),
                  pl.BlockSpec((seq, LANES), lambda b, hp, qi: (b, kb + hp)),
                  pl.BlockSpec((seq, LANES), lambda b, hp, qi: (b, vb + hp)),
                  pl.BlockSpec((tq, LANES), lambda b, hp, qi: (b * nq + qi, 0)),
                  pl.BlockSpec((N_HEADS, seq), lambda b, hp, qi: (0, b))],
        out_specs=pl.BlockSpec((tq, LANES), lambda b, hp, qi: (b * nq + qi, hp)),
        out_shape=jax.ShapeDtypeStruct((bsz * seq, N_HEADS * HEAD_DIM), BF16),
        scratch_shapes=[pltpu.VMEM((2, tq, 1), F32), pltpu.VMEM((2, tq, 1), F32),
                        pltpu.VMEM((tq, LANES), F32)],
        compiler_params=_cparams(("arbitrary", "arbitrary", "arbitrary")),
        name="attn",
    )(proj, proj, proj, fcol, frow)


def _mix_kernel(tiles_per_seq, xc_ref, cb_ref, cc_ref, hxc_ref, hcc_ref, att_ref, gc_ref, ga_ref,
                x_ref, mod_ref, g2_ref, cw_ref, wc_ref, wa_ref, wo_ref, rw_ref, rb_ref,
                x1_ref, h2_ref, lg_ref):
    i = pl.program_id(0)
    u = cc_ref[...].astype(F32) * xc_ref[...].astype(F32)
    hu = hcc_ref[...].astype(F32) * hxc_ref[...].astype(F32)
    hu = jnp.where(i % tiles_per_seq == 0, 0.0, hu)
    nh = hu.shape[0]
    row = lax.broadcasted_iota(jnp.int32, u.shape, 0)
    u1 = jnp.where(row == 0, hu[nh - 1:nh, :], pltpu.roll(u, 1, axis=0))
    u2 = jnp.where(row == 0, hu[nh - 2:nh - 1, :],
                   jnp.where(row == 1, hu[nh - 1:nh, :], pltpu.roll(u, 2, axis=0)))
    conv = cw_ref[0:1, :] * u2 + cw_ref[1:2, :] * u1 + cw_ref[2:3, :] * u
    z = (cb_ref[...].astype(F32) * conv).astype(BF16)
    y_conv = jnp.dot(z, wc_ref[...], preferred_element_type=F32)
    y_attn = jnp.dot(att_ref[...], wa_ref[...], preferred_element_type=F32)
    merged = (jax.nn.sigmoid(gc_ref[...].astype(F32)) * y_conv
              + jax.nn.sigmoid(ga_ref[...].astype(F32)) * y_attn)
    o = jnp.dot(merged.astype(BF16), wo_ref[...], preferred_element_type=F32)
    x1 = x_ref[...] + mod_ref[0, 2:3, :] * o
    x1_ref[...] = x1
    h2 = _rmsnorm_mod(x1, g2_ref[...], mod_ref[0, 4:5, :], mod_ref[0, 3:4, :])
    h2_ref[...] = h2
    lg_ref[...] = jnp.dot(h2, rw_ref[...], precision=lax.Precision.HIGHEST,
                          preferred_element_type=F32) + rb_ref[...]


def _mix(proj, gates, att, x2, mod3, g2, conv_w, wc, wa, wo, rw, rb, seq, d_conv):
    n, d = x2.shape
    tm = 256
    halo = 16
    tiles_per_seq = seq // tm
    d_attn = att.shape[1]

    def resident(shape):
        return pl.BlockSpec(shape, lambda i: (0,) * len(shape), pipeline_mode=pl.Buffered(1))

    return pl.pallas_call(
        functools.partial(_mix_kernel, tiles_per_seq),
        grid=(n // tm,),
        in_specs=[pl.BlockSpec((tm, d_conv), lambda i: (i, 0)),
                  pl.BlockSpec((tm, d_conv), lambda i: (i, 1)),
                  pl.BlockSpec((tm, d_conv), lambda i: (i, 2)),
                  pl.BlockSpec((halo, d_conv), lambda i: (jnp.maximum(i * (tm // halo) - 1, 0), 0)),
                  pl.BlockSpec((halo, d_conv), lambda i: (jnp.maximum(i * (tm // halo) - 1, 0), 2)),
                  pl.BlockSpec((tm, d_attn), lambda i: (i, 0)),
                  pl.BlockSpec((tm, d), lambda i: (i, 0)),
                  pl.BlockSpec((tm, d), lambda i: (i, 1)),
                  pl.BlockSpec((tm, d), lambda i: (i, 0)),
                  pl.BlockSpec((1, 6, d), lambda i: (i // tiles_per_seq, 0, 0)),
                  resident((1, d)),
                  resident((CONV_WIDTH, d_conv)),
                  resident((d_conv, d)),
                  resident((d_attn, d)),
                  resident((d, d)),
                  resident((d, LANES)),
                  resident((1, LANES))],
        out_specs=[pl.BlockSpec((tm, d), lambda i: (i, 0)),
                   pl.BlockSpec((tm, d), lambda i: (i, 0)),
                   pl.BlockSpec((tm, LANES), lambda i: (i, 0))],
        out_shape=[jax.ShapeDtypeStruct((n, d), F32),
                   jax.ShapeDtypeStruct((n, d), F32),
                   jax.ShapeDtypeStruct((n, LANES), F32)],
        compiler_params=_cparams(("arbitrary",)),
        name="mix",
    )(proj, proj, proj, proj, proj, att, gates, gates, x2, mod3, g2.reshape(1, d), conv_w,
      wc, wa, wo, rw, rb)


def _route_kernel(lg_ref, idx_ref, prob_ref, cnt_ref, carry_ref):
    i = pl.program_id(0)

    @pl.when(i == 0)
    def _():
        carry_ref[...] = jnp.zeros_like(carry_ref)

    tt = lg_ref.shape[0]
    lane = lax.broadcasted_iota(jnp.int32, (tt, LANES), 1)
    lane_f = lane.astype(F32)
    l = jnp.where(lane < N_EXPERTS, lg_ref[...], -jnp.inf)
    onehot = jnp.zeros((tt, LANES), F32)
    vals, ids = [], []
    for _ in range(TOP_K):
        m = jnp.max(l, axis=1, keepdims=True)
        idx = jnp.min(jnp.where(l == m, lane_f, float(LANES)), axis=1, keepdims=True)
        sel = lane_f == idx
        vals.append(m)
        ids.append(idx)
        onehot = jnp.where(sel, 1.0, onehot)
        l = jnp.where(sel, -jnp.inf, l)
    es = [jnp.exp(v - vals[0]) for v in vals]
    denom = es[0] + es[1] + es[2] + es[3]
    r = lax.broadcasted_iota(jnp.int32, (tt, tt), 0)
    c = lax.broadcasted_iota(jnp.int32, (tt, tt), 1)
    before = jnp.where(c < r, 1.0, 0.0).astype(BF16)
    cnt_before = jnp.dot(before, onehot.astype(BF16), preferred_element_type=F32) + carry_ref[...]
    idx_out = jnp.zeros((tt, LANES), jnp.int32)
    prob_out = jnp.zeros((tt, LANES), F32)
    for k in range(TOP_K):
        rank = jnp.sum(jnp.where(lane_f == ids[k], cnt_before, 0.0), axis=1, keepdims=True)
        idx_out = jnp.where(lane == k, ids[k].astype(jnp.int32), idx_out)
        idx_out = jnp.where(lane == TOP_K + k, rank.astype(jnp.int32), idx_out)
        prob_out = jnp.where(lane == k, es[k] / denom, prob_out)
    idx_ref[...] = idx_out
    prob_ref[...] = prob_out
    carry_ref[...] = carry_ref[...] + jnp.sum(onehot, axis=0, keepdims=True)
    cnt_ref[...] = carry_ref[...].astype(jnp.int32)


def _route(logits):
    n = logits.shape[0]
    tt = 256
    return pl.pallas_call(
        _route_kernel,
        grid=(n // tt,),
        in_specs=[pl.BlockSpec((tt, LANES), lambda i: (i, 0))],
        out_specs=[pl.BlockSpec((tt, LANES), lambda i: (i, 0)),
                   pl.BlockSpec((tt, LANES), lambda i: (i, 0)),
                   pl.BlockSpec((1, LANES), lambda i: (0, 0))],
        out_shape=[jax.ShapeDtypeStruct((n, LANES), jnp.int32),
                   jax.ShapeDtypeStruct((n, LANES), F32),
                   jax.ShapeDtypeStruct((1, LANES), jnp.int32)],
        scratch_shapes=[pltpu.VMEM((1, LANES), F32)],
        compiler_params=_cparams(("arbitrary",)),
        name="route",
    )(logits)


def _row_gather_start(src_hbm, tok_ref, n_rows, dst, sem):
    def body(r, carry):
        t = tok_ref[0, 0, r]
        pltpu.make_async_copy(src_hbm.at[pl.ds(t, 1), :], dst.at[pl.ds(r, 1), :], sem).start()
        return carry
    lax.fori_loop(0, n_rows, body, 0)


def _dispatch_kernel(blk_ref, nused_ref, tok_cur_ref, tok_nxt_ref, h_hbm, o_ref, stage, sem):
    j = pl.program_id(0)
    n_used = nused_ref[0]
    rows = stage.shape[1]

    @pl.when(j == 0)
    def _():
        _row_gather_start(h_hbm, tok_cur_ref, rows, stage.at[0], sem.at[0])

    @pl.when(j + 1 < n_used)
    def _():
        nxt = (j + 1) % 2
        _row_gather_start(h_hbm, tok_nxt_ref, rows, stage.at[nxt], sem.at[nxt])

    @pl.when(j < n_used)
    def _():
        cur = j % 2
        pltpu.make_async_copy(h_hbm.at[pl.ds(0, rows), :], stage.at[cur], sem.at[cur]).wait()
        o_ref[...] = stage[cur].astype(o_ref.dtype)


def _dispatch(h2, row_tok, blk_ids, n_used, n_slot_rows):
    n, d = h2.shape
    nb_max = blk_ids.shape[0]
    tok3 = row_tok.reshape(n_slot_rows // DISP_ROWS, 1, DISP_ROWS)
    grid_spec = pltpu.PrefetchScalarGridSpec(
        num_scalar_prefetch=2,
        grid=(nb_max,),
        in_specs=[pl.BlockSpec((1, 1, DISP_ROWS), lambda j, blk, nu: (blk[j], 0, 0),
                               memory_space=pltpu.SMEM),
                  pl.BlockSpec((1, 1, DISP_ROWS),
                               lambda j, blk, nu: (blk[jnp.minimum(j + 1, nb_max - 1)], 0, 0),
                               memory_space=pltpu.SMEM),
                  pl.BlockSpec(memory_space=pl.ANY)],
        out_specs=pl.BlockSpec((DISP_ROWS, d), lambda j, blk, nu: (blk[j], 0)),
        scratch_shapes=[pltpu.VMEM((2, DISP_ROWS, d), F32), pltpu.SemaphoreType.DMA((2,))],
    )
    return pl.pallas_call(
        _dispatch_kernel,
        grid_spec=grid_spec,
        out_shape=jax.ShapeDtypeStruct((n_slot_rows, d), BF16),
        compiler_params=_cparams(("arbitrary",)),
        name="dispatch",
    )(blk_ids, n_used, tok3, tok3, h2)


def _expert_kernel(sbe_ref, sbn_ref, ns_ref, x_ref, wg_ref, wu_ref, wd_ref, bg_ref, bu_ref, bd_ref,
                   o_ref, wgu_ref, wdb_ref):
    s = pl.program_id(0)
    c = pl.program_id(1)
    th = wd_ref.shape[1]

    @pl.when(s < ns_ref[0])
    def _():
        wgu_ref[:, :th] = wg_ref[0].astype(BF16)
        wgu_ref[:, th:] = wu_ref[0].astype(BF16)
        wdb_ref[...] = wd_ref[0].astype(BF16)
        bg = bg_ref[0]
        bu = bu_ref[0]
        bd = bd_ref[0]
        n_blk = sbn_ref[s]

        @pl.when(c == 0)
        def _():
            def zero(rb, carry):
                r0 = pl.multiple_of(rb * EXP_TM, EXP_TM)
                o_ref[pl.ds(r0, EXP_TM), :] = jnp.zeros((EXP_TM, o_ref.shape[1]), F32)
                return carry
            lax.fori_loop(n_blk, SB_ROWS // EXP_TM, zero, 0)

        def body(rb, carry):
            r0 = pl.multiple_of(rb * EXP_TM, EXP_TM)
            gu = jnp.dot(x_ref[pl.ds(r0, EXP_TM), :], wgu_ref[...], preferred_element_type=F32)
            g = jnp.minimum(gu[:, :th] + bg, SWIGLU_LIMIT)
            u = jnp.clip(gu[:, th:] + bu, -SWIGLU_LIMIT, SWIGLU_LIMIT)
            act = (u + 1.0) * g * jax.nn.sigmoid(SWIGLU_ALPHA * g)
            y = jnp.dot(act.astype(BF16), wdb_ref[...], preferred_element_type=F32)

            @pl.when(c == 0)
            def _():
                o_ref[pl.ds(r0, EXP_TM), :] = y + bd

            @pl.when(c > 0)
            def _():
                o_ref[pl.ds(r0, EXP_TM), :] += y

            return carry

        lax.fori_loop(0, n_blk, body, 0)


def _expert(xin, sb_expert, sb_nblk, n_super, wg, wu, wd, bg, bu, bd):
    n_slot_rows, d = xin.shape
    s_max = n_slot_rows // SB_ROWS
    d_exp = wg.shape[2]
    n_chunks = d_exp // EXP_TH
    ne = wg.shape[0]

    def s_eff(s, ns):
        return jnp.minimum(s, ns[0] - 1)

    def c_eff(s, c, ns):
        return jnp.where(s < ns[0], c, n_chunks - 1)

    grid_spec = pltpu.PrefetchScalarGridSpec(
        num_scalar_prefetch=3,
        grid=(s_max, n_chunks),
        in_specs=[
            pl.BlockSpec((SB_ROWS, d), lambda s, c, sbe, sbn, ns: (s_eff(s, ns), 0)),
            pl.BlockSpec((1, d, EXP_TH),
                         lambda s, c, sbe, sbn, ns: (sbe[s_eff(s, ns)], 0, c_eff(s, c, ns))),
            pl.BlockSpec((1, d, EXP_TH),
                         lambda s, c, sbe, sbn, ns: (sbe[s_eff(s, ns)], 0, c_eff(s, c, ns))),
            pl.BlockSpec((1, EXP_TH, d),
                         lambda s, c, sbe, sbn, ns: (sbe[s_eff(s, ns)], c_eff(s, c, ns), 0)),
            pl.BlockSpec((1, 1, EXP_TH),
                         lambda s, c, sbe, sbn, ns: (sbe[s_eff(s, ns)], 0, c_eff(s, c, ns))),
            pl.BlockSpec((1, 1, EXP_TH),
                         lambda s, c, sbe, sbn, ns: (sbe[s_eff(s, ns)], 0, c_eff(s, c, ns))),
            pl.BlockSpec((1, 1, d), lambda s, c, sbe, sbn, ns: (sbe[s_eff(s, ns)], 0, 0)),
        ],
        out_specs=pl.BlockSpec((SB_ROWS, d), lambda s, c, sbe, sbn, ns: (s_eff(s, ns), 0)),
        scratch_shapes=[pltpu.VMEM((d, 2 * EXP_TH), BF16), pltpu.VMEM((EXP_TH, d), BF16)],
    )
    return pl.pallas_call(
        _expert_kernel,
        grid_spec=grid_spec,
        out_shape=jax.ShapeDtypeStruct((n_slot_rows, d), F32),
        compiler_params=_cparams(("arbitrary", "arbitrary")),
        name="expert",
    )(sb_expert, sb_nblk, n_super, xin, wg, wu, wd,
      bg.reshape(ne, 1, d_exp), bu.reshape(ne, 1, d_exp), bd.reshape(ne, 1, d))


def _combine_start(y_hbm, pos_ref, stage_slot, sem):
    tt = stage_slot.shape[1]

    def body(r, carry):
        for k in range(TOP_K):
            p = pos_ref[0, 0, r * TOP_K + k]
            pltpu.make_async_copy(y_hbm.at[pl.ds(p, 1), :], stage_slot.at[k, pl.ds(r, 1), :],
                                  sem).start()
        return carry
    lax.fori_loop(0, tt, body, 0)


def _combine_kernel(final_norm, pos_cur_ref, pos_nxt_ref, y_hbm, prob_ref, x1_ref, mod_ref, g_ref,
                    o_ref, stage, sem):
    j = pl.program_id(0)
    nj = pl.num_programs(0)
    tt = stage.shape[2]

    @pl.when(j == 0)
    def _():
        _combine_start(y_hbm, pos_cur_ref, stage.at[0], sem.at[0])

    @pl.when(j + 1 < nj)
    def _():
        nxt = (j + 1) % 2
        _combine_start(y_hbm, pos_nxt_ref, stage.at[nxt], sem.at[nxt])

    cur = j % 2
    for k in range(TOP_K):
        pltpu.make_async_copy(y_hbm.at[pl.ds(0, tt), :], stage.at[cur, k], sem.at[cur]).wait()
    prob = prob_ref[...]
    y = prob[:, 0:1] * stage[cur, 0]
    for k in range(1, TOP_K):
        y = y + prob[:, k:k + 1] * stage[cur, k]
    x2 = x1_ref[...] + mod_ref[0, 5:6, :] * y
    if final_norm:
        r = lax.rsqrt(jnp.mean(x2 * x2, axis=-1, keepdims=True) + RMS_EPS)
        x2 = (x2 * r) * g_ref[...]
    o_ref[...] = x2


def _combine(yb, pos, probs, x1, mod3, g, seq, final_norm):
    n, d = x1.shape
    tt = COMB_ROWS
    nt = n // tt
    tiles_per_seq = seq // tt
    pos3 = pos.reshape(nt, 1, tt * TOP_K)
    return pl.pallas_call(
        functools.partial(_combine_kernel, final_norm),
        grid=(nt,),
        in_specs=[pl.BlockSpec((1, 1, tt * TOP_K), lambda j: (j, 0, 0), memory_space=pltpu.SMEM),
                  pl.BlockSpec((1, 1, tt * TOP_K), lambda j: (jnp.minimum(j + 1, nt - 1), 0, 0),
                               memory_space=pltpu.SMEM),
                  pl.BlockSpec(memory_space=pl.ANY),
                  pl.BlockSpec((tt, LANES), lambda j: (j, 0)),
                  pl.BlockSpec((tt, d), lambda j: (j, 0)),
                  pl.BlockSpec((1, 6, d), lambda j: (j // tiles_per_seq, 0, 0)),
                  pl.BlockSpec((1, d), lambda j: (0, 0))],
        out_specs=pl.BlockSpec((tt, d), lambda j: (j, 0)),
        out_shape=jax.ShapeDtypeStruct((n, d), F32),
        scratch_shapes=[pltpu.VMEM((2, TOP_K, tt, d), F32), pltpu.SemaphoreType.DMA((2,))],
        compiler_params=_cparams(("arbitrary",)),
        name="combine",
    )(pos3, pos3, yb, probs, x1, mod3, g.reshape(1, d))


def _count_le(sorted_ends, q):
    return jnp.sum((sorted_ends[None, :] <= q[:, None]).astype(jnp.int32), axis=1)


def _plan(idx_rank, counts, n_tok):
    s_max = -(-n_tok * TOP_K // SB_ROWS) + N_EXPERTS
    blocks_per_sb = SB_ROWS // DISP_ROWS
    nb_max = n_tok * TOP_K // DISP_ROWS + s_max
    cnt = counts[0, :N_EXPERTS]
    nsb_e = (cnt + SB_ROWS - 1) // SB_ROWS
    sb_end = jnp.cumsum(nsb_e)
    sb_start = sb_end - nsb_e
    n_super = sb_end[-1]
    eid = idx_rank[:, :TOP_K]
    rank = idx_rank[:, TOP_K:2 * TOP_K]
    dest = (sb_start[eid] + rank // SB_ROWS) * SB_ROWS + rank % SB_ROWS
    s_ids = jnp.arange(s_max, dtype=jnp.int32)
    sb_expert = jnp.minimum(_count_le(sb_end, s_ids), N_EXPERTS - 1)
    sb_rows = jnp.clip(cnt[sb_expert] - (s_ids - sb_start[sb_expert]) * SB_ROWS, 0, SB_ROWS)
    sb_rows = jnp.where(s_ids < n_super, sb_rows, 0)
    sb_nblk = ((sb_rows + EXP_TM - 1) // EXP_TM).astype(jnp.int32)
    nblk_d = (sb_rows + DISP_ROWS - 1) // DISP_ROWS
    blk_end = jnp.cumsum(nblk_d)
    n_used = blk_end[-1]
    j_ids = jnp.arange(nb_max, dtype=jnp.int32)
    j_eff = jnp.minimum(j_ids, n_used - 1)
    sb_of = jnp.minimum(_count_le(blk_end, j_eff), s_max - 1)
    blk_ids = (sb_of * blocks_per_sb + (j_eff - (blk_end[sb_of] - nblk_d[sb_of]))).astype(jnp.int32)
    tok = jnp.broadcast_to(jnp.arange(n_tok, dtype=jnp.int32)[:, None], dest.shape)
    row_tok = jnp.zeros((s_max * SB_ROWS,), jnp.int32).at[dest.reshape(-1)].set(tok.reshape(-1))
    return (dest.astype(jnp.int32), row_tok, blk_ids, n_used.reshape(1).astype(jnp.int32),
            sb_expert, sb_nblk, n_super.reshape(1).astype(jnp.int32), s_max * SB_ROWS)


def kernel(x, c, ada_w, ada_b, norm_mix_g, w_in, b_forget, conv_w, w_conv_out, w_attn_out, w_out,
           norm_ffn_g, router_w, router_b, exp_w_gate, exp_b_gate, exp_w_up, exp_b_up,
           exp_w_down, exp_b_down, final_norm_g):
    bsz, seq, d = x.shape
    n = bsz * seq
    d_conv = conv_w.shape[2]
    d_attn = w_attn_out.shape[1]
    n_main = 3 * d_conv + 3 * d_attn
    gate_col0 = n_main + N_HEADS
    x2 = x.reshape(n, d)
    out = x2
    for l in range(ada_w.shape[0]):
        mod3 = _mod(c, ada_w[l], ada_b[l]).reshape(bsz, 6, d)
        w_gates = w_in[l][:, gate_col0:]
        w_f = jnp.pad(w_in[l][:, n_main:gate_col0], ((0, 0), (0, LANES - N_HEADS)))
        proj, gates, f_logit = _inproj(out, mod3, norm_mix_g[l], w_in[l], w_gates, w_f, n_main, seq)
        fcol, frow = _cumf(f_logit, b_forget[l], seq)
        att = _attn(proj, fcol, frow, bsz, seq, 3 * d_conv, 3 * d_conv + d_attn, 3 * d_conv + 2 * d_attn)
        rw = jnp.pad(router_w[l], ((0, 0), (0, LANES - N_EXPERTS)))
        rb = jnp.pad(router_b[l], (0, LANES - N_EXPERTS)).reshape(1, LANES)
        x1, h2, logits = _mix(proj, gates, att, out, mod3, norm_ffn_g[l], conv_w[l],
                              _cast_bf16(w_conv_out[l]), _cast_bf16(w_attn_out[l]), _cast_bf16(w_out[l]),
                              rw, rb, seq, d_conv)
        idx_rank, probs, counts = _route(logits)
        dest, row_tok, blk_ids, n_used, sb_expert, sb_nblk, n_super, n_slot_rows = _plan(idx_rank, counts, n)
        xin = _dispatch(h2, row_tok, blk_ids, n_used, n_slot_rows)
        yb = _expert(xin, sb_expert, sb_nblk, n_super, exp_w_gate[l], exp_w_up[l], exp_w_down[l],
                     exp_b_gate[l], exp_b_up[l], exp_b_down[l])
        out = _combine(yb, dest.reshape(-1), probs, x1, mod3, final_norm_g, seq,
                       final_norm=(l == ada_w.shape[0] - 1))
    return out.reshape(bsz, seq, d)
```

```python
import functools

import jax
import jax.numpy as jnp
from jax import lax
from jax.experimental import pallas as pl
from jax.experimental.pallas import tpu as pltpu

F32 = jnp.float32
BF16 = jnp.bfloat16

N_HEADS = 16
HEAD_DIM = 64
N_EXPERTS = 32
TOP_K = 4
SWIGLU_LIMIT = 7.0
SWIGLU_ALPHA = 1.702
RMS_EPS = 1e-6
CONV_WIDTH = 3

LANES = 128
VMEM_LIMIT = 58 * 1024 * 1024

ROW_ALIGN = 8

SB_ROWS = 1536
EXP_TM = 512
EXP_TH = 256
DISP_TOKENS = 256
COMB_ROWS = 128


def _cparams(sem, vmem=VMEM_LIMIT):
    return pltpu.CompilerParams(dimension_semantics=sem, vmem_limit_bytes=vmem)


def _mod_kernel(cb_ref, w_ref, b_ref, o_ref):
    w = w_ref[...]
    tn = w.shape[1]
    rows = []
    for b in range(cb_ref.shape[0]):
        cv = cb_ref[b]
        ca = cv * jax.nn.sigmoid(cv)
        cols = [jnp.sum(w[:, j * LANES:(j + 1) * LANES] * ca, axis=0, keepdims=True)
                for j in range(tn // LANES)]
        rows.append(jnp.concatenate(cols, axis=1))
    o_ref[...] = jnp.concatenate(rows, axis=0) + b_ref[...]


def _mod(c, w, bias):
    bsz, d = c.shape
    n_out = w.shape[1]
    tn = 1024
    cb = jnp.broadcast_to(c[:, :, None], (bsz, d, LANES))
    return pl.pallas_call(
        _mod_kernel,
        grid=(n_out // tn,),
        in_specs=[pl.BlockSpec((bsz, d, LANES), lambda j: (0, 0, 0)),
                  pl.BlockSpec((d, tn), lambda j: (0, j)),
                  pl.BlockSpec((1, tn), lambda j: (0, j))],
        out_specs=pl.BlockSpec((bsz, tn), lambda j: (0, j)),
        out_shape=jax.ShapeDtypeStruct((bsz, n_out), F32),
        compiler_params=_cparams(("arbitrary",)),
        name="mod",
    )(cb, w, bias.reshape(1, n_out))


def _cast_kernel(x_ref, o_ref):
    o_ref[...] = x_ref[...].astype(o_ref.dtype)


def _cast_bf16(w):
    r, c = w.shape
    tr = 512
    return pl.pallas_call(
        _cast_kernel,
        grid=(r // tr,),
        in_specs=[pl.BlockSpec((tr, c), lambda i: (i, 0))],
        out_specs=pl.BlockSpec((tr, c), lambda i: (i, 0)),
        out_shape=jax.ShapeDtypeStruct((r, c), BF16),
        compiler_params=_cparams(("arbitrary",)),
        name="cast_bf16",
    )(w)


def _rmsnorm_mod(x, g, scale, shift):
    r = lax.rsqrt(jnp.mean(x * x, axis=-1, keepdims=True) + RMS_EPS)
    return (x * r) * g * (1.0 + scale) + shift


def _inproj_kernel(nj_main, x_ref, mod_ref, g_ref, wm_ref, wg_ref, wf_ref,
                   om_ref, og_ref, of_ref, h_ref, wfp_ref):
    j = pl.program_id(1)

    @pl.when(j == 0)
    def _():
        wfp_ref[...] = jnp.zeros(wfp_ref.shape, BF16)
        wfp_ref[0:wf_ref.shape[0], :] = wf_ref[...].astype(BF16)
        rows = 256

        def body(rb, carry):
            r0 = pl.multiple_of(rb * rows, rows)
            h = _rmsnorm_mod(x_ref[pl.ds(r0, rows), :], g_ref[...], mod_ref[0, 1:2, :],
                             mod_ref[0, 0:1, :])
            hb = h.astype(BF16)
            h_ref[pl.ds(r0, rows), :] = hb
            of_ref[pl.ds(r0, rows), :] = _dot_nt(hb, wfp_ref[...])
            return carry

        lax.fori_loop(0, x_ref.shape[0] // rows, body, 0)

    @pl.when(j < nj_main)
    def _():
        om_ref[...] = _dot_nt(h_ref[...], wm_ref[...].astype(BF16)).astype(om_ref.dtype)

    @pl.when(j >= nj_main)
    def _():
        og_ref[...] = _dot_nt(h_ref[...], wg_ref[...].astype(BF16)).astype(og_ref.dtype)


def _inproj(x2, mod3, g, w_in_t, n_main, n_f, n_gates, seq):
    n, d = x2.shape
    tm, tn = 1024, 512
    nj_main = n_main // tn
    nj_g = n_gates // tn
    gate_row0 = n_main + n_f
    tiles_per_seq = seq // tm
    return pl.pallas_call(
        functools.partial(_inproj_kernel, nj_main),
        grid=(n // tm, nj_main + nj_g),
        in_specs=[pl.BlockSpec((tm, d), lambda i, j: (i, 0)),
                  pl.BlockSpec((1, 6, d), lambda i, j: (i // tiles_per_seq, 0, 0)),
                  pl.BlockSpec((1, d), lambda i, j: (0, 0)),
                  pl.BlockSpec((tn, d), lambda i, j: (jnp.minimum(j, nj_main - 1), 0)),
                  pl.BlockSpec((pl.Element(tn), pl.Element(d)),
                               lambda i, j: (pl.multiple_of(gate_row0 + tn * jnp.maximum(j - nj_main, 0), n_f), 0)),
                  pl.BlockSpec((n_f, d), lambda i, j: (n_main // n_f, 0))],
        out_specs=[pl.BlockSpec((tm, tn), lambda i, j: (i, jnp.minimum(j, nj_main - 1))),
                   pl.BlockSpec((tm, tn), lambda i, j: (i, jnp.maximum(j - nj_main, 0))),
                   pl.BlockSpec((tm, LANES), lambda i, j: (i, 0))],
        out_shape=[jax.ShapeDtypeStruct((n, n_main), BF16),
                   jax.ShapeDtypeStruct((n, n_gates), BF16),
                   jax.ShapeDtypeStruct((n, LANES), F32)],
        scratch_shapes=[pltpu.VMEM((tm, d), BF16), pltpu.VMEM((LANES, d), BF16)],
        compiler_params=_cparams(("arbitrary", "arbitrary")),
        name="inproj",
    )(x2, mod3, g.reshape(1, d), w_in_t, w_in_t, w_in_t)


def _cumf_kernel(blocks_per_seq, f_ref, bf_ref, fc_ref, carry_ref):
    i = pl.program_id(0)

    @pl.when(i % blocks_per_seq == 0)
    def _():
        carry_ref[...] = jnp.zeros_like(carry_ref)

    z = f_ref[...] + bf_ref[...]
    lf = -(jnp.maximum(-z, 0.0) + jnp.log1p(jnp.exp(-jnp.abs(z))))
    t = z.shape[0]
    row = lax.broadcasted_iota(jnp.int32, (t, t), 0)
    col = lax.broadcasted_iota(jnp.int32, (t, t), 1)
    tri = (col <= row).astype(F32)
    fb = jnp.dot(tri, lf, precision=lax.Precision.HIGHEST,
                 preferred_element_type=F32) + carry_ref[...]
    fc_ref[...] = fb
    carry_ref[...] = fb[t - 1:t, :]


def _cumf(f_logit, b_forget, seq):
    n = f_logit.shape[0]
    t = 256
    bf = jnp.pad(b_forget, (0, LANES - b_forget.shape[0])).reshape(1, LANES)
    return pl.pallas_call(
        functools.partial(_cumf_kernel, seq // t),
        grid=(n // t,),
        in_specs=[pl.BlockSpec((t, LANES), lambda i: (i, 0)),
                  pl.BlockSpec((1, LANES), lambda i: (0, 0))],
        out_specs=pl.BlockSpec((t, LANES), lambda i: (i, 0)),
        out_shape=jax.ShapeDtypeStruct((n, LANES), F32),
        scratch_shapes=[pltpu.VMEM((1, LANES), F32)],
        compiler_params=_cparams(("arbitrary",)),
        name="cumf",
    )(f_logit, bf)


def _dot_nt(a, b):
    return lax.dot_general(a, b, (((1,), (1,)), ((), ())), preferred_element_type=F32)


def _split3(x):
    hi = x.astype(BF16)
    r1 = x - hi.astype(F32)
    mid = r1.astype(BF16)
    lo = (r1 - mid.astype(F32)).astype(BF16)
    return hi, mid, lo


def _bias_lanes(f, head, lane0, sign):
    r = lax.broadcasted_iota(jnp.int32, (LANES, LANES), 0)
    c = lax.broadcasted_iota(jnp.int32, (LANES, LANES), 1)
    out = None
    for j, part in enumerate(_split3(f)):
        sel = jnp.where(jnp.logical_and(r == head, c == lane0 + j), sign, 0.0).astype(BF16)
        term = jnp.dot(part, sel, preferred_element_type=F32)
        out = term if out is None else out + term
    return out


def _attn_kernel(tq, tk, q_ref, k_ref, v_ref, fc_ref, o_ref, kp_ref, vt_ref, m_ref, l_ref, acc_ref,
                 sa_ref, sb_ref):
    hp = pl.program_id(1)
    qi = pl.program_id(2)
    seq = k_ref.shape[0]
    lane = lax.broadcasted_iota(jnp.int32, (1, LANES), 1)
    own = (lane < HEAD_DIM, lane >= HEAD_DIM)
    spare = (HEAD_DIM, 0)
    ones_k = tuple(jnp.where(jnp.logical_and(lane >= spare[h] + 3, lane < spare[h] + 6), 1.0, 0.0)
                   for h in range(2))
    ones_q = tuple(jnp.where(jnp.logical_and(lane >= spare[h], lane < spare[h] + 3), 1.0, 0.0)
                   for h in range(2))

    @pl.when(qi == 0)
    def _():
        def prep(ci, carry):
            r0 = pl.multiple_of(ci * tk, tk)
            f = fc_ref[pl.ds(r0, tk), :]
            k = k_ref[pl.ds(r0, tk), :]
            for h in range(2):
                aux = _bias_lanes(f, 2 * hp + h, spare[h], -1.0) + ones_k[h]
                kp_ref[h, pl.ds(r0, tk), :] = jnp.where(own[h], k, aux.astype(BF16))
            vt_ref[:, pl.ds(r0, tk)] = v_ref[pl.ds(r0, tk), :].astype(F32).T.astype(BF16)
            return carry
        lax.fori_loop(0, seq // tk, prep, 0)

    q0 = pl.multiple_of(qi * tq, tq)
    fq = fc_ref[pl.ds(q0, tq), :]
    q = q_ref[...] * jnp.asarray(HEAD_DIM ** -0.5, BF16)
    qp = []
    for h in range(2):
        aux = _bias_lanes(fq, 2 * hp + h, spare[h] + 3, 1.0) + ones_q[h]
        qp.append(jnp.where(own[h], q, aux.astype(BF16)))

    m_ref[...] = jnp.full(m_ref.shape, -jnp.inf, F32)
    l_ref[...] = jnp.zeros(l_ref.shape, F32)
    acc_ref[...] = jnp.zeros(acc_ref.shape, F32)

    def scores(kt, s_ref):
        k0 = pl.multiple_of(kt * tk, tk)
        for h in range(2):
            s_ref[h] = _dot_nt(kp_ref[h, pl.ds(k0, tk), :], qp[h])

    def update(kt, s_ref, masked):
        k0 = pl.multiple_of(kt * tk, tk)
        for h in range(2):
            s = s_ref[h]
            if masked:
                kr = lax.broadcasted_iota(jnp.int32, (tk, tq), 0)
                qc = lax.broadcasted_iota(jnp.int32, (tk, tq), 1)
                s = jnp.where(kr <= qc, s, -jnp.inf)
            m_old = m_ref[h]
            m_new = jnp.maximum(m_old, jnp.max(s, axis=0, keepdims=True))
            alpha = jnp.exp(m_old - m_new)
            p = jnp.exp(s - m_new)
            l_ref[h] = l_ref[h] * alpha + jnp.sum(p, axis=0, keepdims=True)
            m_ref[h] = m_new
            rows = pl.ds(h * HEAD_DIM, HEAD_DIM)
            pv = jnp.dot(vt_ref[rows, pl.ds(k0, tk)], p.astype(BF16), preferred_element_type=F32)
            acc_ref[rows, :] = acc_ref[rows, :] * alpha + pv

    scores(0, sa_ref)

    def pair(j, carry):
        scores(2 * j + 1, sb_ref)
        update(2 * j, sa_ref, False)
        scores(2 * j + 2, sa_ref)
        update(2 * j + 1, sb_ref, False)
        return carry

    lax.fori_loop(0, qi // 2, pair, 0)

    @pl.when(qi % 2 == 0)
    def _():
        update(qi, sa_ref, True)

    @pl.when(qi % 2 == 1)
    def _():
        scores(qi, sb_ref)
        update(qi - 1, sa_ref, False)
        update(qi, sb_ref, True)

    out_t = jnp.concatenate([acc_ref[pl.ds(h * HEAD_DIM, HEAD_DIM), :] / l_ref[h] for h in range(2)],
                            axis=0)
    o_ref[...] = out_t.T.astype(o_ref.dtype)


def _attn(proj, fcol, bsz, seq, q_col0, k_col0, v_col0):
    tq = tk = 512
    nq = seq // tq
    n_pairs = N_HEADS * HEAD_DIM // LANES
    qb, kb, vb = q_col0 // LANES, k_col0 // LANES, v_col0 // LANES
    return pl.pallas_call(
        functools.partial(_attn_kernel, tq, tk),
        grid=(bsz, n_pairs, nq),
        in_specs=[pl.BlockSpec((tq, LANES), lambda b, hp, qi: (b * nq + qi, qb + hp)),
                  pl.BlockSpec((seq, LANES), lambda b, hp, qi: (b, kb + hp)),
                  pl.BlockSpec((seq, LANES), lambda b, hp, qi: (b, vb + hp)),
                  pl.BlockSpec((seq, LANES), lambda b, hp, qi: (b, 0))],
        out_specs=pl.BlockSpec((tq, LANES), lambda b, hp, qi: (b * nq + qi, hp)),
        out_shape=jax.ShapeDtypeStruct((bsz * seq, N_HEADS * HEAD_DIM), BF16),
        scratch_shapes=[pltpu.VMEM((2, seq, LANES), BF16), pltpu.VMEM((LANES, seq), BF16),
                        pltpu.VMEM((2, 1, tq), F32), pltpu.VMEM((2, 1, tq), F32),
                        pltpu.VMEM((LANES, tq), F32),
                        pltpu.VMEM((2, tk, tq), F32), pltpu.VMEM((2, tk, tq), F32)],
        compiler_params=_cparams(("arbitrary", "arbitrary", "arbitrary")),
        name="attn",
    )(proj, proj, proj, fcol)


def _mix_kernel(tiles_per_seq, xc_ref, cb_ref, cc_ref, hxc_ref, hcc_ref, att_ref, gc_ref, ga_ref,
                x_ref, mod_ref, g2_ref, cw_ref, wc_ref, wa_ref, wo_ref, rw_ref, rb_ref,
                x1_ref, h2_ref, lg_ref):
    i = pl.program_id(0)
    u = cc_ref[...].astype(F32) * xc_ref[...].astype(F32)
    hu = hcc_ref[...].astype(F32) * hxc_ref[...].astype(F32)
    hu = jnp.where(i % tiles_per_seq == 0, 0.0, hu)
    nh = hu.shape[0]
    row = lax.broadcasted_iota(jnp.int32, u.shape, 0)
    u1 = jnp.where(row == 0, hu[nh - 1:nh, :], pltpu.roll(u, 1, axis=0))
    u2 = jnp.where(row == 0, hu[nh - 2:nh - 1, :],
                   jnp.where(row == 1, hu[nh - 1:nh, :], pltpu.roll(u, 2, axis=0)))
    conv = cw_ref[0:1, :] * u2 + cw_ref[1:2, :] * u1 + cw_ref[2:3, :] * u
    z = (cb_ref[...].astype(F32) * conv).astype(BF16)
    y_conv = jnp.dot(z, wc_ref[...], preferred_element_type=F32)
    y_attn = jnp.dot(att_ref[...], wa_ref[...], preferred_element_type=F32)
    merged = (jax.nn.sigmoid(gc_ref[...].astype(F32)) * y_conv
              + jax.nn.sigmoid(ga_ref[...].astype(F32)) * y_attn)
    o = jnp.dot(merged.astype(BF16), wo_ref[...], preferred_element_type=F32)
    x1 = x_ref[...] + mod_ref[0, 2:3, :] * o
    x1_ref[...] = x1
    h2 = _rmsnorm_mod(x1, g2_ref[...], mod_ref[0, 4:5, :], mod_ref[0, 3:4, :])
    h2_ref[...] = h2
    lg_ref[...] = jnp.dot(h2, rw_ref[...], precision=lax.Precision.HIGHEST,
                          preferred_element_type=F32) + rb_ref[...]


def _mix(proj, gates, att, x2, mod3, g2, conv_w, wc, wa, wo, rw, rb, seq, d_conv):
    n, d = x2.shape
    tm = 256
    halo = 16
    tiles_per_seq = seq // tm
    d_attn = att.shape[1]

    def resident(shape):
        return pl.BlockSpec(shape, lambda i: (0,) * len(shape), pipeline_mode=pl.Buffered(1))

    return pl.pallas_call(
        functools.partial(_mix_kernel, tiles_per_seq),
        grid=(n // tm,),
        in_specs=[pl.BlockSpec((tm, d_conv), lambda i: (i, 0)),
                  pl.BlockSpec((tm, d_conv), lambda i: (i, 1)),
                  pl.BlockSpec((tm, d_conv), lambda i: (i, 2)),
                  pl.BlockSpec((halo, d_conv), lambda i: (jnp.maximum(i * (tm // halo) - 1, 0), 0)),
                  pl.BlockSpec((halo, d_conv), lambda i: (jnp.maximum(i * (tm // halo) - 1, 0), 2)),
                  pl.BlockSpec((tm, d_attn), lambda i: (i, 0)),
                  pl.BlockSpec((tm, d), lambda i: (i, 0)),
                  pl.BlockSpec((tm, d), lambda i: (i, 1)),
                  pl.BlockSpec((tm, d), lambda i: (i, 0)),
                  pl.BlockSpec((1, 6, d), lambda i: (i // tiles_per_seq, 0, 0)),
                  resident((1, d)),
                  resident((CONV_WIDTH, d_conv)),
                  resident((d_conv, d)),
                  resident((d_attn, d)),
                  resident((d, d)),
                  resident((d, LANES)),
                  resident((1, LANES))],
        out_specs=[pl.BlockSpec((tm, d), lambda i: (i, 0)),
                   pl.BlockSpec((tm, d), lambda i: (i, 0)),
                   pl.BlockSpec((tm, LANES), lambda i: (i, 0))],
        out_shape=[jax.ShapeDtypeStruct((n, d), F32),
                   jax.ShapeDtypeStruct((n, d), F32),
                   jax.ShapeDtypeStruct((n, LANES), F32)],
        compiler_params=_cparams(("arbitrary",)),
        name="mix",
    )(proj, proj, proj, proj, proj, att, gates, gates, x2, mod3, g2.reshape(1, d), conv_w,
      wc, wa, wo, rw, rb)


def _route_kernel(lg_ref, idx_ref, prob_ref, cnt_ref, carry_ref):
    i = pl.program_id(0)

    @pl.when(i == 0)
    def _():
        carry_ref[...] = jnp.zeros_like(carry_ref)

    tt = lg_ref.shape[0]
    lane = lax.broadcasted_iota(jnp.int32, (tt, LANES), 1)
    lane_f = lane.astype(F32)
    l = jnp.where(lane < N_EXPERTS, lg_ref[...], -jnp.inf)
    onehot = jnp.zeros((tt, LANES), F32)
    vals, ids = [], []
    for _ in range(TOP_K):
        m = jnp.max(l, axis=1, keepdims=True)
        idx = jnp.min(jnp.where(l == m, lane_f, float(LANES)), axis=1, keepdims=True)
        sel = lane_f == idx
        vals.append(m)
        ids.append(idx)
        onehot = jnp.where(sel, 1.0, onehot)
        l = jnp.where(sel, -jnp.inf, l)
    es = [jnp.exp(v - vals[0]) for v in vals]
    denom = es[0] + es[1] + es[2] + es[3]
    r = lax.broadcasted_iota(jnp.int32, (tt, tt), 0)
    c = lax.broadcasted_iota(jnp.int32, (tt, tt), 1)
    before = jnp.where(c < r, 1.0, 0.0).astype(BF16)
    cnt_before = jnp.dot(before, onehot.astype(BF16), preferred_element_type=F32) + carry_ref[...]
    idx_out = jnp.zeros((tt, LANES), jnp.int32)
    prob_out = jnp.zeros((tt, LANES), F32)
    for k in range(TOP_K):
        rank = jnp.sum(jnp.where(lane_f == ids[k], cnt_before, 0.0), axis=1, keepdims=True)
        idx_out = jnp.where(lane == k, ids[k].astype(jnp.int32), idx_out)
        idx_out = jnp.where(lane == TOP_K + k, rank.astype(jnp.int32), idx_out)
        prob_out = jnp.where(lane == k, es[k] / denom, prob_out)
    idx_ref[...] = idx_out
    prob_ref[...] = prob_out
    carry_ref[...] = carry_ref[...] + jnp.sum(onehot, axis=0, keepdims=True)
    cnt_ref[...] = carry_ref[...].astype(jnp.int32)


def _route(logits):
    n = logits.shape[0]
    tt = 256
    return pl.pallas_call(
        _route_kernel,
        grid=(n // tt,),
        in_specs=[pl.BlockSpec((tt, LANES), lambda i: (i, 0))],
        out_specs=[pl.BlockSpec((tt, LANES), lambda i: (i, 0)),
                   pl.BlockSpec((tt, LANES), lambda i: (i, 0)),
                   pl.BlockSpec((1, LANES), lambda i: (0, 0))],
        out_shape=[jax.ShapeDtypeStruct((n, LANES), jnp.int32),
                   jax.ShapeDtypeStruct((n, LANES), F32),
                   jax.ShapeDtypeStruct((1, LANES), jnp.int32)],
        scratch_shapes=[pltpu.VMEM((1, LANES), F32)],
        compiler_params=_cparams(("arbitrary",)),
        name="route",
    )(logits)


def _zero_fill_rows(zero_rows, dst_hbm, row0, n_rows, sem, start):
    off = row0
    size = zero_rows.shape[0]
    while size >= ROW_ALIGN:
        @pl.when((n_rows & size) != 0)
        def _(off=off, size=size):
            cp = pltpu.make_async_copy(zero_rows.at[pl.ds(0, size), :],
                                       dst_hbm.at[pl.ds(pl.multiple_of(off, ROW_ALIGN), size), :], sem)
            if start:
                cp.start()
            else:
                cp.wait()
        off = off + (n_rows & size)
        size //= 2


def _dispatch_kernel(gs_ref, cnt_ref, gend_ref, dest_ref, h_hbm, xs_hbm, zbuf, sem):
    j = pl.program_id(0)
    nj = pl.num_programs(0)
    tt4 = dest_ref.shape[2]
    tt = tt4 // TOP_K
    n_total = xs_hbm.shape[0]

    @pl.when(j == 0)
    def _():
        zbuf[...] = jnp.zeros(zbuf.shape, F32)
        tail0 = gend_ref[0]

        def pad_rows(start):
            def group(e, carry):
                cnt = cnt_ref[e]

                def row(r, c2):
                    cp = pltpu.make_async_copy(zbuf.at[pl.ds(0, 1), :],
                                               xs_hbm.at[pl.ds(gs_ref[e] + r, 1), :], sem.at[2])
                    if start:
                        cp.start()
                    else:
                        cp.wait()
                    return c2
                lax.fori_loop(cnt, (cnt + ROW_ALIGN - 1) & (-ROW_ALIGN), row, 0)
                return carry
            lax.fori_loop(0, N_EXPERTS, group, 0)

        for start in (True, False):
            _zero_fill_rows(zbuf, xs_hbm, tail0, n_total - tail0, sem.at[2], start)
            pad_rows(start)

    def tok(r, carry):
        t = j * tt + r
        for k in range(TOP_K):
            d = dest_ref[0, 0, r * TOP_K + k]
            pltpu.make_async_copy(h_hbm.at[pl.ds(t, 1), :], xs_hbm.at[pl.ds(d, 1), :],
                                  sem.at[j % 2]).start()
        return carry
    lax.fori_loop(0, tt, tok, 0)

    def wait_tile(slot):
        pltpu.make_async_copy(h_hbm.at[pl.ds(0, tt4), :], xs_hbm.at[pl.ds(0, tt4), :],
                              sem.at[slot]).wait()

    @pl.when(j > 0)
    def _():
        wait_tile((j - 1) % 2)

    @pl.when(j == nj - 1)
    def _():
        wait_tile(j % 2)


def _dispatch(h2, dest, g_start, cnt, g_end, n_rows_total):
    n, d = h2.shape
    tt = DISP_TOKENS
    dest3 = dest.reshape(n // tt, 1, tt * TOP_K)
    grid_spec = pltpu.PrefetchScalarGridSpec(
        num_scalar_prefetch=3,
        grid=(n // tt,),
        in_specs=[pl.BlockSpec((1, 1, tt * TOP_K), lambda j, gs, ct, ge: (j, 0, 0),
                               memory_space=pltpu.SMEM),
                  pl.BlockSpec(memory_space=pl.ANY)],
        out_specs=pl.BlockSpec(memory_space=pl.ANY),
        scratch_shapes=[pltpu.VMEM((EXP_TM, d), F32), pltpu.SemaphoreType.DMA((3,))],
    )
    return pl.pallas_call(
        _dispatch_kernel,
        grid_spec=grid_spec,
        out_shape=jax.ShapeDtypeStruct((n_rows_total, d), F32),
        compiler_params=_cparams(("arbitrary",)),
        name="dispatch",
    )(g_start, cnt, g_end, dest3, h2)


def _expert_kernel(sbe_ref, sbr_ref, sbn_ref, ns_ref, gend_ref,
                   xs_hbm, wg_ref, wu_ref, wd_ref, bg_ref, bu_ref, bd_ref, ys_hbm,
                   stage, xbf, acc, wgu_ref, wdb_ref, ld_sem, st_sem):
    s = pl.program_id(0)
    c = pl.program_id(1)
    last_c = pl.num_programs(1) - 1
    th = wd_ref.shape[1]
    n_super = ns_ref[0]
    n_blocks = SB_ROWS // EXP_TM

    def x_copy(sb, rb, slot):
        r0 = pl.multiple_of(sbr_ref[sb] + rb * EXP_TM, ROW_ALIGN)
        return pltpu.make_async_copy(xs_hbm.at[pl.ds(r0, EXP_TM), :], stage.at[slot], ld_sem.at[slot])

    def y_copy(sb, rb):
        r0 = pl.multiple_of(sbr_ref[sb] + rb * EXP_TM, ROW_ALIGN)
        return pltpu.make_async_copy(acc.at[pl.ds(rb * EXP_TM, EXP_TM), :],
                                     ys_hbm.at[pl.ds(r0, EXP_TM), :], st_sem.at[rb])

    @pl.when(jnp.logical_and(s == 0, c == 0))
    def _():
        stage[1] = jnp.zeros(stage.shape[1:], F32)
        tail0 = gend_ref[0]
        for start in (True, False):
            _zero_fill_rows(stage.at[1], ys_hbm, tail0, ys_hbm.shape[0] - tail0, st_sem.at[0], start)
        x_copy(0, 0, 0).start()

    @pl.when(s < n_super)
    def _():
        wgu_ref[:, :th] = wg_ref[0].astype(BF16)
        wgu_ref[:, th:] = wu_ref[0].astype(BF16)
        wdb_ref[...] = wd_ref[0].astype(BF16)
        bg = bg_ref[0]
        bu = bu_ref[0]
        bd = bd_ref[0]
        n_blk = sbn_ref[s]
        n_blk_prev = sbn_ref[jnp.maximum(s - 1, 0)]

        for rb in range(n_blocks):
            rows = pl.ds(rb * EXP_TM, EXP_TM)

            @pl.when(jnp.logical_and(c == 0, jnp.logical_and(s > 0, rb < n_blk_prev)))
            def _(rb=rb):
                y_copy(s - 1, rb).wait()

            @pl.when(rb < n_blk)
            def _(rb=rb, rows=rows):
                @pl.when(c == 0)
                def _():
                    if rb + 1 < n_blocks:
                        @pl.when(rb + 1 < n_blk)
                        def _():
                            x_copy(s, rb + 1, (rb + 1) % 2).start()
                    x_copy(s, rb, rb % 2).wait()
                    xbf[rows, :] = stage[rb % 2].astype(BF16)

                gu = jnp.dot(xbf[rows, :], wgu_ref[...], preferred_element_type=F32)
                g = jnp.minimum(gu[:, :th] + bg, SWIGLU_LIMIT)
                u = jnp.clip(gu[:, th:] + bu, -SWIGLU_LIMIT, SWIGLU_LIMIT)
                act = (u + 1.0) * g * jax.nn.sigmoid(SWIGLU_ALPHA * g)
                y = jnp.dot(act.astype(BF16), wdb_ref[...], preferred_element_type=F32)

                @pl.when(c == 0)
                def _():
                    acc[rows, :] = y + bd

                @pl.when(c > 0)
                def _():
                    acc[rows, :] += y

                @pl.when(c == last_c)
                def _():
                    y_copy(s, rb).start()

        @pl.when(c == last_c)
        def _():
            @pl.when(s + 1 < n_super)
            def _():
                x_copy(s + 1, 0, 0).start()

            @pl.when(s + 1 == n_super)
            def _():
                for rb in range(n_blocks):
                    @pl.when(rb < n_blk)
                    def _(rb=rb):
                        y_copy(s, rb).wait()


def _expert(xs, sb_expert, sb_row0, sb_nblk, n_super, g_end, wg, wu, wd, bg, bu, bd):
    n_rows_total, d = xs.shape
    s_max = sb_expert.shape[0]
    d_exp = wg.shape[2]
    n_chunks = d_exp // EXP_TH
    ne = wg.shape[0]

    def e_eff(s, sbe, ns):
        return sbe[jnp.minimum(s, ns[0] - 1)]

    def c_eff(s, c, ns):
        return jnp.where(s < ns[0], c, n_chunks - 1)

    grid_spec = pltpu.PrefetchScalarGridSpec(
        num_scalar_prefetch=5,
        grid=(s_max, n_chunks),
        in_specs=[
            pl.BlockSpec(memory_space=pl.ANY),
            pl.BlockSpec((1, d, EXP_TH),
                         lambda s, c, sbe, sbr, sbn, ns, ge: (e_eff(s, sbe, ns), 0, c_eff(s, c, ns))),
            pl.BlockSpec((1, d, EXP_TH),
                         lambda s, c, sbe, sbr, sbn, ns, ge: (e_eff(s, sbe, ns), 0, c_eff(s, c, ns))),
            pl.BlockSpec((1, EXP_TH, d),
                         lambda s, c, sbe, sbr, sbn, ns, ge: (e_eff(s, sbe, ns), c_eff(s, c, ns), 0)),
            pl.BlockSpec((1, 1, EXP_TH),
                         lambda s, c, sbe, sbr, sbn, ns, ge: (e_eff(s, sbe, ns), 0, c_eff(s, c, ns))),
            pl.BlockSpec((1, 1, EXP_TH),
                         lambda s, c, sbe, sbr, sbn, ns, ge: (e_eff(s, sbe, ns), 0, c_eff(s, c, ns))),
            pl.BlockSpec((1, 1, d), lambda s, c, sbe, sbr, sbn, ns, ge: (e_eff(s, sbe, ns), 0, 0)),
        ],
        out_specs=pl.BlockSpec(memory_space=pl.ANY),
        scratch_shapes=[pltpu.VMEM((2, EXP_TM, d), F32), pltpu.VMEM((SB_ROWS, d), BF16),
                        pltpu.VMEM((SB_ROWS, d), F32),
                        pltpu.VMEM((d, 2 * EXP_TH), BF16), pltpu.VMEM((EXP_TH, d), BF16),
                        pltpu.SemaphoreType.DMA((2,)),
                        pltpu.SemaphoreType.DMA((SB_ROWS // EXP_TM,))],
    )
    return pl.pallas_call(
        _expert_kernel,
        grid_spec=grid_spec,
        out_shape=jax.ShapeDtypeStruct((n_rows_total, d), F32),
        compiler_params=_cparams(("arbitrary", "arbitrary")),
        name="expert",
    )(sb_expert, sb_row0, sb_nblk, n_super, g_end, xs, wg, wu, wd,
      bg.reshape(ne, 1, d_exp), bu.reshape(ne, 1, d_exp), bd.reshape(ne, 1, d))


def _combine_start(y_hbm, pos_ref, stage_slot, sem):
    tt = stage_slot.shape[1]

    def body(r, carry):
        for k in range(TOP_K):
            p = pos_ref[0, 0, r * TOP_K + k]
            pltpu.make_async_copy(y_hbm.at[pl.ds(p, 1), :], stage_slot.at[k, pl.ds(r, 1), :],
                                  sem).start()
        return carry
    lax.fori_loop(0, tt, body, 0)


def _combine_kernel(final_norm, pos_cur_ref, pos_nxt_ref, y_hbm, prob_ref, x1_ref, mod_ref, g_ref,
                    o_ref, stage, sem):
    j = pl.program_id(0)
    nj = pl.num_programs(0)
    tt = stage.shape[2]

    @pl.when(j == 0)
    def _():
        _combine_start(y_hbm, pos_cur_ref, stage.at[0], sem.at[0])

    @pl.when(j + 1 < nj)
    def _():
        nxt = (j + 1) % 2
        _combine_start(y_hbm, pos_nxt_ref, stage.at[nxt], sem.at[nxt])

    cur = j % 2
    for k in range(TOP_K):
        pltpu.make_async_copy(y_hbm.at[pl.ds(0, tt), :], stage.at[cur, k], sem.at[cur]).wait()
    prob = prob_ref[...]
    y = prob[:, 0:1] * stage[cur, 0]
    for k in range(1, TOP_K):
        y = y + prob[:, k:k + 1] * stage[cur, k]
    x2 = x1_ref[...] + mod_ref[0, 5:6, :] * y
    if final_norm:
        r = lax.rsqrt(jnp.mean(x2 * x2, axis=-1, keepdims=True) + RMS_EPS)
        x2 = (x2 * r) * g_ref[...]
    o_ref[...] = x2


def _combine(yb, pos, probs, x1, mod3, g, seq, final_norm):
    n, d = x1.shape
    tt = COMB_ROWS
    nt = n // tt
    tiles_per_seq = seq // tt
    pos3 = pos.reshape(nt, 1, tt * TOP_K)
    return pl.pallas_call(
        functools.partial(_combine_kernel, final_norm),
        grid=(nt,),
        in_specs=[pl.BlockSpec((1, 1, tt * TOP_K), lambda j: (j, 0, 0), memory_space=pltpu.SMEM),
                  pl.BlockSpec((1, 1, tt * TOP_K), lambda j: (jnp.minimum(j + 1, nt - 1), 0, 0),
                               memory_space=pltpu.SMEM),
                  pl.BlockSpec(memory_space=pl.ANY),
                  pl.BlockSpec((tt, LANES), lambda j: (j, 0)),
                  pl.BlockSpec((tt, d), lambda j: (j, 0)),
                  pl.BlockSpec((1, 6, d), lambda j: (j // tiles_per_seq, 0, 0)),
                  pl.BlockSpec((1, d), lambda j: (0, 0))],
        out_specs=pl.BlockSpec((tt, d), lambda j: (j, 0)),
        out_shape=jax.ShapeDtypeStruct((n, d), F32),
        scratch_shapes=[pltpu.VMEM((2, TOP_K, tt, d), F32), pltpu.SemaphoreType.DMA((2,))],
        compiler_params=_cparams(("arbitrary",)),
        name="combine",
    )(pos3, pos3, yb, probs, x1, mod3, g.reshape(1, d))


def _count_le(sorted_ends, q):
    return jnp.sum((sorted_ends[None, :] <= q[:, None]).astype(jnp.int32), axis=1)


def _plan(counts, n_tok):
    s_max = -(-n_tok * TOP_K // SB_ROWS) + N_EXPERTS
    cnt = counts[0, :N_EXPERTS]
    cnt_al = (cnt + ROW_ALIGN - 1) // ROW_ALIGN * ROW_ALIGN
    g_end = jnp.cumsum(cnt_al)
    g_start = g_end - cnt_al
    nsb_e = (cnt + SB_ROWS - 1) // SB_ROWS
    sb_end = jnp.cumsum(nsb_e)
    sb_start = sb_end - nsb_e
    n_super = sb_end[-1]
    s_ids = jnp.arange(s_max, dtype=jnp.int32)
    sb_expert = jnp.minimum(_count_le(sb_end, s_ids), N_EXPERTS - 1)
    j_in = s_ids - sb_start[sb_expert]
    live = s_ids < n_super
    sb_rows = jnp.where(live, jnp.clip(cnt[sb_expert] - j_in * SB_ROWS, 0, SB_ROWS), 0)
    sb_row0 = jnp.where(live, g_start[sb_expert] + j_in * SB_ROWS, 0)
    sb_nblk = (sb_rows + EXP_TM - 1) // EXP_TM
    i32 = lambda a: a.astype(jnp.int32)
    return (i32(g_start), i32(cnt), i32(g_end[-1:]), i32(sb_expert), i32(sb_row0), i32(sb_nblk),
            i32(n_super.reshape(1)))


def _dest_kernel(ir_ref, gs_ref, o_ref):
    ir = ir_ref[...].astype(F32)
    lane = lax.broadcasted_iota(jnp.int32, ir.shape, 1)
    lane_f = lane.astype(F32)
    out = jnp.zeros(ir.shape, F32)
    for k in range(TOP_K):
        e = jnp.sum(jnp.where(lane == k, ir, 0.0), axis=1, keepdims=True)
        r = jnp.sum(jnp.where(lane == TOP_K + k, ir, 0.0), axis=1, keepdims=True)
        g = jnp.sum(jnp.where(lane_f == e, gs_ref[...], 0.0), axis=1, keepdims=True)
        out = jnp.where(lane == k, g + r, out)
    o_ref[...] = out.astype(jnp.int32)


def _dest(idx_rank, g_start):
    n = idx_rank.shape[0]
    tt = 1024
    gs = jnp.pad(g_start.astype(F32), (0, LANES - N_EXPERTS)).reshape(1, LANES)
    return pl.pallas_call(
        _dest_kernel,
        grid=(n // tt,),
        in_specs=[pl.BlockSpec((tt, LANES), lambda i: (i, 0)),
                  pl.BlockSpec((1, LANES), lambda i: (0, 0))],
        out_specs=pl.BlockSpec((tt, LANES), lambda i: (i, 0)),
        out_shape=jax.ShapeDtypeStruct((n, LANES), jnp.int32),
        compiler_params=_cparams(("arbitrary",)),
        name="dest",
    )(idx_rank, gs)


def kernel(x, c, ada_w, ada_b, norm_mix_g, w_in, b_forget, conv_w, w_conv_out, w_attn_out, w_out,
           norm_ffn_g, router_w, router_b, exp_w_gate, exp_b_gate, exp_w_up, exp_b_up,
           exp_w_down, exp_b_down, final_norm_g):
    bsz, seq, d = x.shape
    n = bsz * seq
    d_conv = conv_w.shape[2]
    d_attn = w_attn_out.shape[1]
    n_main = 3 * d_conv + 3 * d_attn
    x2 = x.reshape(n, d)
    out = x2
    for l in range(ada_w.shape[0]):
        mod3 = _mod(c, ada_w[l], ada_b[l]).reshape(bsz, 6, d)
        proj, gates, f_logit = _inproj(out, mod3, norm_mix_g[l], w_in[l].T, n_main, N_HEADS,
                                       w_in.shape[2] - n_main - N_HEADS, seq)
        fcol = _cumf(f_logit, b_forget[l], seq)
        att = _attn(proj, fcol, bsz, seq, 3 * d_conv, 3 * d_conv + d_attn, 3 * d_conv + 2 * d_attn)
        rw = jnp.pad(router_w[l], ((0, 0), (0, LANES - N_EXPERTS)))
        rb = jnp.pad(router_b[l], (0, LANES - N_EXPERTS)).reshape(1, LANES)
        x1, h2, logits = _mix(proj, gates, att, out, mod3, norm_ffn_g[l], conv_w[l],
                              _cast_bf16(w_conv_out[l]), _cast_bf16(w_attn_out[l]), _cast_bf16(w_out[l]),
                              rw, rb, seq, d_conv)
        idx_rank, probs, counts = _route(logits)
        g_start, cnt, g_end, sb_expert, sb_row0, sb_nblk, n_super = _plan(counts, n)
        dest = _dest(idx_rank, g_start)[:, :TOP_K].reshape(-1)
        n_rows_total = n * TOP_K + N_EXPERTS * ROW_ALIGN + EXP_TM
        xs = _dispatch(h2, dest, g_start, cnt, g_end, n_rows_total)
        ys = _expert(xs, sb_expert, sb_row0, sb_nblk, n_super, g_end, exp_w_gate[l], exp_w_up[l],
                     exp_w_down[l], exp_b_gate[l], exp_b_up[l], exp_b_down[l])
        out = _combine(ys, dest, probs, x1, mod3, final_norm_g, seq,
                       final_norm=(l == ada_w.shape[0] - 1))
    return out.reshape(bsz, seq, d)
```

```python
import functools

import jax
import jax.numpy as jnp
from jax import lax
from jax.experimental import pallas as pl
from jax.experimental.pallas import tpu as pltpu

F32 = jnp.float32
BF16 = jnp.bfloat16

N_HEADS = 16
HEAD_DIM = 64
N_EXPERTS = 32
TOP_K = 4
SWIGLU_LIMIT = 7.0
SWIGLU_ALPHA = 1.702
RMS_EPS = 1e-6
CONV_WIDTH = 3

LANES = 128
VMEM_LIMIT = 58 * 1024 * 1024

ROW_ALIGN = 8

SB_ROWS = 1536
EXP_TM = 256
EXP_TH = 256
ZERO_ROWS = 2 * EXP_TM
DISP_TOKENS = 256
COMB_ROWS = 128


def _cparams(sem, vmem=VMEM_LIMIT):
    return pltpu.CompilerParams(dimension_semantics=sem, vmem_limit_bytes=vmem)


def _mod_kernel(cb_ref, w_ref, b_ref, o_ref):
    w = w_ref[...]
    tn = w.shape[1]
    rows = []
    for b in range(cb_ref.shape[0]):
        cv = cb_ref[b]
        ca = cv * jax.nn.sigmoid(cv)
        cols = [jnp.sum(w[:, j * LANES:(j + 1) * LANES] * ca, axis=0, keepdims=True)
                for j in range(tn // LANES)]
        rows.append(jnp.concatenate(cols, axis=1))
    o_ref[...] = jnp.concatenate(rows, axis=0) + b_ref[...]


def _mod(c, w, bias):
    bsz, d = c.shape
    n_out = w.shape[1]
    tn = 1024
    cb = jnp.broadcast_to(c[:, :, None], (bsz, d, LANES))
    return pl.pallas_call(
        _mod_kernel,
        grid=(n_out // tn,),
        in_specs=[pl.BlockSpec((bsz, d, LANES), lambda j: (0, 0, 0)),
                  pl.BlockSpec((d, tn), lambda j: (0, j)),
                  pl.BlockSpec((1, tn), lambda j: (0, j))],
        out_specs=pl.BlockSpec((bsz, tn), lambda j: (0, j)),
        out_shape=jax.ShapeDtypeStruct((bsz, n_out), F32),
        compiler_params=_cparams(("arbitrary",)),
        name="mod",
    )(cb, w, bias.reshape(1, n_out))


def _cast_kernel(x_ref, o_ref):
    o_ref[...] = x_ref[...].astype(o_ref.dtype)


def _cast_bf16(w):
    r, c = w.shape
    tr = 512
    return pl.pallas_call(
        _cast_kernel,
        grid=(r // tr,),
        in_specs=[pl.BlockSpec((tr, c), lambda i: (i, 0))],
        out_specs=pl.BlockSpec((tr, c), lambda i: (i, 0)),
        out_shape=jax.ShapeDtypeStruct((r, c), BF16),
        compiler_params=_cparams(("arbitrary",)),
        name="cast_bf16",
    )(w)


def _rmsnorm_mod(x, g, scale, shift):
    r = lax.rsqrt(jnp.mean(x * x, axis=-1, keepdims=True) + RMS_EPS)
    return (x * r) * g * (1.0 + scale) + shift


def _inproj_kernel(nj_main, x_ref, mod_ref, g_ref, wm_ref, wg_ref, wf_ref,
                   om_ref, og_ref, of_ref, h_ref, wfp_ref):
    j = pl.program_id(1)

    @pl.when(j == 0)
    def _():
        wfp_ref[...] = jnp.zeros(wfp_ref.shape, BF16)
        wfp_ref[0:wf_ref.shape[0], :] = wf_ref[...].astype(BF16)
        rows = 256

        def body(rb, carry):
            r0 = pl.multiple_of(rb * rows, rows)
            h = _rmsnorm_mod(x_ref[pl.ds(r0, rows), :], g_ref[...], mod_ref[0, 1:2, :],
                             mod_ref[0, 0:1, :])
            hb = h.astype(BF16)
            h_ref[pl.ds(r0, rows), :] = hb
            of_ref[pl.ds(r0, rows), :] = _dot_nt(hb, wfp_ref[...])
            return carry

        lax.fori_loop(0, x_ref.shape[0] // rows, body, 0)

    @pl.when(j < nj_main)
    def _():
        om_ref[...] = _dot_nt(h_ref[...], wm_ref[...].astype(BF16)).astype(om_ref.dtype)

    @pl.when(j >= nj_main)
    def _():
        og_ref[...] = _dot_nt(h_ref[...], wg_ref[...].astype(BF16)).astype(og_ref.dtype)


def _inproj(x2, mod3, g, w_in_t, n_main, n_f, n_gates, seq):
    n, d = x2.shape
    tm, tn = 1024, 512
    nj_main = n_main // tn
    nj_g = n_gates // tn
    gate_row0 = n_main + n_f
    tiles_per_seq = seq // tm
    return pl.pallas_call(
        functools.partial(_inproj_kernel, nj_main),
        grid=(n // tm, nj_main + nj_g),
        in_specs=[pl.BlockSpec((tm, d), lambda i, j: (i, 0)),
                  pl.BlockSpec((1, 6, d), lambda i, j: (i // tiles_per_seq, 0, 0)),
                  pl.BlockSpec((1, d), lambda i, j: (0, 0)),
                  pl.BlockSpec((tn, d), lambda i, j: (jnp.minimum(j, nj_main - 1), 0)),
                  pl.BlockSpec((pl.Element(tn), pl.Element(d)),
                               lambda i, j: (pl.multiple_of(gate_row0 + tn * jnp.maximum(j - nj_main, 0), n_f), 0)),
                  pl.BlockSpec((n_f, d), lambda i, j: (n_main // n_f, 0))],
        out_specs=[pl.BlockSpec((tm, tn), lambda i, j: (i, jnp.minimum(j, nj_main - 1))),
                   pl.BlockSpec((tm, tn), lambda i, j: (i, jnp.maximum(j - nj_main, 0))),
                   pl.BlockSpec((tm, LANES), lambda i, j: (i, 0))],
        out_shape=[jax.ShapeDtypeStruct((n, n_main), BF16),
                   jax.ShapeDtypeStruct((n, n_gates), BF16),
                   jax.ShapeDtypeStruct((n, LANES), F32)],
        scratch_shapes=[pltpu.VMEM((tm, d), BF16), pltpu.VMEM((LANES, d), BF16)],
        compiler_params=_cparams(("arbitrary", "arbitrary")),
        name="inproj",
    )(x2, mod3, g.reshape(1, d), w_in_t, w_in_t, w_in_t)


def _cumf_kernel(blocks_per_seq, f_ref, bf_ref, fc_ref, carry_ref):
    i = pl.program_id(0)

    @pl.when(i % blocks_per_seq == 0)
    def _():
        carry_ref[...] = jnp.zeros_like(carry_ref)

    z = f_ref[...] + bf_ref[...]
    lf = -(jnp.maximum(-z, 0.0) + jnp.log1p(jnp.exp(-jnp.abs(z))))
    t = z.shape[0]
    row = lax.broadcasted_iota(jnp.int32, (t, t), 0)
    col = lax.broadcasted_iota(jnp.int32, (t, t), 1)
    tri = (col <= row).astype(F32)
    fb = jnp.dot(tri, lf, precision=lax.Precision.HIGHEST,
                 preferred_element_type=F32) + carry_ref[...]
    fc_ref[...] = fb
    carry_ref[...] = fb[t - 1:t, :]


def _cumf(f_logit, b_forget, seq):
    n = f_logit.shape[0]
    t = 256
    bf = jnp.pad(b_forget, (0, LANES - b_forget.shape[0])).reshape(1, LANES)
    return pl.pallas_call(
        functools.partial(_cumf_kernel, seq // t),
        grid=(n // t,),
        in_specs=[pl.BlockSpec((t, LANES), lambda i: (i, 0)),
                  pl.BlockSpec((1, LANES), lambda i: (0, 0))],
        out_specs=pl.BlockSpec((t, LANES), lambda i: (i, 0)),
        out_shape=jax.ShapeDtypeStruct((n, LANES), F32),
        scratch_shapes=[pltpu.VMEM((1, LANES), F32)],
        compiler_params=_cparams(("arbitrary",)),
        name="cumf",
    )(f_logit, bf)


def _dot_nt(a, b):
    return lax.dot_general(a, b, (((1,), (1,)), ((), ())), preferred_element_type=F32)


def _split3(x):
    hi = x.astype(BF16)
    r1 = x - hi.astype(F32)
    mid = r1.astype(BF16)
    lo = (r1 - mid.astype(F32)).astype(BF16)
    return hi, mid, lo


def _bias_lanes(f, head, lane0, sign):
    r = lax.broadcasted_iota(jnp.int32, (LANES, LANES), 0)
    c = lax.broadcasted_iota(jnp.int32, (LANES, LANES), 1)
    out = None
    for j, part in enumerate(_split3(f)):
        sel = jnp.where(jnp.logical_and(r == head, c == lane0 + j), sign, 0.0).astype(BF16)
        term = jnp.dot(part, sel, preferred_element_type=F32)
        out = term if out is None else out + term
    return out


def _attn_kernel(tq, tk, q_ref, k_ref, v_ref, fc_ref, o_ref, kp_ref, vt_ref, m_ref, l_ref, acc_ref,
                 sa_ref, sb_ref):
    hp = pl.program_id(1)
    qi = pl.program_id(2)
    seq = k_ref.shape[0]
    lane = lax.broadcasted_iota(jnp.int32, (1, LANES), 1)
    own = (lane < HEAD_DIM, lane >= HEAD_DIM)
    spare = (HEAD_DIM, 0)
    ones_k = tuple(jnp.where(jnp.logical_and(lane >= spare[h] + 3, lane < spare[h] + 6), 1.0, 0.0)
                   for h in range(2))
    ones_q = tuple(jnp.where(jnp.logical_and(lane >= spare[h], lane < spare[h] + 3), 1.0, 0.0)
                   for h in range(2))

    @pl.when(qi == 0)
    def _():
        def prep(ci, carry):
            r0 = pl.multiple_of(ci * tk, tk)
            f = fc_ref[pl.ds(r0, tk), :]
            k = k_ref[pl.ds(r0, tk), :]
            for h in range(2):
                aux = _bias_lanes(f, 2 * hp + h, spare[h], -1.0) + ones_k[h]
                kp_ref[h, pl.ds(r0, tk), :] = jnp.where(own[h], k, aux.astype(BF16))
            vt_ref[:, pl.ds(r0, tk)] = v_ref[pl.ds(r0, tk), :].astype(F32).T.astype(BF16)
            return carry
        lax.fori_loop(0, seq // tk, prep, 0)

    q0 = pl.multiple_of(qi * tq, tq)
    fq = fc_ref[pl.ds(q0, tq), :]
    q = q_ref[...] * jnp.asarray(HEAD_DIM ** -0.5, BF16)
    qp = []
    for h in range(2):
        aux = _bias_lanes(fq, 2 * hp + h, spare[h] + 3, 1.0) + ones_q[h]
        qp.append(jnp.where(own[h], q, aux.astype(BF16)))

    m_ref[...] = jnp.full(m_ref.shape, -jnp.inf, F32)
    l_ref[...] = jnp.zeros(l_ref.shape, F32)
    acc_ref[...] = jnp.zeros(acc_ref.shape, F32)

    def scores(kt, s_ref):
        k0 = pl.multiple_of(kt * tk, tk)
        for h in range(2):
            s_ref[h] = _dot_nt(kp_ref[h, pl.ds(k0, tk), :], qp[h])

    def update(kt, s_ref, masked):
        k0 = pl.multiple_of(kt * tk, tk)
        for h in range(2):
            s = s_ref[h]
            if masked:
                kr = lax.broadcasted_iota(jnp.int32, (tk, tq), 0)
                qc = lax.broadcasted_iota(jnp.int32, (tk, tq), 1)
                s = jnp.where(kr <= qc, s, -jnp.inf)
            m_old = m_ref[h]
            m_new = jnp.maximum(m_old, jnp.max(s, axis=0, keepdims=True))
            alpha = jnp.exp(m_old - m_new)
            p = jnp.exp(s - m_new)
            l_ref[h] = l_ref[h] * alpha + jnp.sum(p, axis=0, keepdims=True)
            m_ref[h] = m_new
            rows = pl.ds(h * HEAD_DIM, HEAD_DIM)
            pv = jnp.dot(vt_ref[rows, pl.ds(k0, tk)], p.astype(BF16), preferred_element_type=F32)
            acc_ref[rows, :] = acc_ref[rows, :] * alpha + pv

    scores(0, sa_ref)

    def pair(j, carry):
        scores(2 * j + 1, sb_ref)
        update(2 * j, sa_ref, False)
        scores(2 * j + 2, sa_ref)
        update(2 * j + 1, sb_ref, False)
        return carry

    lax.fori_loop(0, qi // 2, pair, 0)

    @pl.when(qi % 2 == 0)
    def _():
        update(qi, sa_ref, True)

    @pl.when(qi % 2 == 1)
    def _():
        scores(qi, sb_ref)
        update(qi - 1, sa_ref, False)
        update(qi, sb_ref, True)

    out_t = jnp.concatenate([acc_ref[pl.ds(h * HEAD_DIM, HEAD_DIM), :] / l_ref[h] for h in range(2)],
                            axis=0)
    o_ref[...] = out_t.T.astype(o_ref.dtype)


def _attn(proj, fcol, bsz, seq, q_col0, k_col0, v_col0):
    tq = tk = 512
    nq = seq // tq
    n_pairs = N_HEADS * HEAD_DIM // LANES
    qb, kb, vb = q_col0 // LANES, k_col0 // LANES, v_col0 // LANES
    return pl.pallas_call(
        functools.partial(_attn_kernel, tq, tk),
        grid=(bsz, n_pairs, nq),
        in_specs=[pl.BlockSpec((tq, LANES), lambda b, hp, qi: (b * nq + qi, qb + hp)),
                  pl.BlockSpec((seq, LANES), lambda b, hp, qi: (b, kb + hp)),
                  pl.BlockSpec((seq, LANES), lambda b, hp, qi: (b, vb + hp)),
                  pl.BlockSpec((seq, LANES), lambda b, hp, qi: (b, 0))],
        out_specs=pl.BlockSpec((tq, LANES), lambda b, hp, qi: (b * nq + qi, hp)),
        out_shape=jax.ShapeDtypeStruct((bsz * seq, N_HEADS * HEAD_DIM), BF16),
        scratch_shapes=[pltpu.VMEM((2, seq, LANES), BF16), pltpu.VMEM((LANES, seq), BF16),
                        pltpu.VMEM((2, 1, tq), F32), pltpu.VMEM((2, 1, tq), F32),
                        pltpu.VMEM((LANES, tq), F32),
                        pltpu.VMEM((2, tk, tq), F32), pltpu.VMEM((2, tk, tq), F32)],
        compiler_params=_cparams(("arbitrary", "arbitrary", "arbitrary")),
        name="attn",
    )(proj, proj, proj, fcol)


def _mix_kernel(tiles_per_seq, xc_ref, cb_ref, cc_ref, hxc_ref, hcc_ref, att_ref, gc_ref, ga_ref,
                x_ref, mod_ref, g2_ref, cw_ref, wc_ref, wa_ref, wo_ref, rw_ref, rb_ref,
                x1_ref, h2_ref, lg_ref):
    i = pl.program_id(0)
    u = cc_ref[...].astype(F32) * xc_ref[...].astype(F32)
    hu = hcc_ref[...].astype(F32) * hxc_ref[...].astype(F32)
    hu = jnp.where(i % tiles_per_seq == 0, 0.0, hu)
    nh = hu.shape[0]
    row = lax.broadcasted_iota(jnp.int32, u.shape, 0)
    u1 = jnp.where(row == 0, hu[nh - 1:nh, :], pltpu.roll(u, 1, axis=0))
    u2 = jnp.where(row == 0, hu[nh - 2:nh - 1, :],
                   jnp.where(row == 1, hu[nh - 1:nh, :], pltpu.roll(u, 2, axis=0)))
    conv = cw_ref[0:1, :] * u2 + cw_ref[1:2, :] * u1 + cw_ref[2:3, :] * u
    z = (cb_ref[...].astype(F32) * conv).astype(BF16)
    y_conv = jnp.dot(z, wc_ref[...], preferred_element_type=F32)
    y_attn = jnp.dot(att_ref[...], wa_ref[...], preferred_element_type=F32)
    merged = (jax.nn.sigmoid(gc_ref[...].astype(F32)) * y_conv
              + jax.nn.sigmoid(ga_ref[...].astype(F32)) * y_attn)
    o = jnp.dot(merged.astype(BF16), wo_ref[...], preferred_element_type=F32)
    x1 = x_ref[...] + mod_ref[0, 2:3, :] * o
    x1_ref[...] = x1
    h2 = _rmsnorm_mod(x1, g2_ref[...], mod_ref[0, 4:5, :], mod_ref[0, 3:4, :])
    h2_ref[...] = h2
    lg_ref[...] = jnp.dot(h2, rw_ref[...], precision=lax.Precision.HIGHEST,
                          preferred_element_type=F32) + rb_ref[...]


def _mix(proj, gates, att, x2, mod3, g2, conv_w, wc, wa, wo, rw, rb, seq, d_conv):
    n, d = x2.shape
    tm = 256
    halo = 16
    tiles_per_seq = seq // tm
    d_attn = att.shape[1]

    def resident(shape):
        return pl.BlockSpec(shape, lambda i: (0,) * len(shape), pipeline_mode=pl.Buffered(1))

    return pl.pallas_call(
        functools.partial(_mix_kernel, tiles_per_seq),
        grid=(n // tm,),
        in_specs=[pl.BlockSpec((tm, d_conv), lambda i: (i, 0)),
                  pl.BlockSpec((tm, d_conv), lambda i: (i, 1)),
                  pl.BlockSpec((tm, d_conv), lambda i: (i, 2)),
                  pl.BlockSpec((halo, d_conv), lambda i: (jnp.maximum(i * (tm // halo) - 1, 0), 0)),
                  pl.BlockSpec((halo, d_conv), lambda i: (jnp.maximum(i * (tm // halo) - 1, 0), 2)),
                  pl.BlockSpec((tm, d_attn), lambda i: (i, 0)),
                  pl.BlockSpec((tm, d), lambda i: (i, 0)),
                  pl.BlockSpec((tm, d), lambda i: (i, 1)),
                  pl.BlockSpec((tm, d), lambda i: (i, 0)),
                  pl.BlockSpec((1, 6, d), lambda i: (i // tiles_per_seq, 0, 0)),
                  resident((1, d)),
                  resident((CONV_WIDTH, d_conv)),
                  resident((d_conv, d)),
                  resident((d_attn, d)),
                  resident((d, d)),
                  resident((d, LANES)),
                  resident((1, LANES))],
        out_specs=[pl.BlockSpec((tm, d), lambda i: (i, 0)),
                   pl.BlockSpec((tm, d), lambda i: (i, 0)),
                   pl.BlockSpec((tm, LANES), lambda i: (i, 0))],
        out_shape=[jax.ShapeDtypeStruct((n, d), F32),
                   jax.ShapeDtypeStruct((n, d), F32),
                   jax.ShapeDtypeStruct((n, LANES), F32)],
        compiler_params=_cparams(("arbitrary",)),
        name="mix",
    )(proj, proj, proj, proj, proj, att, gates, gates, x2, mod3, g2.reshape(1, d), conv_w,
      wc, wa, wo, rw, rb)


def _route_kernel(lg_ref, idx_ref, prob_ref, cnt_ref, carry_ref):
    i = pl.program_id(0)

    @pl.when(i == 0)
    def _():
        carry_ref[...] = jnp.zeros_like(carry_ref)

    tt = lg_ref.shape[0]
    lane = lax.broadcasted_iota(jnp.int32, (tt, LANES), 1)
    lane_f = lane.astype(F32)
    l = jnp.where(lane < N_EXPERTS, lg_ref[...], -jnp.inf)
    onehot = jnp.zeros((tt, LANES), F32)
    vals, ids = [], []
    for _ in range(TOP_K):
        m = jnp.max(l, axis=1, keepdims=True)
        idx = jnp.min(jnp.where(l == m, lane_f, float(LANES)), axis=1, keepdims=True)
        sel = lane_f == idx
        vals.append(m)
        ids.append(idx)
        onehot = jnp.where(sel, 1.0, onehot)
        l = jnp.where(sel, -jnp.inf, l)
    es = [jnp.exp(v - vals[0]) for v in vals]
    denom = es[0] + es[1] + es[2] + es[3]
    r = lax.broadcasted_iota(jnp.int32, (tt, tt), 0)
    c = lax.broadcasted_iota(jnp.int32, (tt, tt), 1)
    before = jnp.where(c < r, 1.0, 0.0).astype(BF16)
    cnt_before = jnp.dot(before, onehot.astype(BF16), preferred_element_type=F32) + carry_ref[...]
    idx_out = jnp.zeros((tt, LANES), jnp.int32)
    prob_out = jnp.zeros((tt, LANES), F32)
    for k in range(TOP_K):
        rank = jnp.sum(jnp.where(lane_f == ids[k], cnt_before, 0.0), axis=1, keepdims=True)
        idx_out = jnp.where(lane == k, ids[k].astype(jnp.int32), idx_out)
        idx_out = jnp.where(lane == TOP_K + k, rank.astype(jnp.int32), idx_out)
        prob_out = jnp.where(lane == k, es[k] / denom, prob_out)
    idx_ref[...] = idx_out
    prob_ref[...] = prob_out
    carry_ref[...] = carry_ref[...] + jnp.sum(onehot, axis=0, keepdims=True)
    cnt_ref[...] = carry_ref[...].astype(jnp.int32)


def _route(logits):
    n = logits.shape[0]
    tt = 256
    return pl.pallas_call(
        _route_kernel,
        grid=(n // tt,),
        in_specs=[pl.BlockSpec((tt, LANES), lambda i: (i, 0))],
        out_specs=[pl.BlockSpec((tt, LANES), lambda i: (i, 0)),
                   pl.BlockSpec((tt, LANES), lambda i: (i, 0)),
                   pl.BlockSpec((1, LANES), lambda i: (0, 0))],
        out_shape=[jax.ShapeDtypeStruct((n, LANES), jnp.int32),
                   jax.ShapeDtypeStruct((n, LANES), F32),
                   jax.ShapeDtypeStruct((1, LANES), jnp.int32)],
        scratch_shapes=[pltpu.VMEM((1, LANES), F32)],
        compiler_params=_cparams(("arbitrary",)),
        name="route",
    )(logits)


def _zero_fill_rows(zero_rows, dst_hbm, row0, n_rows, sem, start):
    off = row0
    size = zero_rows.shape[0]
    while size >= ROW_ALIGN:
        @pl.when((n_rows & size) != 0)
        def _(off=off, size=size):
            cp = pltpu.make_async_copy(zero_rows.at[pl.ds(0, size), :],
                                       dst_hbm.at[pl.ds(pl.multiple_of(off, ROW_ALIGN), size), :], sem)
            if start:
                cp.start()
            else:
                cp.wait()
        off = off + (n_rows & size)
        size //= 2


def _dispatch_kernel(gs_ref, cnt_ref, gend_ref, dest_ref, h_ref, xs_hbm, zbuf, sem):
    j = pl.program_id(0)
    tt = h_ref.shape[0]
    n_total = xs_hbm.shape[0]

    @pl.when(j == 0)
    def _():
        zbuf[...] = jnp.zeros(zbuf.shape, F32)
        tail0 = gend_ref[0]

        def pad_rows(start):
            def group(e, carry):
                cnt = cnt_ref[e]

                def row(r, c2):
                    cp = pltpu.make_async_copy(zbuf.at[pl.ds(0, 1), :],
                                               xs_hbm.at[pl.ds(gs_ref[e] + r, 1), :], sem.at[2])
                    if start:
                        cp.start()
                    else:
                        cp.wait()
                    return c2
                lax.fori_loop(cnt, (cnt + ROW_ALIGN - 1) & (-ROW_ALIGN), row, 0)
                return carry
            lax.fori_loop(0, N_EXPERTS, group, 0)

        for start in (True, False):
            _zero_fill_rows(zbuf, xs_hbm, tail0, n_total - tail0, sem.at[2], start)
            pad_rows(start)

    def tok(r, carry):
        for k in range(TOP_K):
            d = dest_ref[0, 0, r * TOP_K + k]
            pltpu.make_async_copy(h_ref.at[pl.ds(r, 1), :], xs_hbm.at[pl.ds(d, 1), :],
                                  sem.at[0]).start()
        return carry
    lax.fori_loop(0, tt, tok, 0)

    for k in range(TOP_K):
        pltpu.make_async_copy(h_ref, xs_hbm.at[pl.ds(0, tt), :], sem.at[0]).wait()


def _dispatch(h2, dest, g_start, cnt, g_end, n_rows_total):
    n, d = h2.shape
    tt = DISP_TOKENS
    dest3 = dest.reshape(n // tt, 1, tt * TOP_K)
    grid_spec = pltpu.PrefetchScalarGridSpec(
        num_scalar_prefetch=3,
        grid=(n // tt,),
        in_specs=[pl.BlockSpec((1, 1, tt * TOP_K), lambda j, gs, ct, ge: (j, 0, 0),
                               memory_space=pltpu.SMEM),
                  pl.BlockSpec((tt, d), lambda j, gs, ct, ge: (j, 0))],
        out_specs=pl.BlockSpec(memory_space=pl.ANY),
        scratch_shapes=[pltpu.VMEM((ZERO_ROWS, d), F32), pltpu.SemaphoreType.DMA((3,))],
    )
    return pl.pallas_call(
        _dispatch_kernel,
        grid_spec=grid_spec,
        out_shape=jax.ShapeDtypeStruct((n_rows_total, d), F32),
        compiler_params=_cparams(("arbitrary",)),
        name="dispatch",
    )(g_start, cnt, g_end, dest3, h2)


def _for_blocks(n_blk, body):
    n_quads = lax.shift_right_logical(n_blk, 2)

    def quad(i, carry):
        for u in range(4):
            body(i * 4 + u)
        return carry
    lax.fori_loop(0, n_quads, quad, 0)
    base = n_quads * 4
    rem = n_blk - base

    @pl.when(rem >= 2)
    def _():
        body(base)
        body(base + 1)

    @pl.when(rem % 2 == 1)
    def _():
        body(base + rem - 1)


def _expert_kernel(n_chunks, sbe_ref, sbr_ref, sbn_ref, ns_ref, gend_ref,
                   xs_hbm, wg_ref, wu_ref, wd_ref, bg_ref, bu_ref, bd_ref, ys_hbm,
                   stage, xbf, act, out, wgu_ref, wdb_ref, ld_sem, st_sem):
    s = pl.program_id(0)
    c = pl.program_id(1)
    th = EXP_TH
    n_super = ns_ref[0]
    n_blocks = SB_ROWS // EXP_TM

    def slot_ref(slot):
        return stage.at[pl.ds(slot * EXP_TM, EXP_TM), :]

    def x_copy(sb, rb, slot):
        r0 = pl.multiple_of(sbr_ref[sb] + rb * EXP_TM, ROW_ALIGN)
        return pltpu.make_async_copy(xs_hbm.at[pl.ds(r0, EXP_TM), :], slot_ref(slot), ld_sem.at[slot])

    def y_copy(sb, rb):
        r0 = pl.multiple_of(sbr_ref[sb] + rb * EXP_TM, ROW_ALIGN)
        return pltpu.make_async_copy(out.at[pl.ds(rb * EXP_TM, EXP_TM), :],
                                     ys_hbm.at[pl.ds(r0, EXP_TM), :], st_sem.at[rb])

    @pl.when(jnp.logical_and(s == 0, c == 0))
    def _():
        stage[...] = jnp.zeros(stage.shape, F32)
        tail0 = gend_ref[0]
        for start in (True, False):
            _zero_fill_rows(stage, ys_hbm, tail0, ys_hbm.shape[0] - tail0, st_sem.at[0], start)
        x_copy(0, 0, 0).start()

    @pl.when(s < n_super)
    def _():
        n_blk = sbn_ref[s]
        n_blk_prev = sbn_ref[jnp.maximum(s - 1, 0)]

        @pl.when(c == 0)
        def _():
            for rb in range(n_blocks):
                @pl.when(rb < n_blk)
                def _(rb=rb):
                    if rb + 1 < n_blocks:
                        @pl.when(rb + 1 < n_blk)
                        def _():
                            x_copy(s, rb + 1, (rb + 1) % 2).start()
                    x_copy(s, rb, rb % 2).wait()
                    xbf[pl.ds(rb * EXP_TM, EXP_TM), :] = slot_ref(rb % 2)[...].astype(BF16)

        @pl.when(c < n_chunks)
        def _():
            wgu_ref[:, :th] = wg_ref[0].astype(BF16)
            wgu_ref[:, th:] = wu_ref[0].astype(BF16)
            bg = bg_ref[0]
            bu = bu_ref[0]
            cols = pl.ds(pl.multiple_of(c * th, th), th)

            def act_block(rb):
                rows = pl.ds(pl.multiple_of(rb * EXP_TM, EXP_TM), EXP_TM)
                gu = jnp.dot(xbf[rows, :], wgu_ref[...], preferred_element_type=F32)
                g = jnp.minimum(gu[:, :th] + bg, SWIGLU_LIMIT)
                u = jnp.clip(gu[:, th:] + bu, -SWIGLU_LIMIT, SWIGLU_LIMIT)
                a = (u + 1.0) * g * jax.nn.sigmoid(SWIGLU_ALPHA * g)
                act[rows, cols] = a.astype(BF16)

            _for_blocks(n_blk, act_block)

        @pl.when(c >= n_chunks)
        def _():
            wdb_ref[...] = wd_ref[0].astype(BF16)
            bd = bd_ref[0]
            cols = pl.ds(pl.multiple_of((c - n_chunks) * th, th), th)
            for rb in range(n_blocks):
                @pl.when(jnp.logical_and(c == n_chunks, jnp.logical_and(s > 0, rb < n_blk_prev)))
                def _(rb=rb):
                    y_copy(s - 1, rb).wait()

            def down_block(rb):
                rows = pl.ds(pl.multiple_of(rb * EXP_TM, EXP_TM), EXP_TM)
                out[rows, cols] = jnp.dot(act[rows, :], wdb_ref[...],
                                          preferred_element_type=F32) + bd

            _for_blocks(n_blk, down_block)

            @pl.when(c == 2 * n_chunks - 1)
            def _():
                for rb in range(n_blocks):
                    @pl.when(rb < n_blk)
                    def _(rb=rb):
                        y_copy(s, rb).start()

                @pl.when(s + 1 < n_super)
                def _():
                    x_copy(s + 1, 0, 0).start()

                @pl.when(s + 1 == n_super)
                def _():
                    for rb in range(n_blocks):
                        @pl.when(rb < n_blk)
                        def _(rb=rb):
                            y_copy(s, rb).wait()


def _expert(xs, sb_expert, sb_row0, sb_nblk, n_super, g_end, wg, wu, wd, bg, bu, bd):
    n_rows_total, d = xs.shape
    s_max = sb_expert.shape[0]
    d_exp = wg.shape[2]
    n_chunks = d_exp // EXP_TH
    ne = wg.shape[0]

    def e_eff(s, sbe, ns):
        return sbe[jnp.minimum(s, ns[0] - 1)]

    def c_act(s, c, ns):
        return jnp.where(s < ns[0], jnp.minimum(c, n_chunks - 1), n_chunks - 1)

    def c_down(s, c, ns):
        return jnp.where(s < ns[0], jnp.maximum(c - n_chunks, 0), n_chunks - 1)

    act_w = pl.BlockSpec((1, d, EXP_TH),
                         lambda s, c, sbe, sbr, sbn, ns, ge: (e_eff(s, sbe, ns), 0, c_act(s, c, ns)))
    act_b = pl.BlockSpec((1, 1, EXP_TH),
                         lambda s, c, sbe, sbr, sbn, ns, ge: (e_eff(s, sbe, ns), 0, c_act(s, c, ns)))
    grid_spec = pltpu.PrefetchScalarGridSpec(
        num_scalar_prefetch=5,
        grid=(s_max, 2 * n_chunks),
        in_specs=[
            pl.BlockSpec(memory_space=pl.ANY),
            act_w, act_w,
            pl.BlockSpec((1, d_exp, EXP_TH),
                         lambda s, c, sbe, sbr, sbn, ns, ge: (e_eff(s, sbe, ns), 0, c_down(s, c, ns))),
            act_b, act_b,
            pl.BlockSpec((1, 1, EXP_TH),
                         lambda s, c, sbe, sbr, sbn, ns, ge: (e_eff(s, sbe, ns), 0, c_down(s, c, ns))),
        ],
        out_specs=pl.BlockSpec(memory_space=pl.ANY),
        scratch_shapes=[pltpu.VMEM((ZERO_ROWS, d), F32), pltpu.VMEM((SB_ROWS, d), BF16),
                        pltpu.VMEM((SB_ROWS, d_exp), BF16), pltpu.VMEM((SB_ROWS, d), F32),
                        pltpu.VMEM((d, 2 * EXP_TH), BF16), pltpu.VMEM((d_exp, EXP_TH), BF16),
                        pltpu.SemaphoreType.DMA((2,)),
                        pltpu.SemaphoreType.DMA((SB_ROWS // EXP_TM,))],
    )
    assert d % EXP_TH == 0 and d_exp % EXP_TH == 0 and d // EXP_TH == n_chunks
    assert ZERO_ROWS == 2 * EXP_TM
    return pl.pallas_call(
        functools.partial(_expert_kernel, n_chunks),
        grid_spec=grid_spec,
        out_shape=jax.ShapeDtypeStruct((n_rows_total, d), F32),
        compiler_params=_cparams(("arbitrary", "arbitrary")),
        name="expert",
    )(sb_expert, sb_row0, sb_nblk, n_super, g_end, xs, wg, wu, wd,
      bg.reshape(ne, 1, d_exp), bu.reshape(ne, 1, d_exp), bd.reshape(ne, 1, d))


def _combine_start(y_hbm, pos_ref, stage_slot, sem):
    tt = stage_slot.shape[1]

    def body(r, carry):
        for k in range(TOP_K):
            p = pos_ref[0, 0, r * TOP_K + k]
            pltpu.make_async_copy(y_hbm.at[pl.ds(p, 1), :], stage_slot.at[k, pl.ds(r, 1), :],
                                  sem).start()
        return carry
    lax.fori_loop(0, tt, body, 0)


def _combine_kernel(final_norm, pos_cur_ref, pos_nxt_ref, y_hbm, prob_ref, x1_ref, mod_ref, g_ref,
                    o_ref, stage, sem):
    j = pl.program_id(0)
    nj = pl.num_programs(0)
    tt = stage.shape[2]

    @pl.when(j == 0)
    def _():
        _combine_start(y_hbm, pos_cur_ref, stage.at[0], sem.at[0])

    @pl.when(j + 1 < nj)
    def _():
        nxt = (j + 1) % 2
        _combine_start(y_hbm, pos_nxt_ref, stage.at[nxt], sem.at[nxt])

    cur = j % 2
    for k in range(TOP_K):
        pltpu.make_async_copy(y_hbm.at[pl.ds(0, tt), :], stage.at[cur, k], sem.at[cur]).wait()
    prob = prob_ref[...]
    y = prob[:, 0:1] * stage[cur, 0]
    for k in range(1, TOP_K):
        y = y + prob[:, k:k + 1] * stage[cur, k]
    x2 = x1_ref[...] + mod_ref[0, 5:6, :] * y
    if final_norm:
        r = lax.rsqrt(jnp.mean(x2 * x2, axis=-1, keepdims=True) + RMS_EPS)
        x2 = (x2 * r) * g_ref[...]
    o_ref[...] = x2


def _combine(yb, pos, probs, x1, mod3, g, seq, final_norm):
    n, d = x1.shape
    tt = COMB_ROWS
    nt = n // tt
    tiles_per_seq = seq // tt
    pos3 = pos.reshape(nt, 1, tt * TOP_K)
    return pl.pallas_call(
        functools.partial(_combine_kernel, final_norm),
        grid=(nt,),
        in_specs=[pl.BlockSpec((1, 1, tt * TOP_K), lambda j: (j, 0, 0), memory_space=pltpu.SMEM),
                  pl.BlockSpec((1, 1, tt * TOP_K), lambda j: (jnp.minimum(j + 1, nt - 1), 0, 0),
                               memory_space=pltpu.SMEM),
                  pl.BlockSpec(memory_space=pl.ANY),
                  pl.BlockSpec((tt, LANES), lambda j: (j, 0)),
                  pl.BlockSpec((tt, d), lambda j: (j, 0)),
                  pl.BlockSpec((1, 6, d), lambda j: (j // tiles_per_seq, 0, 0)),
                  pl.BlockSpec((1, d), lambda j: (0, 0))],
        out_specs=pl.BlockSpec((tt, d), lambda j: (j, 0)),
        out_shape=jax.ShapeDtypeStruct((n, d), F32),
        scratch_shapes=[pltpu.VMEM((2, TOP_K, tt, d), F32), pltpu.SemaphoreType.DMA((2,))],
        compiler_params=_cparams(("arbitrary",)),
        name="combine",
    )(pos3, pos3, yb, probs, x1, mod3, g.reshape(1, d))


def _count_le(sorted_ends, q):
    return jnp.sum((sorted_ends[None, :] <= q[:, None]).astype(jnp.int32), axis=1)


def _plan(counts, n_tok):
    s_max = -(-n_tok * TOP_K // SB_ROWS) + N_EXPERTS
    cnt = counts[0, :N_EXPERTS]
    cnt_al = (cnt + ROW_ALIGN - 1) // ROW_ALIGN * ROW_ALIGN
    g_end = jnp.cumsum(cnt_al)
    g_start = g_end - cnt_al
    nsb_e = (cnt + SB_ROWS - 1) // SB_ROWS
    sb_end = jnp.cumsum(nsb_e)
    sb_start = sb_end - nsb_e
    n_super = sb_end[-1]
    s_ids = jnp.arange(s_max, dtype=jnp.int32)
    sb_expert = jnp.minimum(_count_le(sb_end, s_ids), N_EXPERTS - 1)
    j_in = s_ids - sb_start[sb_expert]
    live = s_ids < n_super
    sb_rows = jnp.where(live, jnp.clip(cnt[sb_expert] - j_in * SB_ROWS, 0, SB_ROWS), 0)
    sb_row0 = jnp.where(live, g_start[sb_expert] + j_in * SB_ROWS, 0)
    sb_nblk = (sb_rows + EXP_TM - 1) // EXP_TM
    i32 = lambda a: a.astype(jnp.int32)
    return (i32(g_start), i32(cnt), i32(g_end[-1:]), i32(sb_expert), i32(sb_row0), i32(sb_nblk),
            i32(n_super.reshape(1)))


def _dest_kernel(ir_ref, gs_ref, o_ref):
    ir = ir_ref[...].astype(F32)
    lane = lax.broadcasted_iota(jnp.int32, ir.shape, 1)
    lane_f = lane.astype(F32)
    out = jnp.zeros(ir.shape, F32)
    for k in range(TOP_K):
        e = jnp.sum(jnp.where(lane == k, ir, 0.0), axis=1, keepdims=True)
        r = jnp.sum(jnp.where(lane == TOP_K + k, ir, 0.0), axis=1, keepdims=True)
        g = jnp.sum(jnp.where(lane_f == e, gs_ref[...], 0.0), axis=1, keepdims=True)
        out = jnp.where(lane == k, g + r, out)
    o_ref[...] = out.astype(jnp.int32)


def _dest(idx_rank, g_start):
    n = idx_rank.shape[0]
    tt = 1024
    gs = jnp.pad(g_start.astype(F32), (0, LANES - N_EXPERTS)).reshape(1, LANES)
    return pl.pallas_call(
        _dest_kernel,
        grid=(n // tt,),
        in_specs=[pl.BlockSpec((tt, LANES), lambda i: (i, 0)),
                  pl.BlockSpec((1, LANES), lambda i: (0, 0))],
        out_specs=pl.BlockSpec((tt, LANES), lambda i: (i, 0)),
        out_shape=jax.ShapeDtypeStruct((n, LANES), jnp.int32),
        compiler_params=_cparams(("arbitrary",)),
        name="dest",
    )(idx_rank, gs)


def kernel(x, c, ada_w, ada_b, norm_mix_g, w_in, b_forget, conv_w, w_conv_out, w_attn_out, w_out,
           norm_ffn_g, router_w, router_b, exp_w_gate, exp_b_gate, exp_w_up, exp_b_up,
           exp_w_down, exp_b_down, final_norm_g):
    bsz, seq, d = x.shape
    n = bsz * seq
    d_conv = conv_w.shape[2]
    d_attn = w_attn_out.shape[1]
    n_main = 3 * d_conv + 3 * d_attn
    x2 = x.reshape(n, d)
    out = x2
    for l in range(ada_w.shape[0]):
        mod3 = _mod(c, ada_w[l], ada_b[l]).reshape(bsz, 6, d)
        proj, gates, f_logit = _inproj(out, mod3, norm_mix_g[l], w_in[l].T, n_main, N_HEADS,
                                       w_in.shape[2] - n_main - N_HEADS, seq)
        fcol = _cumf(f_logit, b_forget[l], seq)
        att = _attn(proj, fcol, bsz, seq, 3 * d_conv, 3 * d_conv + d_attn, 3 * d_conv + 2 * d_attn)
        rw = jnp.pad(router_w[l], ((0, 0), (0, LANES - N_EXPERTS)))
        rb = jnp.pad(router_b[l], (0, LANES - N_EXPERTS)).reshape(1, LANES)
        x1, h2, logits = _mix(proj, gates, att, out, mod3, norm_ffn_g[l], conv_w[l],
                              _cast_bf16(w_conv_out[l]), _cast_bf16(w_attn_out[l]), _cast_bf16(w_out[l]),
                              rw, rb, seq, d_conv)
        idx_rank, probs, counts = _route(logits)
        g_start, cnt, g_end, sb_expert, sb_row0, sb_nblk, n_super = _plan(counts, n)
        dest = _dest(idx_rank, g_start)[:, :TOP_K].reshape(-1)
        n_rows_total = n * TOP_K + N_EXPERTS * ROW_ALIGN + EXP_TM
        xs = _dispatch(h2, dest, g_start, cnt, g_end, n_rows_total)
        ys = _expert(xs, sb_expert, sb_row0, sb_nblk, n_super, g_end, exp_w_gate[l], exp_w_up[l],
                     exp_w_down[l], exp_b_gate[l], exp_b_up[l], exp_b_down[l])
        out = _combine(ys, dest, probs, x1, mod3, final_norm_g, seq,
                       final_norm=(l == ada_w.shape[0] - 1))
    return out.reshape(bsz, seq, d)
```

```python
import functools

import jax
import jax.numpy as jnp
from jax import lax
from jax.experimental import pallas as pl
from jax.experimental.pallas import tpu as pltpu

F32 = jnp.float32
BF16 = jnp.bfloat16

N_HEADS = 16
HEAD_DIM = 64
N_EXPERTS = 32
TOP_K = 4
SWIGLU_LIMIT = 7.0
SWIGLU_ALPHA = 1.702
RMS_EPS = 1e-6
CONV_WIDTH = 3

LANES = 128
VMEM_LIMIT = 58 * 1024 * 1024

ROW_ALIGN = 8

SB_ROWS = 1536
EXP_TM = 256
EXP_TH = 256
EXP_TN = 512
ZERO_ROWS = 2 * EXP_TM
DISP_TOKENS = 256
COMB_ROWS = 128


def _cparams(sem, vmem=VMEM_LIMIT):
    return pltpu.CompilerParams(dimension_semantics=sem, vmem_limit_bytes=vmem)


def _mod_kernel(cb_ref, w_ref, b_ref, o_ref):
    w = w_ref[...]
    tn = w.shape[1]
    rows = []
    for b in range(cb_ref.shape[0]):
        cv = cb_ref[b]
        ca = cv * jax.nn.sigmoid(cv)
        cols = [jnp.sum(w[:, j * LANES:(j + 1) * LANES] * ca, axis=0, keepdims=True)
                for j in range(tn // LANES)]
        rows.append(jnp.concatenate(cols, axis=1))
    o_ref[...] = jnp.concatenate(rows, axis=0) + b_ref[...]


def _mod(c, w, bias):
    bsz, d = c.shape
    n_out = w.shape[1]
    tn = 1024
    cb = jnp.broadcast_to(c[:, :, None], (bsz, d, LANES))
    return pl.pallas_call(
        _mod_kernel,
        grid=(n_out // tn,),
        in_specs=[pl.BlockSpec((bsz, d, LANES), lambda j: (0, 0, 0)),
                  pl.BlockSpec((d, tn), lambda j: (0, j)),
                  pl.BlockSpec((1, tn), lambda j: (0, j))],
        out_specs=pl.BlockSpec((bsz, tn), lambda j: (0, j)),
        out_shape=jax.ShapeDtypeStruct((bsz, n_out), F32),
        compiler_params=_cparams(("arbitrary",)),
        name="mod",
    )(cb, w, bias.reshape(1, n_out))


def _cast_kernel(x_ref, o_ref):
    o_ref[...] = x_ref[...].astype(o_ref.dtype)


def _cast_bf16(w):
    r, c = w.shape
    tr = 512
    return pl.pallas_call(
        _cast_kernel,
        grid=(r // tr,),
        in_specs=[pl.BlockSpec((tr, c), lambda i: (i, 0))],
        out_specs=pl.BlockSpec((tr, c), lambda i: (i, 0)),
        out_shape=jax.ShapeDtypeStruct((r, c), BF16),
        compiler_params=_cparams(("arbitrary",)),
        name="cast_bf16",
    )(w)


def _rmsnorm_mod(x, g, scale, shift):
    r = lax.rsqrt(jnp.mean(x * x, axis=-1, keepdims=True) + RMS_EPS)
    return (x * r) * g * (1.0 + scale) + shift


def _inproj_kernel(nj_main, x_ref, mod_ref, g_ref, wm_ref, wg_ref, wf_ref,
                   om_ref, og_ref, of_ref, h_ref, wfp_ref):
    j = pl.program_id(1)

    @pl.when(j == 0)
    def _():
        wfp_ref[...] = jnp.zeros(wfp_ref.shape, BF16)
        wfp_ref[0:wf_ref.shape[0], :] = wf_ref[...].astype(BF16)
        rows = 256

        def body(rb, carry):
            r0 = pl.multiple_of(rb * rows, rows)
            h = _rmsnorm_mod(x_ref[pl.ds(r0, rows), :], g_ref[...], mod_ref[0, 1:2, :],
                             mod_ref[0, 0:1, :])
            hb = h.astype(BF16)
            h_ref[pl.ds(r0, rows), :] = hb
            of_ref[pl.ds(r0, rows), :] = _dot_nt(hb, wfp_ref[...])
            return carry

        lax.fori_loop(0, x_ref.shape[0] // rows, body, 0)

    @pl.when(j < nj_main)
    def _():
        om_ref[...] = _dot_nt(h_ref[...], wm_ref[...].astype(BF16)).astype(om_ref.dtype)

    @pl.when(j >= nj_main)
    def _():
        og_ref[...] = _dot_nt(h_ref[...], wg_ref[...].astype(BF16)).astype(og_ref.dtype)


def _inproj(x2, mod3, g, w_in_t, n_main, n_f, n_gates, seq):
    n, d = x2.shape
    tm, tn = 1024, 512
    nj_main = n_main // tn
    nj_g = n_gates // tn
    gate_row0 = n_main + n_f
    tiles_per_seq = seq // tm
    return pl.pallas_call(
        functools.partial(_inproj_kernel, nj_main),
        grid=(n // tm, nj_main + nj_g),
        in_specs=[pl.BlockSpec((tm, d), lambda i, j: (i, 0)),
                  pl.BlockSpec((1, 6, d), lambda i, j: (i // tiles_per_seq, 0, 0)),
                  pl.BlockSpec((1, d), lambda i, j: (0, 0)),
                  pl.BlockSpec((tn, d), lambda i, j: (jnp.minimum(j, nj_main - 1), 0)),
                  pl.BlockSpec((pl.Element(tn), pl.Element(d)),
                               lambda i, j: (pl.multiple_of(gate_row0 + tn * jnp.maximum(j - nj_main, 0), n_f), 0)),
                  pl.BlockSpec((n_f, d), lambda i, j: (n_main // n_f, 0))],
        out_specs=[pl.BlockSpec((tm, tn), lambda i, j: (i, jnp.minimum(j, nj_main - 1))),
                   pl.BlockSpec((tm, tn), lambda i, j: (i, jnp.maximum(j - nj_main, 0))),
                   pl.BlockSpec((tm, LANES), lambda i, j: (i, 0))],
        out_shape=[jax.ShapeDtypeStruct((n, n_main), BF16),
                   jax.ShapeDtypeStruct((n, n_gates), BF16),
                   jax.ShapeDtypeStruct((n, LANES), F32)],
        scratch_shapes=[pltpu.VMEM((tm, d), BF16), pltpu.VMEM((LANES, d), BF16)],
        compiler_params=_cparams(("arbitrary", "arbitrary")),
        name="inproj",
    )(x2, mod3, g.reshape(1, d), w_in_t, w_in_t, w_in_t)


def _cumf_kernel(blocks_per_seq, f_ref, bf_ref, fc_ref, carry_ref):
    i = pl.program_id(0)

    @pl.when(i % blocks_per_seq == 0)
    def _():
        carry_ref[...] = jnp.zeros_like(carry_ref)

    z = f_ref[...] + bf_ref[...]
    lf = -(jnp.maximum(-z, 0.0) + jnp.log1p(jnp.exp(-jnp.abs(z))))
    t = z.shape[0]
    row = lax.broadcasted_iota(jnp.int32, (t, t), 0)
    col = lax.broadcasted_iota(jnp.int32, (t, t), 1)
    tri = jnp.where(col <= row, 1.0, 0.0).astype(BF16)
    parts = [jnp.dot(tri, p, preferred_element_type=F32) for p in _split3(lf)]
    fb = (parts[0] + (parts[1] + parts[2])) + carry_ref[...]
    fc_ref[...] = fb
    carry_ref[...] = fb[t - 1:t, :]


def _cumf(f_logit, b_forget, seq):
    n = f_logit.shape[0]
    t = 256
    bf = jnp.pad(b_forget, (0, LANES - b_forget.shape[0])).reshape(1, LANES)
    return pl.pallas_call(
        functools.partial(_cumf_kernel, seq // t),
        grid=(n // t,),
        in_specs=[pl.BlockSpec((t, LANES), lambda i: (i, 0)),
                  pl.BlockSpec((1, LANES), lambda i: (0, 0))],
        out_specs=pl.BlockSpec((t, LANES), lambda i: (i, 0)),
        out_shape=jax.ShapeDtypeStruct((n, LANES), F32),
        scratch_shapes=[pltpu.VMEM((1, LANES), F32)],
        compiler_params=_cparams(("arbitrary",)),
        name="cumf",
    )(f_logit, bf)


def _dot_nt(a, b):
    return lax.dot_general(a, b, (((1,), (1,)), ((), ())), preferred_element_type=F32)


def _split3(x):
    hi = x.astype(BF16)
    r1 = x - hi.astype(F32)
    mid = r1.astype(BF16)
    lo = (r1 - mid.astype(F32)).astype(BF16)
    return hi, mid, lo


def _bias_lanes(f, head, lane0, sign):
    r = lax.broadcasted_iota(jnp.int32, (LANES, LANES), 0)
    c = lax.broadcasted_iota(jnp.int32, (LANES, LANES), 1)
    out = None
    for j, part in enumerate(_split3(f)):
        sel = jnp.where(jnp.logical_and(r == head, c == lane0 + j), sign, 0.0).astype(BF16)
        term = jnp.dot(part, sel, preferred_element_type=F32)
        out = term if out is None else out + term
    return out


def _attn_kernel(tq, tk, q_ref, k_ref, v_ref, fc_ref, o_ref, kp_ref, qp_ref, vt_ref, m_ref, l_ref,
                 acc_ref, sa_ref, sb_ref):
    hp = pl.program_id(1)
    qi = pl.program_id(2)
    seq = k_ref.shape[0]
    lane = lax.broadcasted_iota(jnp.int32, (1, LANES), 1)
    own = (lane < HEAD_DIM, lane >= HEAD_DIM)
    spare = (HEAD_DIM, 0)
    ones_k = tuple(jnp.where(jnp.logical_and(lane >= spare[h] + 3, lane < spare[h] + 6), 1.0, 0.0)
                   for h in range(2))
    ones_q = tuple(jnp.where(jnp.logical_and(lane >= spare[h], lane < spare[h] + 3), 1.0, 0.0)
                   for h in range(2))

    @pl.when(qi == 0)
    def _():
        def prep(ci, carry):
            r0 = pl.multiple_of(ci * tk, tk)
            f = fc_ref[pl.ds(r0, tk), :]
            k = k_ref[pl.ds(r0, tk), :]
            q = q_ref[pl.ds(r0, tk), :] * jnp.asarray(HEAD_DIM ** -0.5, BF16)
            for h in range(2):
                aux = _bias_lanes(f, 2 * hp + h, spare[h], -1.0) + ones_k[h]
                kp_ref[h, pl.ds(r0, tk), :] = jnp.where(own[h], k, aux.astype(BF16))
                aux = _bias_lanes(f, 2 * hp + h, spare[h] + 3, 1.0) + ones_q[h]
                qp_ref[h, pl.ds(r0, tk), :] = jnp.where(own[h], q, aux.astype(BF16))
            vt_ref[:, pl.ds(r0, tk)] = v_ref[pl.ds(r0, tk), :].astype(F32).T.astype(BF16)
            return carry
        lax.fori_loop(0, seq // tk, prep, 0)

    q0 = pl.multiple_of(qi * tq, tq)
    qp = [qp_ref[h, pl.ds(q0, tq), :] for h in range(2)]

    m_ref[...] = jnp.full(m_ref.shape, -jnp.inf, F32)
    l_ref[...] = jnp.zeros(l_ref.shape, F32)
    acc_ref[...] = jnp.zeros(acc_ref.shape, F32)

    def scores(kt, s_ref):
        k0 = pl.multiple_of(kt * tk, tk)
        for h in range(2):
            s_ref[h] = _dot_nt(kp_ref[h, pl.ds(k0, tk), :], qp[h])

    def update(kt, s_ref, masked):
        k0 = pl.multiple_of(kt * tk, tk)
        for h in range(2):
            s = s_ref[h]
            if masked:
                kr = lax.broadcasted_iota(jnp.int32, (tk, tq), 0)
                qc = lax.broadcasted_iota(jnp.int32, (tk, tq), 1)
                s = jnp.where(kr <= qc, s, -jnp.inf)
            m_old = m_ref[h]
            m_new = jnp.maximum(m_old, jnp.max(s, axis=0, keepdims=True))
            alpha = jnp.exp(m_old - m_new)
            p = jnp.exp(s - m_new)
            l_ref[h] = l_ref[h] * alpha + jnp.sum(p, axis=0, keepdims=True)
            m_ref[h] = m_new
            rows = pl.ds(h * HEAD_DIM, HEAD_DIM)
            pv = jnp.dot(vt_ref[rows, pl.ds(k0, tk)], p.astype(BF16), preferred_element_type=F32)
            acc_ref[rows, :] = acc_ref[rows, :] * alpha + pv

    scores(0, sa_ref)

    def pair(j, carry):
        scores(2 * j + 1, sb_ref)
        update(2 * j, sa_ref, False)
        scores(2 * j + 2, sa_ref)
        update(2 * j + 1, sb_ref, False)
        return carry

    lax.fori_loop(0, qi // 2, pair, 0)

    @pl.when(qi % 2 == 0)
    def _():
        update(qi, sa_ref, True)

    @pl.when(qi % 2 == 1)
    def _():
        scores(qi, sb_ref)
        update(qi - 1, sa_ref, False)
        update(qi, sb_ref, True)

    out_t = jnp.concatenate([acc_ref[pl.ds(h * HEAD_DIM, HEAD_DIM), :] / l_ref[h] for h in range(2)],
                            axis=0)
    o_ref[...] = out_t.T.astype(o_ref.dtype)


def _attn(proj, fcol, bsz, seq, q_col0, k_col0, v_col0):
    tq = tk = 512
    nq = seq // tq
    n_pairs = N_HEADS * HEAD_DIM // LANES
    qb, kb, vb = q_col0 // LANES, k_col0 // LANES, v_col0 // LANES
    return pl.pallas_call(
        functools.partial(_attn_kernel, tq, tk),
        grid=(bsz, n_pairs, nq),
        in_specs=[pl.BlockSpec((seq, LANES), lambda b, hp, qi: (b, qb + hp)),
                  pl.BlockSpec((seq, LANES), lambda b, hp, qi: (b, kb + hp)),
                  pl.BlockSpec((seq, LANES), lambda b, hp, qi: (b, vb + hp)),
                  pl.BlockSpec((seq, LANES), lambda b, hp, qi: (b, 0))],
        out_specs=pl.BlockSpec((tq, LANES), lambda b, hp, qi: (b * nq + qi, hp)),
        out_shape=jax.ShapeDtypeStruct((bsz * seq, N_HEADS * HEAD_DIM), BF16),
        scratch_shapes=[pltpu.VMEM((2, seq, LANES), BF16), pltpu.VMEM((2, seq, LANES), BF16),
                        pltpu.VMEM((LANES, seq), BF16),
                        pltpu.VMEM((2, 1, tq), F32), pltpu.VMEM((2, 1, tq), F32),
                        pltpu.VMEM((LANES, tq), F32),
                        pltpu.VMEM((2, tk, tq), F32), pltpu.VMEM((2, tk, tq), F32)],
        compiler_params=_cparams(("arbitrary", "arbitrary", "arbitrary")),
        name="attn",
    )(proj, proj, proj, fcol)


def _mix_kernel(tiles_per_seq, xc_ref, cb_ref, cc_ref, hxc_ref, hcc_ref, att_ref, gc_ref, ga_ref,
                x_ref, mod_ref, g2_ref, cw_ref, wc_ref, wa_ref, wo_ref, rw_ref, rb_ref,
                x1_ref, h2_ref, lg_ref):
    i = pl.program_id(0)
    u = cc_ref[...].astype(F32) * xc_ref[...].astype(F32)
    hu = hcc_ref[...].astype(F32) * hxc_ref[...].astype(F32)
    hu = jnp.where(i % tiles_per_seq == 0, 0.0, hu)
    nh = hu.shape[0]
    row = lax.broadcasted_iota(jnp.int32, u.shape, 0)
    u1 = jnp.where(row == 0, hu[nh - 1:nh, :], pltpu.roll(u, 1, axis=0))
    u2 = jnp.where(row == 0, hu[nh - 2:nh - 1, :],
                   jnp.where(row == 1, hu[nh - 1:nh, :], pltpu.roll(u, 2, axis=0)))
    conv = cw_ref[0:1, :] * u2 + cw_ref[1:2, :] * u1 + cw_ref[2:3, :] * u
    z = (cb_ref[...].astype(F32) * conv).astype(BF16)
    y_conv = jnp.dot(z, wc_ref[...], preferred_element_type=F32)
    y_attn = jnp.dot(att_ref[...], wa_ref[...], preferred_element_type=F32)
    merged = (jax.nn.sigmoid(gc_ref[...].astype(F32)) * y_conv
              + jax.nn.sigmoid(ga_ref[...].astype(F32)) * y_attn)
    o = jnp.dot(merged.astype(BF16), wo_ref[...], preferred_element_type=F32)
    x1 = x_ref[...] + mod_ref[0, 2:3, :] * o
    x1_ref[...] = x1
    h2 = _rmsnorm_mod(x1, g2_ref[...], mod_ref[0, 4:5, :], mod_ref[0, 3:4, :])
    h2_ref[...] = h2
    h_hi, h_lo, _ = _split3(h2)
    w_hi, w_lo, _ = _split3(rw_ref[...])
    lg_ref[...] = (jnp.dot(h_hi, w_hi, preferred_element_type=F32)
                   + (jnp.dot(h_hi, w_lo, preferred_element_type=F32)
                      + jnp.dot(h_lo, w_hi, preferred_element_type=F32))) + rb_ref[...]


def _mix(proj, gates, att, x2, mod3, g2, conv_w, wc, wa, wo, rw, rb, seq, d_conv):
    n, d = x2.shape
    tm = 256
    halo = 16
    tiles_per_seq = seq // tm
    d_attn = att.shape[1]

    def resident(shape):
        return pl.BlockSpec(shape, lambda i: (0,) * len(shape), pipeline_mode=pl.Buffered(1))

    return pl.pallas_call(
        functools.partial(_mix_kernel, tiles_per_seq),
        grid=(n // tm,),
        in_specs=[pl.BlockSpec((tm, d_conv), lambda i: (i, 0)),
                  pl.BlockSpec((tm, d_conv), lambda i: (i, 1)),
                  pl.BlockSpec((tm, d_conv), lambda i: (i, 2)),
                  pl.BlockSpec((halo, d_conv), lambda i: (jnp.maximum(i * (tm // halo) - 1, 0), 0)),
                  pl.BlockSpec((halo, d_conv), lambda i: (jnp.maximum(i * (tm // halo) - 1, 0), 2)),
                  pl.BlockSpec((tm, d_attn), lambda i: (i, 0)),
                  pl.BlockSpec((tm, d), lambda i: (i, 0)),
                  pl.BlockSpec((tm, d), lambda i: (i, 1)),
                  pl.BlockSpec((tm, d), lambda i: (i, 0)),
                  pl.BlockSpec((1, 6, d), lambda i: (i // tiles_per_seq, 0, 0)),
                  resident((1, d)),
                  resident((CONV_WIDTH, d_conv)),
                  resident((d_conv, d)),
                  resident((d_attn, d)),
                  resident((d, d)),
                  resident((d, LANES)),
                  resident((1, LANES))],
        out_specs=[pl.BlockSpec((tm, d), lambda i: (i, 0)),
                   pl.BlockSpec((tm, d), lambda i: (i, 0)),
                   pl.BlockSpec((tm, LANES), lambda i: (i, 0))],
        out_shape=[jax.ShapeDtypeStruct((n, d), F32),
                   jax.ShapeDtypeStruct((n, d), F32),
                   jax.ShapeDtypeStruct((n, LANES), F32)],
        compiler_params=_cparams(("arbitrary",)),
        name="mix",
    )(proj, proj, proj, proj, proj, att, gates, gates, x2, mod3, g2.reshape(1, d), conv_w,
      wc, wa, wo, rw, rb)


def _route_kernel(lg_ref, idx_ref, prob_ref, cnt_ref, carry_ref):
    i = pl.program_id(0)

    @pl.when(i == 0)
    def _():
        carry_ref[...] = jnp.zeros_like(carry_ref)

    tt = lg_ref.shape[0]
    lane = lax.broadcasted_iota(jnp.int32, (tt, LANES), 1)
    lane_f = lane.astype(F32)
    l = jnp.where(lane < N_EXPERTS, lg_ref[...], -jnp.inf)
    onehot = jnp.zeros((tt, LANES), F32)
    vals, ids = [], []
    for _ in range(TOP_K):
        m = jnp.max(l, axis=1, keepdims=True)
        idx = jnp.min(jnp.where(l == m, lane_f, float(LANES)), axis=1, keepdims=True)
        sel = lane_f == idx
        vals.append(m)
        ids.append(idx)
        onehot = jnp.where(sel, 1.0, onehot)
        l = jnp.where(sel, -jnp.inf, l)
    es = [jnp.exp(v - vals[0]) for v in vals]
    denom = es[0] + es[1] + es[2] + es[3]
    r = lax.broadcasted_iota(jnp.int32, (tt, tt), 0)
    c = lax.broadcasted_iota(jnp.int32, (tt, tt), 1)
    before = jnp.where(c < r, 1.0, 0.0).astype(BF16)
    cnt_before = jnp.dot(before, onehot.astype(BF16), preferred_element_type=F32) + carry_ref[...]
    idx_out = jnp.zeros((tt, LANES), jnp.int32)
    prob_out = jnp.zeros((tt, LANES), F32)
    for k in range(TOP_K):
        rank = jnp.sum(jnp.where(lane_f == ids[k], cnt_before, 0.0), axis=1, keepdims=True)
        idx_out = jnp.where(lane == k, ids[k].astype(jnp.int32), idx_out)
        idx_out = jnp.where(lane == TOP_K + k, rank.astype(jnp.int32), idx_out)
        prob_out = jnp.where(lane == k, es[k] / denom, prob_out)
    idx_ref[...] = idx_out
    prob_ref[...] = prob_out
    carry_ref[...] = carry_ref[...] + jnp.sum(onehot, axis=0, keepdims=True)
    cnt_ref[...] = carry_ref[...].astype(jnp.int32)


def _route(logits):
    n = logits.shape[0]
    tt = 256
    return pl.pallas_call(
        _route_kernel,
        grid=(n // tt,),
        in_specs=[pl.BlockSpec((tt, LANES), lambda i: (i, 0))],
        out_specs=[pl.BlockSpec((tt, LANES), lambda i: (i, 0)),
                   pl.BlockSpec((tt, LANES), lambda i: (i, 0)),
                   pl.BlockSpec((1, LANES), lambda i: (0, 0))],
        out_shape=[jax.ShapeDtypeStruct((n, LANES), jnp.int32),
                   jax.ShapeDtypeStruct((n, LANES), F32),
                   jax.ShapeDtypeStruct((1, LANES), jnp.int32)],
        scratch_shapes=[pltpu.VMEM((1, LANES), F32)],
        compiler_params=_cparams(("arbitrary",)),
        name="route",
    )(logits)


def _zero_fill_rows(zero_rows, dst_hbm, row0, n_rows, sem, start):
    off = row0
    size = zero_rows.shape[0]
    while size >= ROW_ALIGN:
        @pl.when((n_rows & size) != 0)
        def _(off=off, size=size):
            cp = pltpu.make_async_copy(zero_rows.at[pl.ds(0, size), :],
                                       dst_hbm.at[pl.ds(pl.multiple_of(off, ROW_ALIGN), size), :], sem)
            if start:
                cp.start()
            else:
                cp.wait()
        off = off + (n_rows & size)
        size //= 2


def _dispatch_kernel(gs_ref, cnt_ref, gend_ref, dest_ref, h_ref, xs_hbm, zbuf, sem):
    j = pl.program_id(0)
    tt = h_ref.shape[0]
    n_total = xs_hbm.shape[0]

    @pl.when(j == 0)
    def _():
        zbuf[...] = jnp.zeros(zbuf.shape, F32)
        tail0 = gend_ref[0]

        def pad_rows(start):
            def group(e, carry):
                cnt = cnt_ref[e]

                def row(r, c2):
                    cp = pltpu.make_async_copy(zbuf.at[pl.ds(0, 1), :],
                                               xs_hbm.at[pl.ds(gs_ref[e] + r, 1), :], sem.at[2])
                    if start:
                        cp.start()
                    else:
                        cp.wait()
                    return c2
                lax.fori_loop(cnt, (cnt + ROW_ALIGN - 1) & (-ROW_ALIGN), row, 0)
                return carry
            lax.fori_loop(0, N_EXPERTS, group, 0)

        for start in (True, False):
            _zero_fill_rows(zbuf, xs_hbm, tail0, n_total - tail0, sem.at[2], start)
            pad_rows(start)

    def tok(r, carry):
        for k in range(TOP_K):
            d = dest_ref[0, 0, r * TOP_K + k]
            pltpu.make_async_copy(h_ref.at[pl.ds(r, 1), :], xs_hbm.at[pl.ds(d, 1), :],
                                  sem.at[0]).start()
        return carry
    lax.fori_loop(0, tt, tok, 0, unroll=4)

    for k in range(TOP_K):
        pltpu.make_async_copy(h_ref, xs_hbm.at[pl.ds(0, tt), :], sem.at[0]).wait()


def _dispatch(h2, dest, g_start, cnt, g_end, n_rows_total):
    n, d = h2.shape
    tt = DISP_TOKENS
    dest3 = dest.reshape(n // tt, 1, tt * TOP_K)
    grid_spec = pltpu.PrefetchScalarGridSpec(
        num_scalar_prefetch=3,
        grid=(n // tt,),
        in_specs=[pl.BlockSpec((1, 1, tt * TOP_K), lambda j, gs, ct, ge: (j, 0, 0),
                               memory_space=pltpu.SMEM),
                  pl.BlockSpec((tt, d), lambda j, gs, ct, ge: (j, 0))],
        out_specs=pl.BlockSpec(memory_space=pl.ANY),
        scratch_shapes=[pltpu.VMEM((ZERO_ROWS, d), F32), pltpu.SemaphoreType.DMA((3,))],
    )
    return pl.pallas_call(
        _dispatch_kernel,
        grid_spec=grid_spec,
        out_shape=jax.ShapeDtypeStruct((n_rows_total, d), F32),
        compiler_params=_cparams(("arbitrary",)),
        name="dispatch",
    )(g_start, cnt, g_end, dest3, h2)


def _for_blocks(n_blk, n_max, pre, body):
    for n in range(4, n_max + 1):
        @pl.when(n_blk == n)
        def _(n=n):
            pre()
            for rb in range(n):
                body(rb)

    @pl.when(n_blk < 4)
    def _():
        pre()

        @pl.when(n_blk >= 2)
        def _():
            body(0)
            body(1)

        @pl.when(n_blk % 2 == 1)
        def _():
            body(n_blk - 1)


def _expert_kernel(n_act, n_down, sbe_ref, sbr_ref, sbn_ref, ns_ref, gend_ref,
                   xs_hbm, wg_ref, wu_ref, wd_ref, bg_ref, bu_ref, bd_ref, ys_hbm,
                   stage, xbf, act, out, wgu_ref, wdb_ref, ld_sem, st_sem):
    s = pl.program_id(0)
    c = pl.program_id(1)
    th = EXP_TH
    n_super = ns_ref[0]
    n_blocks = SB_ROWS // EXP_TM

    def slot_ref(slot):
        return stage.at[pl.ds(slot * EXP_TM, EXP_TM), :]

    def x_copy(sb, rb, slot):
        r0 = pl.multiple_of(sbr_ref[sb] + rb * EXP_TM, ROW_ALIGN)
        return pltpu.make_async_copy(xs_hbm.at[pl.ds(r0, EXP_TM), :], slot_ref(slot), ld_sem.at[slot])

    def y_copy(sb, rb):
        r0 = pl.multiple_of(sbr_ref[sb] + rb * EXP_TM, ROW_ALIGN)
        return pltpu.make_async_copy(out.at[pl.ds(rb * EXP_TM, EXP_TM), :],
                                     ys_hbm.at[pl.ds(r0, EXP_TM), :], st_sem.at[rb])

    @pl.when(jnp.logical_and(s == 0, c == 0))
    def _():
        stage[...] = jnp.zeros(stage.shape, F32)
        tail0 = gend_ref[0]
        for start in (True, False):
            _zero_fill_rows(stage, ys_hbm, tail0, ys_hbm.shape[0] - tail0, st_sem.at[0], start)
        x_copy(0, 0, 0).start()

    @pl.when(s < n_super)
    def _():
        n_blk = sbn_ref[s]
        n_blk_prev = sbn_ref[jnp.maximum(s - 1, 0)]

        @pl.when(c == 0)
        def _():
            for rb in range(n_blocks):
                @pl.when(rb < n_blk)
                def _(rb=rb):
                    if rb + 1 < n_blocks:
                        @pl.when(rb + 1 < n_blk)
                        def _():
                            x_copy(s, rb + 1, (rb + 1) % 2).start()
                    x_copy(s, rb, rb % 2).wait()
                    xbf[pl.ds(rb * EXP_TM, EXP_TM), :] = slot_ref(rb % 2)[...].astype(BF16)

        @pl.when(c < n_act)
        def _():
            bg = bg_ref[0]
            bu = bu_ref[0]
            cols = pl.ds(pl.multiple_of(c * th, th), th)

            def cast_weights():
                wgu_ref[:, :th] = wg_ref[0].astype(BF16)
                wgu_ref[:, th:] = wu_ref[0].astype(BF16)

            def act_block(rb):
                rows = pl.ds(pl.multiple_of(rb * EXP_TM, EXP_TM), EXP_TM)
                gu = jnp.dot(xbf[rows, :], wgu_ref[...], preferred_element_type=F32)
                g = jnp.minimum(gu[:, :th] + bg, SWIGLU_LIMIT)
                u = jnp.clip(gu[:, th:] + bu, -SWIGLU_LIMIT, SWIGLU_LIMIT)
                a = (u + 1.0) * g * jax.nn.sigmoid(SWIGLU_ALPHA * g)
                act[rows, cols] = a.astype(BF16)

            _for_blocks(n_blk, n_blocks, cast_weights, act_block)

        @pl.when(c >= n_act)
        def _():
            bd = bd_ref[0]
            cols = pl.ds(pl.multiple_of((c - n_act) * EXP_TN, EXP_TN), EXP_TN)
            for rb in range(n_blocks):
                @pl.when(jnp.logical_and(c == n_act, jnp.logical_and(s > 0, rb < n_blk_prev)))
                def _(rb=rb):
                    y_copy(s - 1, rb).wait()

            def cast_weights():
                wdb_ref[...] = wd_ref[0].astype(BF16)

            def down_block(rb):
                rows = pl.ds(pl.multiple_of(rb * EXP_TM, EXP_TM), EXP_TM)
                out[rows, cols] = jnp.dot(act[rows, :], wdb_ref[...],
                                          preferred_element_type=F32) + bd

            _for_blocks(n_blk, n_blocks, cast_weights, down_block)

            @pl.when(c == n_act + n_down - 1)
            def _():
                for rb in range(n_blocks):
                    @pl.when(rb < n_blk)
                    def _(rb=rb):
                        y_copy(s, rb).start()

                @pl.when(s + 1 < n_super)
                def _():
                    x_copy(s + 1, 0, 0).start()

                @pl.when(s + 1 == n_super)
                def _():
                    for rb in range(n_blocks):
                        @pl.when(rb < n_blk)
                        def _(rb=rb):
                            y_copy(s, rb).wait()


def _expert(xs, sb_expert, sb_row0, sb_nblk, n_super, g_end, wg, wu, wd, bg, bu, bd):
    n_rows_total, d = xs.shape
    s_max = sb_expert.shape[0]
    d_exp = wg.shape[2]
    assert d_exp % EXP_TH == 0 and d % EXP_TN == 0 and ZERO_ROWS == 2 * EXP_TM
    n_act = d_exp // EXP_TH
    n_down = d // EXP_TN
    ne = wg.shape[0]

    def e_eff(s, sbe, ns):
        return sbe[jnp.minimum(s, ns[0] - 1)]

    def c_act(s, c, ns):
        return jnp.where(s < ns[0], jnp.minimum(c, n_act - 1), n_act - 1)

    def c_down(s, c, ns):
        return jnp.where(s < ns[0], jnp.maximum(c - n_act, 0), n_down - 1)

    act_w = pl.BlockSpec((1, d, EXP_TH),
                         lambda s, c, sbe, sbr, sbn, ns, ge: (e_eff(s, sbe, ns), 0, c_act(s, c, ns)))
    act_b = pl.BlockSpec((1, 1, EXP_TH),
                         lambda s, c, sbe, sbr, sbn, ns, ge: (e_eff(s, sbe, ns), 0, c_act(s, c, ns)))
    grid_spec = pltpu.PrefetchScalarGridSpec(
        num_scalar_prefetch=5,
        grid=(s_max, n_act + n_down),
        in_specs=[
            pl.BlockSpec(memory_space=pl.ANY),
            act_w, act_w,
            pl.BlockSpec((1, d_exp, EXP_TN),
                         lambda s, c, sbe, sbr, sbn, ns, ge: (e_eff(s, sbe, ns), 0, c_down(s, c, ns))),
            act_b, act_b,
            pl.BlockSpec((1, 1, EXP_TN),
                         lambda s, c, sbe, sbr, sbn, ns, ge: (e_eff(s, sbe, ns), 0, c_down(s, c, ns))),
        ],
        out_specs=pl.BlockSpec(memory_space=pl.ANY),
        scratch_shapes=[pltpu.VMEM((ZERO_ROWS, d), F32), pltpu.VMEM((SB_ROWS, d), BF16),
                        pltpu.VMEM((SB_ROWS, d_exp), BF16), pltpu.VMEM((SB_ROWS, d), F32),
                        pltpu.VMEM((d, 2 * EXP_TH), BF16), pltpu.VMEM((d_exp, EXP_TN), BF16),
                        pltpu.SemaphoreType.DMA((2,)),
                        pltpu.SemaphoreType.DMA((SB_ROWS // EXP_TM,))],
    )
    return pl.pallas_call(
        functools.partial(_expert_kernel, n_act, n_down),
        grid_spec=grid_spec,
        out_shape=jax.ShapeDtypeStruct((n_rows_total, d), F32),
        compiler_params=_cparams(("arbitrary", "arbitrary")),
        name="expert",
    )(sb_expert, sb_row0, sb_nblk, n_super, g_end, xs, wg, wu, wd,
      bg.reshape(ne, 1, d_exp), bu.reshape(ne, 1, d_exp), bd.reshape(ne, 1, d))


def _combine_start(y_hbm, pos_ref, stage_slot, sem):
    tt = stage_slot.shape[1]

    def body(r, carry):
        for k in range(TOP_K):
            p = pos_ref[0, 0, r * TOP_K + k]
            pltpu.make_async_copy(y_hbm.at[pl.ds(p, 1), :], stage_slot.at[k, pl.ds(r, 1), :],
                                  sem).start()
        return carry
    lax.fori_loop(0, tt, body, 0, unroll=4)


def _combine_kernel(final_norm, pos_cur_ref, pos_nxt_ref, y_hbm, prob_ref, x1_ref, mod_ref, g_ref,
                    o_ref, stage, sem):
    j = pl.program_id(0)
    nj = pl.num_programs(0)
    tt = stage.shape[2]

    @pl.when(j == 0)
    def _():
        _combine_start(y_hbm, pos_cur_ref, stage.at[0], sem.at[0])

    @pl.when(j + 1 < nj)
    def _():
        nxt = (j + 1) % 2
        _combine_start(y_hbm, pos_nxt_ref, stage.at[nxt], sem.at[nxt])

    cur = j % 2
    for k in range(TOP_K):
        pltpu.make_async_copy(y_hbm.at[pl.ds(0, tt), :], stage.at[cur, k], sem.at[cur]).wait()
    prob = prob_ref[...]
    y = prob[:, 0:1] * stage[cur, 0]
    for k in range(1, TOP_K):
        y = y + prob[:, k:k + 1] * stage[cur, k]
    x2 = x1_ref[...] + mod_ref[0, 5:6, :] * y
    if final_norm:
        r = lax.rsqrt(jnp.mean(x2 * x2, axis=-1, keepdims=True) + RMS_EPS)
        x2 = (x2 * r) * g_ref[...]
    o_ref[...] = x2


def _combine(yb, pos, probs, x1, mod3, g, seq, final_norm):
    n, d = x1.shape
    tt = COMB_ROWS
    nt = n // tt
    tiles_per_seq = seq // tt
    pos3 = pos.reshape(nt, 1, tt * TOP_K)
    return pl.pallas_call(
        functools.partial(_combine_kernel, final_norm),
        grid=(nt,),
        in_specs=[pl.BlockSpec((1, 1, tt * TOP_K), lambda j: (j, 0, 0), memory_space=pltpu.SMEM),
                  pl.BlockSpec((1, 1, tt * TOP_K), lambda j: (jnp.minimum(j + 1, nt - 1), 0, 0),
                               memory_space=pltpu.SMEM),
                  pl.BlockSpec(memory_space=pl.ANY),
                  pl.BlockSpec((tt, LANES), lambda j: (j, 0)),
                  pl.BlockSpec((tt, d), lambda j: (j, 0)),
                  pl.BlockSpec((1, 6, d), lambda j: (j // tiles_per_seq, 0, 0)),
                  pl.BlockSpec((1, d), lambda j: (0, 0))],
        out_specs=pl.BlockSpec((tt, d), lambda j: (j, 0)),
        out_shape=jax.ShapeDtypeStruct((n, d), F32),
        scratch_shapes=[pltpu.VMEM((2, TOP_K, tt, d), F32), pltpu.SemaphoreType.DMA((2,))],
        compiler_params=_cparams(("arbitrary",)),
        name="combine",
    )(pos3, pos3, yb, probs, x1, mod3, g.reshape(1, d))


def _count_le(sorted_ends, q):
    return jnp.sum((sorted_ends[None, :] <= q[:, None]).astype(jnp.int32), axis=1)


def _plan(counts, n_tok):
    s_max = -(-n_tok * TOP_K // SB_ROWS) + N_EXPERTS
    cnt = counts[0, :N_EXPERTS]
    cnt_al = (cnt + ROW_ALIGN - 1) // ROW_ALIGN * ROW_ALIGN
    g_end = jnp.cumsum(cnt_al)
    g_start = g_end - cnt_al
    nsb_e = (cnt + SB_ROWS - 1) // SB_ROWS
    sb_end = jnp.cumsum(nsb_e)
    sb_start = sb_end - nsb_e
    n_super = sb_end[-1]
    s_ids = jnp.arange(s_max, dtype=jnp.int32)
    sb_expert = jnp.minimum(_count_le(sb_end, s_ids), N_EXPERTS - 1)
    j_in = s_ids - sb_start[sb_expert]
    live = s_ids < n_super
    sb_rows = jnp.where(live, jnp.clip(cnt[sb_expert] - j_in * SB_ROWS, 0, SB_ROWS), 0)
    sb_row0 = jnp.where(live, g_start[sb_expert] + j_in * SB_ROWS, 0)
    sb_nblk = (sb_rows + EXP_TM - 1) // EXP_TM
    i32 = lambda a: a.astype(jnp.int32)
    return (i32(g_start), i32(cnt), i32(g_end[-1:]), i32(sb_expert), i32(sb_row0), i32(sb_nblk),
            i32(n_super.reshape(1)))


def _dest_kernel(ir_ref, gs_ref, o_ref):
    ir = ir_ref[...].astype(F32)
    lane = lax.broadcasted_iota(jnp.int32, ir.shape, 1)
    lane_f = lane.astype(F32)
    out = jnp.zeros(ir.shape, F32)
    for k in range(TOP_K):
        e = jnp.sum(jnp.where(lane == k, ir, 0.0), axis=1, keepdims=True)
        r = jnp.sum(jnp.where(lane == TOP_K + k, ir, 0.0), axis=1, keepdims=True)
        g = jnp.sum(jnp.where(lane_f == e, gs_ref[...], 0.0), axis=1, keepdims=True)
        out = jnp.where(lane == k, g + r, out)
    o_ref[...] = out.astype(jnp.int32)


def _dest(idx_rank, g_start):
    n = idx_rank.shape[0]
    tt = 1024
    gs = jnp.pad(g_start.astype(F32), (0, LANES - N_EXPERTS)).reshape(1, LANES)
    return pl.pallas_call(
        _dest_kernel,
        grid=(n // tt,),
        in_specs=[pl.BlockSpec((tt, LANES), lambda i: (i, 0)),
                  pl.BlockSpec((1, LANES), lambda i: (0, 0))],
        out_specs=pl.BlockSpec((tt, LANES), lambda i: (i, 0)),
        out_shape=jax.ShapeDtypeStruct((n, LANES), jnp.int32),
        compiler_params=_cparams(("arbitrary",)),
        name="dest",
    )(idx_rank, gs)


def kernel(x, c, ada_w, ada_b, norm_mix_g, w_in, b_forget, conv_w, w_conv_out, w_attn_out, w_out,
           norm_ffn_g, router_w, router_b, exp_w_gate, exp_b_gate, exp_w_up, exp_b_up,
           exp_w_down, exp_b_down, final_norm_g):
    bsz, seq, d = x.shape
    n = bsz * seq
    d_conv = conv_w.shape[2]
    d_attn = w_attn_out.shape[1]
    n_main = 3 * d_conv + 3 * d_attn
    x2 = x.reshape(n, d)
    out = x2
    for l in range(ada_w.shape[0]):
        mod3 = _mod(c, ada_w[l], ada_b[l]).reshape(bsz, 6, d)
        proj, gates, f_logit = _inproj(out, mod3, norm_mix_g[l], w_in[l].T, n_main, N_HEADS,
                                       w_in.shape[2] - n_main - N_HEADS, seq)
        fcol = _cumf(f_logit, b_forget[l], seq)
        att = _attn(proj, fcol, bsz, seq, 3 * d_conv, 3 * d_conv + d_attn, 3 * d_conv + 2 * d_attn)
        rw = jnp.pad(router_w[l], ((0, 0), (0, LANES - N_EXPERTS)))
        rb = jnp.pad(router_b[l], (0, LANES - N_EXPERTS)).reshape(1, LANES)
        x1, h2, logits = _mix(proj, gates, att, out, mod3, norm_ffn_g[l], conv_w[l],
                              _cast_bf16(w_conv_out[l]), _cast_bf16(w_attn_out[l]), _cast_bf16(w_out[l]),
                              rw, rb, seq, d_conv)
        idx_rank, probs, counts = _route(logits)
        g_start, cnt, g_end, sb_expert, sb_row0, sb_nblk, n_super = _plan(counts, n)
        dest = _dest(idx_rank, g_start)[:, :TOP_K].reshape(-1)
        n_rows_total = n * TOP_K + N_EXPERTS * ROW_ALIGN + EXP_TM
        xs = _dispatch(h2, dest, g_start, cnt, g_end, n_rows_total)
        ys = _expert(xs, sb_expert, sb_row0, sb_nblk, n_super, g_end, exp_w_gate[l], exp_w_up[l],
                     exp_w_down[l], exp_b_gate[l], exp_b_up[l], exp_b_down[l])
        out = _combine(ys, dest, probs, x1, mod3, final_norm_g, seq,
                       final_norm=(l == ada_w.shape[0] - 1))
    return out.reshape(bsz, seq, d)
```

```python
import functools

import jax
import jax.numpy as jnp
from jax import lax
from jax.experimental import pallas as pl
from jax.experimental.pallas import tpu as pltpu

F32 = jnp.float32
BF16 = jnp.bfloat16

N_HEADS = 16
HEAD_DIM = 64
N_EXPERTS = 32
TOP_K = 4
SWIGLU_LIMIT = 7.0
SWIGLU_ALPHA = 1.702
RMS_EPS = 1e-6
CONV_WIDTH = 3

LOG2E = 1.4426950408889634
QK_SCALE = HEAD_DIM ** -0.5 * LOG2E

LANES = 128
VMEM_LIMIT = 58 * 1024 * 1024

ROW_ALIGN = 8

SB_ROWS = 1536
EXP_TM = 256
EXP_TH = 256
EXP_TN = 512
ZERO_ROWS = 2 * EXP_TM
DISP_TOKENS = 256
COMB_ROWS = 128


def _cparams(sem, vmem=VMEM_LIMIT):
    return pltpu.CompilerParams(dimension_semantics=sem, vmem_limit_bytes=vmem)


def _mod_kernel(cb_ref, w_ref, b_ref, o_ref):
    w = w_ref[...]
    tn = w.shape[1]
    rows = []
    for b in range(cb_ref.shape[0]):
        cv = cb_ref[b]
        ca = cv * jax.nn.sigmoid(cv)
        cols = [jnp.sum(w[:, j * LANES:(j + 1) * LANES] * ca, axis=0, keepdims=True)
                for j in range(tn // LANES)]
        rows.append(jnp.concatenate(cols, axis=1))
    o_ref[...] = jnp.concatenate(rows, axis=0) + b_ref[...]


def _mod(c, w, bias):
    bsz, d = c.shape
    n_out = w.shape[1]
    tn = 1024
    cb = jnp.broadcast_to(c[:, :, None], (bsz, d, LANES))
    return pl.pallas_call(
        _mod_kernel,
        grid=(n_out // tn,),
        in_specs=[pl.BlockSpec((bsz, d, LANES), lambda j: (0, 0, 0)),
                  pl.BlockSpec((d, tn), lambda j: (0, j)),
                  pl.BlockSpec((1, tn), lambda j: (0, j))],
        out_specs=pl.BlockSpec((bsz, tn), lambda j: (0, j)),
        out_shape=jax.ShapeDtypeStruct((bsz, n_out), F32),
        compiler_params=_cparams(("arbitrary",)),
        name="mod",
    )(cb, w, bias.reshape(1, n_out))


def _cast_kernel(x_ref, o_ref):
    o_ref[...] = x_ref[...].astype(o_ref.dtype)


def _cast_bf16(w):
    r, c = w.shape
    tr = 512
    return pl.pallas_call(
        _cast_kernel,
        grid=(r // tr,),
        in_specs=[pl.BlockSpec((tr, c), lambda i: (i, 0))],
        out_specs=pl.BlockSpec((tr, c), lambda i: (i, 0)),
        out_shape=jax.ShapeDtypeStruct((r, c), BF16),
        compiler_params=_cparams(("arbitrary",)),
        name="cast_bf16",
    )(w)


def _rmsnorm_mod(x, g, scale, shift):
    r = lax.rsqrt(jnp.mean(x * x, axis=-1, keepdims=True) + RMS_EPS)
    return (x * r) * g * (1.0 + scale) + shift


def _inproj_kernel(nj_main, jq, x_ref, mod_ref, g_ref, wm_ref, wg_ref, wf_ref,
                   om_ref, og_ref, of_ref, h_ref, wfp_ref):
    j = pl.program_id(1)

    @pl.when(j == 0)
    def _():
        wfp_ref[...] = jnp.zeros(wfp_ref.shape, BF16)
        wfp_ref[0:wf_ref.shape[0], :] = wf_ref[...].astype(BF16)
        rows = 256

        def body(rb, carry):
            r0 = pl.multiple_of(rb * rows, rows)
            h = _rmsnorm_mod(x_ref[pl.ds(r0, rows), :], g_ref[...], mod_ref[0, 1:2, :],
                             mod_ref[0, 0:1, :])
            hb = h.astype(BF16)
            h_ref[pl.ds(r0, rows), :] = hb
            of_ref[pl.ds(r0, rows), :] = _dot_nt(hb, wfp_ref[...])
            return carry

        lax.fori_loop(0, x_ref.shape[0] // rows, body, 0)

    @pl.when(j < nj_main)
    def _():
        scale = jnp.where(jnp.logical_and(j >= jq[0], j < jq[1]), QK_SCALE, 1.0)
        om_ref[...] = (_dot_nt(h_ref[...], wm_ref[...].astype(BF16)) * scale).astype(om_ref.dtype)

    @pl.when(j >= nj_main)
    def _():
        og_ref[...] = _dot_nt(h_ref[...], wg_ref[...].astype(BF16)).astype(og_ref.dtype)


def _inproj(x2, mod3, g, w_in_t, n_main, n_f, n_gates, seq, q_cols):
    n, d = x2.shape
    tm, tn = 1024, 512
    nj_main = n_main // tn
    assert q_cols[0] % tn == 0 and q_cols[1] % tn == 0
    jq = (q_cols[0] // tn, q_cols[1] // tn)
    nj_g = n_gates // tn
    gate_row0 = n_main + n_f
    tiles_per_seq = seq // tm
    return pl.pallas_call(
        functools.partial(_inproj_kernel, nj_main, jq),
        grid=(n // tm, nj_main + nj_g),
        in_specs=[pl.BlockSpec((tm, d), lambda i, j: (i, 0)),
                  pl.BlockSpec((1, 6, d), lambda i, j: (i // tiles_per_seq, 0, 0)),
                  pl.BlockSpec((1, d), lambda i, j: (0, 0)),
                  pl.BlockSpec((tn, d), lambda i, j: (jnp.minimum(j, nj_main - 1), 0)),
                  pl.BlockSpec((pl.Element(tn), pl.Element(d)),
                               lambda i, j: (pl.multiple_of(gate_row0 + tn * jnp.maximum(j - nj_main, 0), n_f), 0)),
                  pl.BlockSpec((n_f, d), lambda i, j: (n_main // n_f, 0))],
        out_specs=[pl.BlockSpec((tm, tn), lambda i, j: (i, jnp.minimum(j, nj_main - 1))),
                   pl.BlockSpec((tm, tn), lambda i, j: (i, jnp.maximum(j - nj_main, 0))),
                   pl.BlockSpec((tm, LANES), lambda i, j: (i, 0))],
        out_shape=[jax.ShapeDtypeStruct((n, n_main), BF16),
                   jax.ShapeDtypeStruct((n, n_gates), BF16),
                   jax.ShapeDtypeStruct((n, LANES), F32)],
        scratch_shapes=[pltpu.VMEM((tm, d), BF16), pltpu.VMEM((LANES, d), BF16)],
        compiler_params=_cparams(("arbitrary", "arbitrary")),
        name="inproj",
    )(x2, mod3, g.reshape(1, d), w_in_t, w_in_t, w_in_t)


def _cumf_kernel(blocks_per_seq, f_ref, bf_ref, fc_ref, carry_ref):
    i = pl.program_id(0)

    @pl.when(i % blocks_per_seq == 0)
    def _():
        carry_ref[...] = jnp.zeros_like(carry_ref)

    z = f_ref[...] + bf_ref[...]
    lf = -(jnp.maximum(-z, 0.0) + jnp.log1p(jnp.exp(-jnp.abs(z))))
    t = z.shape[0]
    row = lax.broadcasted_iota(jnp.int32, (t, t), 0)
    col = lax.broadcasted_iota(jnp.int32, (t, t), 1)
    tri = jnp.where(col <= row, 1.0, 0.0).astype(BF16)
    parts = [jnp.dot(tri, p, preferred_element_type=F32) for p in _split3(lf)]
    fb = (parts[0] + (parts[1] + parts[2])) + carry_ref[...]
    fc_ref[...] = fb
    carry_ref[...] = fb[t - 1:t, :]


def _cumf(f_logit, b_forget, seq):
    n = f_logit.shape[0]
    t = 512
    bf = jnp.pad(b_forget, (0, LANES - b_forget.shape[0])).reshape(1, LANES)
    return pl.pallas_call(
        functools.partial(_cumf_kernel, seq // t),
        grid=(n // t,),
        in_specs=[pl.BlockSpec((t, LANES), lambda i: (i, 0)),
                  pl.BlockSpec((1, LANES), lambda i: (0, 0))],
        out_specs=pl.BlockSpec((t, LANES), lambda i: (i, 0)),
        out_shape=jax.ShapeDtypeStruct((n, LANES), F32),
        scratch_shapes=[pltpu.VMEM((1, LANES), F32)],
        compiler_params=_cparams(("arbitrary",)),
        name="cumf",
    )(f_logit, bf)


def _dot_nt(a, b):
    return lax.dot_general(a, b, (((1,), (1,)), ((), ())), preferred_element_type=F32)


def _split3(x):
    hi = x.astype(BF16)
    r1 = x - hi.astype(F32)
    mid = r1.astype(BF16)
    lo = (r1 - mid.astype(F32)).astype(BF16)
    return hi, mid, lo


def _bias_lanes(f, head, lane0, sign):
    r = lax.broadcasted_iota(jnp.int32, (LANES, LANES), 0)
    c = lax.broadcasted_iota(jnp.int32, (LANES, LANES), 1)
    out = None
    for j, part in enumerate(_split3(f)):
        sel = jnp.where(jnp.logical_and(r == head, c == lane0 + j), sign, 0.0).astype(BF16)
        term = jnp.dot(part, sel, preferred_element_type=F32)
        out = term if out is None else out + term
    return out


def _attn_kernel(tq, tk, q_ref, k_ref, v_ref, fc_ref, o_ref, kp_ref, qp_ref, vt_ref, m_ref, l_ref,
                 acc_ref, sa_ref, sb_ref):
    hp = pl.program_id(1)
    qi = pl.program_id(2)
    seq = k_ref.shape[0]
    lane = lax.broadcasted_iota(jnp.int32, (1, LANES), 1)
    own = (lane < HEAD_DIM, lane >= HEAD_DIM)
    spare = (HEAD_DIM, 0)
    ones_k = tuple(jnp.where(jnp.logical_and(lane >= spare[h] + 3, lane < spare[h] + 6), 1.0, 0.0)
                   for h in range(2))
    ones_q = tuple(jnp.where(jnp.logical_and(lane >= spare[h], lane < spare[h] + 3), 1.0, 0.0)
                   for h in range(2))

    @pl.when(qi == 0)
    def _():
        def prep(ci, carry):
            r0 = pl.multiple_of(ci * tk, tk)
            f = fc_ref[pl.ds(r0, tk), :] * LOG2E
            k = k_ref[pl.ds(r0, tk), :]
            q = q_ref[pl.ds(r0, tk), :]
            for h in range(2):
                aux = _bias_lanes(f, 2 * hp + h, spare[h], -1.0) + ones_k[h]
                kp_ref[h, pl.ds(r0, tk), :] = jnp.where(own[h], k, aux.astype(BF16))
                aux = _bias_lanes(f, 2 * hp + h, spare[h] + 3, 1.0) + ones_q[h]
                qp_ref[h, pl.ds(r0, tk), :] = jnp.where(own[h], q, aux.astype(BF16))
            vt_ref[:, pl.ds(r0, tk)] = v_ref[pl.ds(r0, tk), :].astype(F32).T.astype(BF16)
            return carry
        lax.fori_loop(0, seq // tk, prep, 0)

    q0 = pl.multiple_of(qi * tq, tq)
    qp = [qp_ref[h, pl.ds(q0, tq), :] for h in range(2)]

    m_ref[...] = jnp.full(m_ref.shape, -jnp.inf, F32)
    l_ref[...] = jnp.zeros(l_ref.shape, F32)
    acc_ref[...] = jnp.zeros(acc_ref.shape, F32)

    def scores(kt, s_ref):
        k0 = pl.multiple_of(kt * tk, tk)
        for h in range(2):
            s_ref[h] = _dot_nt(kp_ref[h, pl.ds(k0, tk), :], qp[h])

    def update(kt, s_ref, masked):
        k0 = pl.multiple_of(kt * tk, tk)
        for h in range(2):
            s = s_ref[h]
            if masked:
                kr = lax.broadcasted_iota(jnp.int32, (tk, tq), 0)
                qc = lax.broadcasted_iota(jnp.int32, (tk, tq), 1)
                s = jnp.where(kr <= qc, s, -jnp.inf)
            m_old = m_ref[h]
            m_new = jnp.maximum(m_old, jnp.max(s, axis=0, keepdims=True))
            alpha = jnp.exp2(m_old - m_new)
            p = jnp.exp2(s - m_new)
            l_ref[h] = l_ref[h] * alpha + jnp.sum(p, axis=0, keepdims=True)
            m_ref[h] = m_new
            rows = pl.ds(h * HEAD_DIM, HEAD_DIM)
            pv = jnp.dot(vt_ref[rows, pl.ds(k0, tk)], p.astype(BF16), preferred_element_type=F32)
            acc_ref[rows, :] = acc_ref[rows, :] * alpha + pv

    scores(0, sa_ref)

    def pair(j, carry):
        scores(2 * j + 1, sb_ref)
        update(2 * j, sa_ref, False)
        scores(2 * j + 2, sa_ref)
        update(2 * j + 1, sb_ref, False)
        return carry

    lax.fori_loop(0, qi // 2, pair, 0)

    @pl.when(qi % 2 == 0)
    def _():
        update(qi, sa_ref, True)

    @pl.when(qi % 2 == 1)
    def _():
        scores(qi, sb_ref)
        update(qi - 1, sa_ref, False)
        update(qi, sb_ref, True)

    out_t = jnp.concatenate([acc_ref[pl.ds(h * HEAD_DIM, HEAD_DIM), :] / l_ref[h] for h in range(2)],
                            axis=0)
    o_ref[...] = out_t.T.astype(o_ref.dtype)


def _attn(proj, fcol, bsz, seq, q_col0, k_col0, v_col0):
    tq = tk = 512
    nq = seq // tq
    n_pairs = N_HEADS * HEAD_DIM // LANES
    qb, kb, vb = q_col0 // LANES, k_col0 // LANES, v_col0 // LANES
    return pl.pallas_call(
        functools.partial(_attn_kernel, tq, tk),
        grid=(bsz, n_pairs, nq),
        in_specs=[pl.BlockSpec((seq, LANES), lambda b, hp, qi: (b, qb + hp)),
                  pl.BlockSpec((seq, LANES), lambda b, hp, qi: (b, kb + hp)),
                  pl.BlockSpec((seq, LANES), lambda b, hp, qi: (b, vb + hp)),
                  pl.BlockSpec((seq, LANES), lambda b, hp, qi: (b, 0))],
        out_specs=pl.BlockSpec((tq, LANES), lambda b, hp, qi: (b * nq + qi, hp)),
        out_shape=jax.ShapeDtypeStruct((bsz * seq, N_HEADS * HEAD_DIM), BF16),
        scratch_shapes=[pltpu.VMEM((2, seq, LANES), BF16), pltpu.VMEM((2, seq, LANES), BF16),
                        pltpu.VMEM((LANES, seq), BF16),
                        pltpu.VMEM((2, 1, tq), F32), pltpu.VMEM((2, 1, tq), F32),
                        pltpu.VMEM((LANES, tq), F32),
                        pltpu.VMEM((2, tk, tq), F32), pltpu.VMEM((2, tk, tq), F32)],
        compiler_params=_cparams(("arbitrary", "arbitrary", "arbitrary")),
        name="attn",
    )(proj, proj, proj, fcol)


def _mix_kernel(tiles_per_seq, xc_ref, cb_ref, cc_ref, hxc_ref, hcc_ref, att_ref, gc_ref, ga_ref,
                x_ref, mod_ref, g2_ref, cw_ref, wc_ref, wa_ref, wo_ref, rw_ref, rb_ref,
                x1_ref, h2_ref, lg_ref):
    i = pl.program_id(0)
    u = cc_ref[...].astype(F32) * xc_ref[...].astype(F32)
    hu = hcc_ref[...].astype(F32) * hxc_ref[...].astype(F32)
    hu = jnp.where(i % tiles_per_seq == 0, 0.0, hu)
    nh = hu.shape[0]
    row = lax.broadcasted_iota(jnp.int32, u.shape, 0)
    u1 = jnp.where(row == 0, hu[nh - 1:nh, :], pltpu.roll(u, 1, axis=0))
    u2 = jnp.where(row == 0, hu[nh - 2:nh - 1, :],
                   jnp.where(row == 1, hu[nh - 1:nh, :], pltpu.roll(u, 2, axis=0)))
    conv = cw_ref[0:1, :] * u2 + cw_ref[1:2, :] * u1 + cw_ref[2:3, :] * u
    z = (cb_ref[...].astype(F32) * conv).astype(BF16)
    y_conv = jnp.dot(z, wc_ref[...], preferred_element_type=F32)
    y_attn = jnp.dot(att_ref[...], wa_ref[...], preferred_element_type=F32)
    merged = (jax.nn.sigmoid(gc_ref[...].astype(F32)) * y_conv
              + jax.nn.sigmoid(ga_ref[...].astype(F32)) * y_attn)
    o = jnp.dot(merged.astype(BF16), wo_ref[...], preferred_element_type=F32)
    x1 = x_ref[...] + mod_ref[0, 2:3, :] * o
    x1_ref[...] = x1
    h2 = _rmsnorm_mod(x1, g2_ref[...], mod_ref[0, 4:5, :], mod_ref[0, 3:4, :])
    h2_ref[...] = h2
    h_hi, h_lo, _ = _split3(h2)
    w_hi, w_lo, _ = _split3(rw_ref[...])
    lg_ref[...] = (jnp.dot(h_hi, w_hi, preferred_element_type=F32)
                   + (jnp.dot(h_hi, w_lo, preferred_element_type=F32)
                      + jnp.dot(h_lo, w_hi, preferred_element_type=F32))) + rb_ref[...]


def _mix(proj, gates, att, x2, mod3, g2, conv_w, wc, wa, wo, rw, rb, seq, d_conv):
    n, d = x2.shape
    tm = 256
    halo = 16
    tiles_per_seq = seq // tm
    d_attn = att.shape[1]

    def resident(shape):
        return pl.BlockSpec(shape, lambda i: (0,) * len(shape), pipeline_mode=pl.Buffered(1))

    return pl.pallas_call(
        functools.partial(_mix_kernel, tiles_per_seq),
        grid=(n // tm,),
        in_specs=[pl.BlockSpec((tm, d_conv), lambda i: (i, 0)),
                  pl.BlockSpec((tm, d_conv), lambda i: (i, 1)),
                  pl.BlockSpec((tm, d_conv), lambda i: (i, 2)),
                  pl.BlockSpec((halo, d_conv), lambda i: (jnp.maximum(i * (tm // halo) - 1, 0), 0)),
                  pl.BlockSpec((halo, d_conv), lambda i: (jnp.maximum(i * (tm // halo) - 1, 0), 2)),
                  pl.BlockSpec((tm, d_attn), lambda i: (i, 0)),
                  pl.BlockSpec((tm, d), lambda i: (i, 0)),
                  pl.BlockSpec((tm, d), lambda i: (i, 1)),
                  pl.BlockSpec((tm, d), lambda i: (i, 0)),
                  pl.BlockSpec((1, 6, d), lambda i: (i // tiles_per_seq, 0, 0)),
                  resident((1, d)),
                  resident((CONV_WIDTH, d_conv)),
                  resident((d_conv, d)),
                  resident((d_attn, d)),
                  resident((d, d)),
                  resident((d, LANES)),
                  resident((1, LANES))],
        out_specs=[pl.BlockSpec((tm, d), lambda i: (i, 0)),
                   pl.BlockSpec((tm, d), lambda i: (i, 0)),
                   pl.BlockSpec((tm, LANES), lambda i: (i, 0))],
        out_shape=[jax.ShapeDtypeStruct((n, d), F32),
                   jax.ShapeDtypeStruct((n, d), F32),
                   jax.ShapeDtypeStruct((n, LANES), F32)],
        compiler_params=_cparams(("arbitrary",)),
        name="mix",
    )(proj, proj, proj, proj, proj, att, gates, gates, x2, mod3, g2.reshape(1, d), conv_w,
      wc, wa, wo, rw, rb)


def _route_kernel(lg_ref, idx_ref, prob_ref, cnt_ref, carry_ref):
    i = pl.program_id(0)

    @pl.when(i == 0)
    def _():
        carry_ref[...] = jnp.zeros_like(carry_ref)

    tt = lg_ref.shape[0]
    lane = lax.broadcasted_iota(jnp.int32, (tt, LANES), 1)
    lane_f = lane.astype(F32)
    l = jnp.where(lane < N_EXPERTS, lg_ref[...], -jnp.inf)
    onehot = jnp.zeros((tt, LANES), F32)
    vals, ids = [], []
    for _ in range(TOP_K):
        m = jnp.max(l, axis=1, keepdims=True)
        idx = jnp.min(jnp.where(l == m, lane_f, float(LANES)), axis=1, keepdims=True)
        sel = lane_f == idx
        vals.append(m)
        ids.append(idx)
        onehot = jnp.where(sel, 1.0, onehot)
        l = jnp.where(sel, -jnp.inf, l)
    es = [jnp.exp(v - vals[0]) for v in vals]
    denom = es[0] + es[1] + es[2] + es[3]
    r = lax.broadcasted_iota(jnp.int32, (tt, tt), 0)
    c = lax.broadcasted_iota(jnp.int32, (tt, tt), 1)
    before = jnp.where(c < r, 1.0, 0.0).astype(BF16)
    cnt_before = jnp.dot(before, onehot.astype(BF16), preferred_element_type=F32) + carry_ref[...]
    idx_out = jnp.zeros((tt, LANES), jnp.int32)
    prob_out = jnp.zeros((tt, LANES), F32)
    for k in range(TOP_K):
        rank = jnp.sum(jnp.where(lane_f == ids[k], cnt_before, 0.0), axis=1, keepdims=True)
        idx_out = jnp.where(lane == k, ids[k].astype(jnp.int32), idx_out)
        idx_out = jnp.where(lane == TOP_K + k, rank.astype(jnp.int32), idx_out)
        prob_out = jnp.where(lane == k, es[k] / denom, prob_out)
    idx_ref[...] = idx_out
    prob_ref[...] = prob_out
    carry_ref[...] = carry_ref[...] + jnp.sum(onehot, axis=0, keepdims=True)
    cnt_ref[...] = carry_ref[...].astype(jnp.int32)


def _route(logits):
    n = logits.shape[0]
    tt = 512
    return pl.pallas_call(
        _route_kernel,
        grid=(n // tt,),
        in_specs=[pl.BlockSpec((tt, LANES), lambda i: (i, 0))],
        out_specs=[pl.BlockSpec((tt, LANES), lambda i: (i, 0)),
                   pl.BlockSpec((tt, LANES), lambda i: (i, 0)),
                   pl.BlockSpec((1, LANES), lambda i: (0, 0))],
        out_shape=[jax.ShapeDtypeStruct((n, LANES), jnp.int32),
                   jax.ShapeDtypeStruct((n, LANES), F32),
                   jax.ShapeDtypeStruct((1, LANES), jnp.int32)],
        scratch_shapes=[pltpu.VMEM((1, LANES), F32)],
        compiler_params=_cparams(("arbitrary",)),
        name="route",
    )(logits)


def _zero_fill_rows(zero_rows, dst_hbm, row0, n_rows, sem, start):
    off = row0
    size = zero_rows.shape[0]
    while size >= ROW_ALIGN:
        @pl.when((n_rows & size) != 0)
        def _(off=off, size=size):
            cp = pltpu.make_async_copy(zero_rows.at[pl.ds(0, size), :],
                                       dst_hbm.at[pl.ds(pl.multiple_of(off, ROW_ALIGN), size), :], sem)
            if start:
                cp.start()
            else:
                cp.wait()
        off = off + (n_rows & size)
        size //= 2


def _dispatch_kernel(gs_ref, cnt_ref, gend_ref, dest_ref, h_ref, xs_hbm, zbuf, sem):
    j = pl.program_id(0)
    tt = h_ref.shape[0]
    n_total = xs_hbm.shape[0]

    @pl.when(j == 0)
    def _():
        zbuf[...] = jnp.zeros(zbuf.shape, F32)
        tail0 = gend_ref[0]

        def pad_rows(start):
            def group(e, carry):
                cnt = cnt_ref[e]

                def row(r, c2):
                    cp = pltpu.make_async_copy(zbuf.at[pl.ds(0, 1), :],
                                               xs_hbm.at[pl.ds(gs_ref[e] + r, 1), :], sem.at[2])
                    if start:
                        cp.start()
                    else:
                        cp.wait()
                    return c2
                lax.fori_loop(cnt, (cnt + ROW_ALIGN - 1) & (-ROW_ALIGN), row, 0)
                return carry
            lax.fori_loop(0, N_EXPERTS, group, 0)

        for start in (True, False):
            _zero_fill_rows(zbuf, xs_hbm, tail0, n_total - tail0, sem.at[2], start)
            pad_rows(start)

    def tok(r, carry):
        for k in range(TOP_K):
            d = dest_ref[0, 0, r * TOP_K + k]
            pltpu.make_async_copy(h_ref.at[pl.ds(r, 1), :], xs_hbm.at[pl.ds(d, 1), :],
                                  sem.at[0]).start()
        return carry
    lax.fori_loop(0, tt, tok, 0, unroll=4)

    for k in range(TOP_K):
        pltpu.make_async_copy(h_ref, xs_hbm.at[pl.ds(0, tt), :], sem.at[0]).wait()


def _dispatch(h2, dest, g_start, cnt, g_end, n_rows_total):
    n, d = h2.shape
    tt = DISP_TOKENS
    dest3 = dest.reshape(n // tt, 1, tt * TOP_K)
    grid_spec = pltpu.PrefetchScalarGridSpec(
        num_scalar_prefetch=3,
        grid=(n // tt,),
        in_specs=[pl.BlockSpec((1, 1, tt * TOP_K), lambda j, gs, ct, ge: (j, 0, 0),
                               memory_space=pltpu.SMEM),
                  pl.BlockSpec((tt, d), lambda j, gs, ct, ge: (j, 0))],
        out_specs=pl.BlockSpec(memory_space=pl.ANY),
        scratch_shapes=[pltpu.VMEM((ZERO_ROWS, d), F32), pltpu.SemaphoreType.DMA((3,))],
    )
    return pl.pallas_call(
        _dispatch_kernel,
        grid_spec=grid_spec,
        out_shape=jax.ShapeDtypeStruct((n_rows_total, d), F32),
        compiler_params=_cparams(("arbitrary",)),
        name="dispatch",
    )(g_start, cnt, g_end, dest3, h2)


def _for_blocks(n_blk, n_max, pre, body):
    for n in range(4, n_max + 1):
        @pl.when(n_blk == n)
        def _(n=n):
            pre()
            for rb in range(n):
                body(rb)

    @pl.when(n_blk < 4)
    def _():
        pre()

        @pl.when(n_blk >= 2)
        def _():
            body(0)
            body(1)

        @pl.when(n_blk % 2 == 1)
        def _():
            body(n_blk - 1)


def _expert_kernel(n_act, n_down, sbe_ref, sbr_ref, sbn_ref, ns_ref, gend_ref,
                   xs_hbm, wg_ref, wu_ref, wd_ref, bg_ref, bu_ref, bd_ref, ys_hbm,
                   stage, xbf, act, out, wgu_ref, wdb_ref, ld_sem, st_sem):
    s = pl.program_id(0)
    c = pl.program_id(1)
    th = EXP_TH
    n_super = ns_ref[0]
    n_blocks = SB_ROWS // EXP_TM

    def slot_ref(slot):
        return stage.at[pl.ds(slot * EXP_TM, EXP_TM), :]

    def x_copy(sb, rb, slot):
        r0 = pl.multiple_of(sbr_ref[sb] + rb * EXP_TM, ROW_ALIGN)
        return pltpu.make_async_copy(xs_hbm.at[pl.ds(r0, EXP_TM), :], slot_ref(slot), ld_sem.at[slot])

    def y_copy(sb, rb):
        r0 = pl.multiple_of(sbr_ref[sb] + rb * EXP_TM, ROW_ALIGN)
        return pltpu.make_async_copy(out.at[pl.ds(rb * EXP_TM, EXP_TM), :],
                                     ys_hbm.at[pl.ds(r0, EXP_TM), :], st_sem.at[rb])

    @pl.when(jnp.logical_and(s == 0, c == 0))
    def _():
        stage[...] = jnp.zeros(stage.shape, F32)
        tail0 = gend_ref[0]
        for start in (True, False):
            _zero_fill_rows(stage, ys_hbm, tail0, ys_hbm.shape[0] - tail0, st_sem.at[0], start)
        x_copy(0, 0, 0).start()

    @pl.when(s < n_super)
    def _():
        n_blk = sbn_ref[s]
        n_blk_prev = sbn_ref[jnp.maximum(s - 1, 0)]
        expert = sbe_ref[s]

        @pl.when(c == 0)
        def _():
            for rb in range(n_blocks):
                @pl.when(rb < n_blk)
                def _(rb=rb):
                    if rb + 1 < n_blocks:
                        @pl.when(rb + 1 < n_blk)
                        def _():
                            x_copy(s, rb + 1, (rb + 1) % 2).start()
                    x_copy(s, rb, rb % 2).wait()
                    xbf[pl.ds(rb * EXP_TM, EXP_TM), :] = slot_ref(rb % 2)[...].astype(BF16)

        @pl.when(c < n_act)
        def _():
            cols = pl.ds(pl.multiple_of(c * th, th), th)
            bg = bg_ref[pl.ds(expert, 1), cols]
            bu = bu_ref[pl.ds(expert, 1), cols]

            def cast_weights():
                wgu_ref[:, :th] = wg_ref[0].astype(BF16)
                wgu_ref[:, th:] = wu_ref[0].astype(BF16)

            def act_block(rb):
                rows = pl.ds(pl.multiple_of(rb * EXP_TM, EXP_TM), EXP_TM)
                gu = jnp.dot(xbf[rows, :], wgu_ref[...], preferred_element_type=F32)
                g = jnp.minimum(gu[:, :th] + bg, SWIGLU_LIMIT)
                u = jnp.clip(gu[:, th:] + bu, -SWIGLU_LIMIT, SWIGLU_LIMIT)
                a = (u + 1.0) * g * jax.nn.sigmoid(SWIGLU_ALPHA * g)
                act[rows, cols] = a.astype(BF16)

            _for_blocks(n_blk, n_blocks, cast_weights, act_block)

        @pl.when(c >= n_act)
        def _():
            cols = pl.ds(pl.multiple_of((c - n_act) * EXP_TN, EXP_TN), EXP_TN)
            bd = bd_ref[pl.ds(expert, 1), cols]
            for rb in range(n_blocks):
                @pl.when(jnp.logical_and(c == n_act, jnp.logical_and(s > 0, rb < n_blk_prev)))
                def _(rb=rb):
                    y_copy(s - 1, rb).wait()

            def cast_weights():
                wdb_ref[...] = wd_ref[0].astype(BF16)

            def down_block(rb):
                rows = pl.ds(pl.multiple_of(rb * EXP_TM, EXP_TM), EXP_TM)
                out[rows, cols] = jnp.dot(act[rows, :], wdb_ref[...],
                                          preferred_element_type=F32) + bd

            _for_blocks(n_blk, n_blocks, cast_weights, down_block)

            @pl.when(c == n_act + n_down - 1)
            def _():
                for rb in range(n_blocks):
                    @pl.when(rb < n_blk)
                    def _(rb=rb):
                        y_copy(s, rb).start()

                @pl.when(s + 1 < n_super)
                def _():
                    x_copy(s + 1, 0, 0).start()

                @pl.when(s + 1 == n_super)
                def _():
                    for rb in range(n_blocks):
                        @pl.when(rb < n_blk)
                        def _(rb=rb):
                            y_copy(s, rb).wait()


def _expert(xs, sb_expert, sb_row0, sb_nblk, n_super, g_end, wg, wu, wd, bg, bu, bd):
    n_rows_total, d = xs.shape
    s_max = sb_expert.shape[0]
    d_exp = wg.shape[2]
    assert d_exp % EXP_TH == 0 and d % EXP_TN == 0 and ZERO_ROWS == 2 * EXP_TM
    n_act = d_exp // EXP_TH
    n_down = d // EXP_TN
    ne = wg.shape[0]

    def e_eff(s, sbe, ns):
        return sbe[jnp.minimum(s, ns[0] - 1)]

    def c_act(s, c, ns):
        return jnp.where(s < ns[0], jnp.minimum(c, n_act - 1), n_act - 1)

    def c_down(s, c, ns):
        return jnp.where(s < ns[0], jnp.maximum(c - n_act, 0), n_down - 1)

    act_w = pl.BlockSpec((1, d, EXP_TH),
                         lambda s, c, sbe, sbr, sbn, ns, ge: (e_eff(s, sbe, ns), 0, c_act(s, c, ns)))
    bias = pl.BlockSpec((ne, d_exp), lambda s, c, sbe, sbr, sbn, ns, ge: (0, 0))
    grid_spec = pltpu.PrefetchScalarGridSpec(
        num_scalar_prefetch=5,
        grid=(s_max, n_act + n_down),
        in_specs=[
            pl.BlockSpec(memory_space=pl.ANY),
            act_w, act_w,
            pl.BlockSpec((1, d_exp, EXP_TN),
                         lambda s, c, sbe, sbr, sbn, ns, ge: (e_eff(s, sbe, ns), 0, c_down(s, c, ns))),
            bias, bias,
            pl.BlockSpec((ne, d), lambda s, c, sbe, sbr, sbn, ns, ge: (0, 0)),
        ],
        out_specs=pl.BlockSpec(memory_space=pl.ANY),
        scratch_shapes=[pltpu.VMEM((ZERO_ROWS, d), F32), pltpu.VMEM((SB_ROWS, d), BF16),
                        pltpu.VMEM((SB_ROWS, d_exp), BF16), pltpu.VMEM((SB_ROWS, d), F32),
                        pltpu.VMEM((d, 2 * EXP_TH), BF16), pltpu.VMEM((d_exp, EXP_TN), BF16),
                        pltpu.SemaphoreType.DMA((2,)),
                        pltpu.SemaphoreType.DMA((SB_ROWS // EXP_TM,))],
    )
    return pl.pallas_call(
        functools.partial(_expert_kernel, n_act, n_down),
        grid_spec=grid_spec,
        out_shape=jax.ShapeDtypeStruct((n_rows_total, d), F32),
        compiler_params=_cparams(("arbitrary", "arbitrary")),
        name="expert",
    )(sb_expert, sb_row0, sb_nblk, n_super, g_end, xs, wg, wu, wd, bg, bu, bd)


def _combine_start(y_hbm, pos_ref, stage_slot, sem):
    tt = stage_slot.shape[1]

    def body(r, carry):
        for k in range(TOP_K):
            p = pos_ref[0, 0, r * TOP_K + k]
            pltpu.make_async_copy(y_hbm.at[pl.ds(p, 1), :], stage_slot.at[k, pl.ds(r, 1), :],
                                  sem).start()
        return carry
    lax.fori_loop(0, tt, body, 0, unroll=4)


def _combine_kernel(final_norm, pos_cur_ref, pos_nxt_ref, y_hbm, prob_ref, x1_ref, mod_ref, g_ref,
                    o_ref, stage, sem):
    j = pl.program_id(0)
    nj = pl.num_programs(0)
    tt = stage.shape[2]

    @pl.when(j == 0)
    def _():
        _combine_start(y_hbm, pos_cur_ref, stage.at[0], sem.at[0])

    @pl.when(j + 1 < nj)
    def _():
        nxt = (j + 1) % 2
        _combine_start(y_hbm, pos_nxt_ref, stage.at[nxt], sem.at[nxt])

    cur = j % 2
    for k in range(TOP_K):
        pltpu.make_async_copy(y_hbm.at[pl.ds(0, tt), :], stage.at[cur, k], sem.at[cur]).wait()
    prob = prob_ref[...]
    y = prob[:, 0:1] * stage[cur, 0]
    for k in range(1, TOP_K):
        y = y + prob[:, k:k + 1] * stage[cur, k]
    x2 = x1_ref[...] + mod_ref[0, 5:6, :] * y
    if final_norm:
        r = lax.rsqrt(jnp.mean(x2 * x2, axis=-1, keepdims=True) + RMS_EPS)
        x2 = (x2 * r) * g_ref[...]
    o_ref[...] = x2


def _combine(yb, pos, probs, x1, mod3, g, seq, final_norm):
    n, d = x1.shape
    tt = COMB_ROWS
    nt = n // tt
    tiles_per_seq = seq // tt
    pos3 = pos.reshape(nt, 1, tt * TOP_K)
    return pl.pallas_call(
        functools.partial(_combine_kernel, final_norm),
        grid=(nt,),
        in_specs=[pl.BlockSpec((1, 1, tt * TOP_K), lambda j: (j, 0, 0), memory_space=pltpu.SMEM),
                  pl.BlockSpec((1, 1, tt * TOP_K), lambda j: (jnp.minimum(j + 1, nt - 1), 0, 0),
                               memory_space=pltpu.SMEM),
                  pl.BlockSpec(memory_space=pl.ANY),
                  pl.BlockSpec((tt, LANES), lambda j: (j, 0)),
                  pl.BlockSpec((tt, d), lambda j: (j, 0)),
                  pl.BlockSpec((1, 6, d), lambda j: (j // tiles_per_seq, 0, 0)),
                  pl.BlockSpec((1, d), lambda j: (0, 0))],
        out_specs=pl.BlockSpec((tt, d), lambda j: (j, 0)),
        out_shape=jax.ShapeDtypeStruct((n, d), F32),
        scratch_shapes=[pltpu.VMEM((2, TOP_K, tt, d), F32), pltpu.SemaphoreType.DMA((2,))],
        compiler_params=_cparams(("arbitrary",)),
        name="combine",
    )(pos3, pos3, yb, probs, x1, mod3, g.reshape(1, d))


def _count_le(sorted_ends, q):
    return jnp.sum((sorted_ends[None, :] <= q[:, None]).astype(jnp.int32), axis=1)


def _plan(counts, n_tok):
    s_max = -(-n_tok * TOP_K // SB_ROWS) + N_EXPERTS
    cnt = counts[0, :N_EXPERTS]
    cnt_al = (cnt + ROW_ALIGN - 1) // ROW_ALIGN * ROW_ALIGN
    g_end = jnp.cumsum(cnt_al)
    g_start = g_end - cnt_al
    nsb_e = (cnt + SB_ROWS - 1) // SB_ROWS
    sb_end = jnp.cumsum(nsb_e)
    sb_start = sb_end - nsb_e
    n_super = sb_end[-1]
    s_ids = jnp.arange(s_max, dtype=jnp.int32)
    sb_expert = jnp.minimum(_count_le(sb_end, s_ids), N_EXPERTS - 1)
    j_in = s_ids - sb_start[sb_expert]
    live = s_ids < n_super
    sb_rows = jnp.where(live, jnp.clip(cnt[sb_expert] - j_in * SB_ROWS, 0, SB_ROWS), 0)
    sb_row0 = jnp.where(live, g_start[sb_expert] + j_in * SB_ROWS, 0)
    sb_nblk = (sb_rows + EXP_TM - 1) // EXP_TM
    i32 = lambda a: a.astype(jnp.int32)
    return (i32(g_start), i32(cnt), i32(g_end[-1:]), i32(sb_expert), i32(sb_row0), i32(sb_nblk),
            i32(n_super.reshape(1)))


def _dest_kernel(ir_ref, gs_ref, o_ref):
    ir = ir_ref[...].astype(F32)
    lane = lax.broadcasted_iota(jnp.int32, ir.shape, 1)
    lane_f = lane.astype(F32)
    out = jnp.zeros(ir.shape, F32)
    for k in range(TOP_K):
        e = jnp.sum(jnp.where(lane == k, ir, 0.0), axis=1, keepdims=True)
        r = jnp.sum(jnp.where(lane == TOP_K + k, ir, 0.0), axis=1, keepdims=True)
        g = jnp.sum(jnp.where(lane_f == e, gs_ref[...], 0.0), axis=1, keepdims=True)
        out = jnp.where(lane == k, g + r, out)
    o_ref[...] = out.astype(jnp.int32)


def _dest(idx_rank, g_start):
    n = idx_rank.shape[0]
    tt = 1024
    gs = jnp.pad(g_start.astype(F32), (0, LANES - N_EXPERTS)).reshape(1, LANES)
    return pl.pallas_call(
        _dest_kernel,
        grid=(n // tt,),
        in_specs=[pl.BlockSpec((tt, LANES), lambda i: (i, 0)),
                  pl.BlockSpec((1, LANES), lambda i: (0, 0))],
        out_specs=pl.BlockSpec((tt, LANES), lambda i: (i, 0)),
        out_shape=jax.ShapeDtypeStruct((n, LANES), jnp.int32),
        compiler_params=_cparams(("arbitrary",)),
        name="dest",
    )(idx_rank, gs)


def kernel(x, c, ada_w, ada_b, norm_mix_g, w_in, b_forget, conv_w, w_conv_out, w_attn_out, w_out,
           norm_ffn_g, router_w, router_b, exp_w_gate, exp_b_gate, exp_w_up, exp_b_up,
           exp_w_down, exp_b_down, final_norm_g):
    bsz, seq, d = x.shape
    n = bsz * seq
    d_conv = conv_w.shape[2]
    d_attn = w_attn_out.shape[1]
    n_main = 3 * d_conv + 3 * d_attn
    x2 = x.reshape(n, d)
    out = x2
    for l in range(ada_w.shape[0]):
        mod3 = _mod(c, ada_w[l], ada_b[l]).reshape(bsz, 6, d)
        proj, gates, f_logit = _inproj(out, mod3, norm_mix_g[l], w_in[l].T, n_main, N_HEADS,
                                       w_in.shape[2] - n_main - N_HEADS, seq,
                                       (3 * d_conv, 3 * d_conv + d_attn))
        fcol = _cumf(f_logit, b_forget[l], seq)
        att = _attn(proj, fcol, bsz, seq, 3 * d_conv, 3 * d_conv + d_attn, 3 * d_conv + 2 * d_attn)
        rw = jnp.pad(router_w[l], ((0, 0), (0, LANES - N_EXPERTS)))
        rb = jnp.pad(router_b[l], (0, LANES - N_EXPERTS)).reshape(1, LANES)
        x1, h2, logits = _mix(proj, gates, att, out, mod3, norm_ffn_g[l], conv_w[l],
                              _cast_bf16(w_conv_out[l]), _cast_bf16(w_attn_out[l]), _cast_bf16(w_out[l]),
                              rw, rb, seq, d_conv)
        idx_rank, probs, counts = _route(logits)
        g_start, cnt, g_end, sb_expert, sb_row0, sb_nblk, n_super = _plan(counts, n)
        dest = _dest(idx_rank, g_start)[:, :TOP_K].reshape(-1)
        n_rows_total = n * TOP_K + N_EXPERTS * ROW_ALIGN + EXP_TM
        xs = _dispatch(h2, dest, g_start, cnt, g_end, n_rows_total)
        ys = _expert(xs, sb_expert, sb_row0, sb_nblk, n_super, g_end, exp_w_gate[l], exp_w_up[l],
                     exp_w_down[l], exp_b_gate[l], exp_b_up[l], exp_b_down[l])
        out = _combine(ys, dest, probs, x1, mod3, final_norm_g, seq,
                       final_norm=(l == ada_w.shape[0] - 1))
    return out.reshape(bsz, seq, d)
```

```python
import functools

import jax
import jax.numpy as jnp
from jax import lax
from jax.experimental import pallas as pl
from jax.experimental.pallas import tpu as pltpu

F32 = jnp.float32
BF16 = jnp.bfloat16

N_HEADS = 16
HEAD_DIM = 64
N_EXPERTS = 32
TOP_K = 4
SWIGLU_LIMIT = 7.0
SWIGLU_ALPHA = 1.702
RMS_EPS = 1e-6
CONV_WIDTH = 3

LOG2E = 1.4426950408889634
QK_SCALE = HEAD_DIM ** -0.5 * LOG2E

LANES = 128
VMEM_LIMIT = 58 * 1024 * 1024

ROW_ALIGN = 8

EXP_TM = 272
SB_ROWS = 6 * EXP_TM
EXP_TH = 256
EXP_TN = 512
ZERO_ROWS = 2 * EXP_TM
DISP_TOKENS = 256
COMB_ROWS = 128


def _cparams(sem, vmem=VMEM_LIMIT):
    return pltpu.CompilerParams(dimension_semantics=sem, vmem_limit_bytes=vmem)


def _mod_kernel(cb_ref, w_ref, b_ref, o_ref):
    w = w_ref[...]
    tn = w.shape[1]
    rows = []
    for b in range(cb_ref.shape[0]):
        cv = cb_ref[b]
        ca = cv * jax.nn.sigmoid(cv)
        cols = [jnp.sum(w[:, j * LANES:(j + 1) * LANES] * ca, axis=0, keepdims=True)
                for j in range(tn // LANES)]
        rows.append(jnp.concatenate(cols, axis=1))
    o_ref[...] = jnp.concatenate(rows, axis=0) + b_ref[...]


def _mod(c, w, bias):
    bsz, d = c.shape
    n_out = w.shape[1]
    tn = 1024
    cb = jnp.broadcast_to(c[:, :, None], (bsz, d, LANES))
    return pl.pallas_call(
        _mod_kernel,
        grid=(n_out // tn,),
        in_specs=[pl.BlockSpec((bsz, d, LANES), lambda j: (0, 0, 0)),
                  pl.BlockSpec((d, tn), lambda j: (0, j)),
                  pl.BlockSpec((1, tn), lambda j: (0, j))],
        out_specs=pl.BlockSpec((bsz, tn), lambda j: (0, j)),
        out_shape=jax.ShapeDtypeStruct((bsz, n_out), F32),
        compiler_params=_cparams(("arbitrary",)),
        name="mod",
    )(cb, w, bias.reshape(1, n_out))


def _cast_kernel(x_ref, o_ref):
    o_ref[...] = x_ref[...].astype(o_ref.dtype)


def _cast_bf16(w):
    r, c = w.shape
    tr = 512
    return pl.pallas_call(
        _cast_kernel,
        grid=(r // tr,),
        in_specs=[pl.BlockSpec((tr, c), lambda i: (i, 0))],
        out_specs=pl.BlockSpec((tr, c), lambda i: (i, 0)),
        out_shape=jax.ShapeDtypeStruct((r, c), BF16),
        compiler_params=_cparams(("arbitrary",)),
        name="cast_bf16",
    )(w)


def _rmsnorm_mod(x, g, scale, shift):
    r = lax.rsqrt(jnp.mean(x * x, axis=-1, keepdims=True) + RMS_EPS)
    return (x * r) * g * (1.0 + scale) + shift


def _inproj_kernel(nj_main, jq, x_ref, mod_ref, g_ref, wm_ref, wg_ref, wf_ref,
                   om_ref, og_ref, of_ref, h_ref, wfp_ref):
    j = pl.program_id(1)

    @pl.when(j == 0)
    def _():
        wfp_ref[...] = jnp.zeros(wfp_ref.shape, BF16)
        wfp_ref[0:wf_ref.shape[0], :] = wf_ref[...].astype(BF16)
        rows = 256

        def body(rb, carry):
            r0 = pl.multiple_of(rb * rows, rows)
            h = _rmsnorm_mod(x_ref[pl.ds(r0, rows), :], g_ref[...], mod_ref[0, 1:2, :],
                             mod_ref[0, 0:1, :])
            hb = h.astype(BF16)
            h_ref[pl.ds(r0, rows), :] = hb
            of_ref[pl.ds(r0, rows), :] = _dot_nt(hb, wfp_ref[...])
            return carry

        lax.fori_loop(0, x_ref.shape[0] // rows, body, 0)

    @pl.when(j < nj_main)
    def _():
        scale = jnp.where(jnp.logical_and(j >= jq[0], j < jq[1]), QK_SCALE, 1.0)
        om_ref[...] = (_dot_nt(h_ref[...], wm_ref[...].astype(BF16)) * scale).astype(om_ref.dtype)

    @pl.when(j >= nj_main)
    def _():
        og_ref[...] = _dot_nt(h_ref[...], wg_ref[...].astype(BF16)).astype(og_ref.dtype)


def _inproj(x2, mod3, g, w_in_t, n_main, n_f, n_gates, seq, q_cols):
    n, d = x2.shape
    tm, tn = 1024, 512
    nj_main = n_main // tn
    assert q_cols[0] % tn == 0 and q_cols[1] % tn == 0
    jq = (q_cols[0] // tn, q_cols[1] // tn)
    nj_g = n_gates // tn
    gate_row0 = n_main + n_f
    tiles_per_seq = seq // tm
    return pl.pallas_call(
        functools.partial(_inproj_kernel, nj_main, jq),
        grid=(n // tm, nj_main + nj_g),
        in_specs=[pl.BlockSpec((tm, d), lambda i, j: (i, 0)),
                  pl.BlockSpec((1, 6, d), lambda i, j: (i // tiles_per_seq, 0, 0)),
                  pl.BlockSpec((1, d), lambda i, j: (0, 0)),
                  pl.BlockSpec((tn, d), lambda i, j: (jnp.minimum(j, nj_main - 1), 0)),
                  pl.BlockSpec((pl.Element(tn), pl.Element(d)),
                               lambda i, j: (pl.multiple_of(gate_row0 + tn * jnp.maximum(j - nj_main, 0), n_f), 0)),
                  pl.BlockSpec((n_f, d), lambda i, j: (n_main // n_f, 0))],
        out_specs=[pl.BlockSpec((tm, tn), lambda i, j: (i, jnp.minimum(j, nj_main - 1))),
                   pl.BlockSpec((tm, tn), lambda i, j: (i, jnp.maximum(j - nj_main, 0))),
                   pl.BlockSpec((tm, LANES), lambda i, j: (i, 0))],
        out_shape=[jax.ShapeDtypeStruct((n, n_main), BF16),
                   jax.ShapeDtypeStruct((n, n_gates), BF16),
                   jax.ShapeDtypeStruct((n, LANES), F32)],
        scratch_shapes=[pltpu.VMEM((tm, d), BF16), pltpu.VMEM((LANES, d), BF16)],
        compiler_params=_cparams(("arbitrary", "arbitrary")),
        name="inproj",
    )(x2, mod3, g.reshape(1, d), w_in_t, w_in_t, w_in_t)


def _cumf_kernel(blocks_per_seq, f_ref, bf_ref, fc_ref, carry_ref):
    i = pl.program_id(0)

    @pl.when(i % blocks_per_seq == 0)
    def _():
        carry_ref[...] = jnp.zeros_like(carry_ref)

    z = f_ref[...] + bf_ref[...]
    lf = -(jnp.maximum(-z, 0.0) + jnp.log1p(jnp.exp(-jnp.abs(z))))
    t = z.shape[0]
    row = lax.broadcasted_iota(jnp.int32, (t, t), 0)
    col = lax.broadcasted_iota(jnp.int32, (t, t), 1)
    tri = jnp.where(col <= row, 1.0, 0.0).astype(BF16)
    parts = [jnp.dot(tri, p, preferred_element_type=F32) for p in _split3(lf)]
    fb = (parts[0] + (parts[1] + parts[2])) + carry_ref[...]
    fc_ref[...] = fb
    carry_ref[...] = fb[t - 1:t, :]


def _cumf(f_logit, b_forget, seq):
    n = f_logit.shape[0]
    t = 512
    bf = jnp.pad(b_forget, (0, LANES - b_forget.shape[0])).reshape(1, LANES)
    return pl.pallas_call(
        functools.partial(_cumf_kernel, seq // t),
        grid=(n // t,),
        in_specs=[pl.BlockSpec((t, LANES), lambda i: (i, 0)),
                  pl.BlockSpec((1, LANES), lambda i: (0, 0))],
        out_specs=pl.BlockSpec((t, LANES), lambda i: (i, 0)),
        out_shape=jax.ShapeDtypeStruct((n, LANES), F32),
        scratch_shapes=[pltpu.VMEM((1, LANES), F32)],
        compiler_params=_cparams(("arbitrary",)),
        name="cumf",
    )(f_logit, bf)


def _dot_nt(a, b):
    return lax.dot_general(a, b, (((1,), (1,)), ((), ())), preferred_element_type=F32)


def _split3(x):
    hi = x.astype(BF16)
    r1 = x - hi.astype(F32)
    mid = r1.astype(BF16)
    lo = (r1 - mid.astype(F32)).astype(BF16)
    return hi, mid, lo


def _bias_lanes(f, head, lane0, sign):
    r = lax.broadcasted_iota(jnp.int32, (LANES, LANES), 0)
    c = lax.broadcasted_iota(jnp.int32, (LANES, LANES), 1)
    out = None
    for j, part in enumerate(_split3(f)):
        sel = jnp.where(jnp.logical_and(r == head, c == lane0 + j), sign, 0.0).astype(BF16)
        term = jnp.dot(part, sel, preferred_element_type=F32)
        out = term if out is None else out + term
    return out


def _attn_kernel(tq, tk, q_ref, k_ref, v_ref, fc_ref, o_ref, kp_ref, qp_ref, vt_ref, m_ref, l_ref,
                 acc_ref, sa_ref, sb_ref):
    hp = pl.program_id(1)
    qi = pl.program_id(2)
    seq = k_ref.shape[0]
    lane = lax.broadcasted_iota(jnp.int32, (1, LANES), 1)
    own = (lane < HEAD_DIM, lane >= HEAD_DIM)
    spare = (HEAD_DIM, 0)
    ones_k = tuple(jnp.where(jnp.logical_and(lane >= spare[h] + 3, lane < spare[h] + 6), 1.0, 0.0)
                   for h in range(2))
    ones_q = tuple(jnp.where(jnp.logical_and(lane >= spare[h], lane < spare[h] + 3), 1.0, 0.0)
                   for h in range(2))

    @pl.when(qi == 0)
    def _():
        def prep(ci, carry):
            r0 = pl.multiple_of(ci * tk, tk)
            f = fc_ref[pl.ds(r0, tk), :] * LOG2E
            k = k_ref[pl.ds(r0, tk), :]
            q = q_ref[pl.ds(r0, tk), :]
            for h in range(2):
                aux = _bias_lanes(f, 2 * hp + h, spare[h], -1.0) + ones_k[h]
                kp_ref[h, pl.ds(r0, tk), :] = jnp.where(own[h], k, aux.astype(BF16))
                aux = _bias_lanes(f, 2 * hp + h, spare[h] + 3, 1.0) + ones_q[h]
                qp_ref[h, pl.ds(r0, tk), :] = jnp.where(own[h], q, aux.astype(BF16))
            vt_ref[:, pl.ds(r0, tk)] = v_ref[pl.ds(r0, tk), :].astype(F32).T.astype(BF16)
            return carry
        lax.fori_loop(0, seq // tk, prep, 0)

    q0 = pl.multiple_of(qi * tq, tq)
    qp = [qp_ref[h, pl.ds(q0, tq), :] for h in range(2)]

    m_ref[...] = jnp.full(m_ref.shape, -jnp.inf, F32)
    l_ref[...] = jnp.zeros(l_ref.shape, F32)
    acc_ref[...] = jnp.zeros(acc_ref.shape, F32)

    def scores(kt, s_ref):
        k0 = pl.multiple_of(kt * tk, tk)
        for h in range(2):
            s_ref[h] = _dot_nt(kp_ref[h, pl.ds(k0, tk), :], qp[h])

    def update(kt, s_ref, masked):
        k0 = pl.multiple_of(kt * tk, tk)
        for h in range(2):
            s = s_ref[h]
            if masked:
                kr = lax.broadcasted_iota(jnp.int32, (tk, tq), 0)
                qc = lax.broadcasted_iota(jnp.int32, (tk, tq), 1)
                s = jnp.where(kr <= qc, s, -jnp.inf)
            m_old = m_ref[h]
            m_new = jnp.maximum(m_old, jnp.max(s, axis=0, keepdims=True))
            alpha = jnp.exp2(m_old - m_new)
            p = jnp.exp2(s - m_new)
            l_ref[h] = l_ref[h] * alpha + jnp.sum(p, axis=0, keepdims=True)
            m_ref[h] = m_new
            rows = pl.ds(h * HEAD_DIM, HEAD_DIM)
            pv = jnp.dot(vt_ref[rows, pl.ds(k0, tk)], p.astype(BF16), preferred_element_type=F32)
            acc_ref[rows, :] = acc_ref[rows, :] * alpha + pv

    scores(0, sa_ref)

    def pair(j, carry):
        scores(2 * j + 1, sb_ref)
        update(2 * j, sa_ref, False)
        scores(2 * j + 2, sa_ref)
        update(2 * j + 1, sb_ref, False)
        return carry

    lax.fori_loop(0, qi // 2, pair, 0)

    @pl.when(qi % 2 == 0)
    def _():
        update(qi, sa_ref, True)

    @pl.when(qi % 2 == 1)
    def _():
        scores(qi, sb_ref)
        update(qi - 1, sa_ref, False)
        update(qi, sb_ref, True)

    out_t = jnp.concatenate([acc_ref[pl.ds(h * HEAD_DIM, HEAD_DIM), :] / l_ref[h] for h in range(2)],
                            axis=0)
    o_ref[...] = out_t.T.astype(o_ref.dtype)


def _attn(proj, fcol, bsz, seq, q_col0, k_col0, v_col0):
    tq = tk = 512
    nq = seq // tq
    n_pairs = N_HEADS * HEAD_DIM // LANES
    qb, kb, vb = q_col0 // LANES, k_col0 // LANES, v_col0 // LANES
    return pl.pallas_call(
        functools.partial(_attn_kernel, tq, tk),
        grid=(bsz, n_pairs, nq),
        in_specs=[pl.BlockSpec((seq, LANES), lambda b, hp, qi: (b, qb + hp)),
                  pl.BlockSpec((seq, LANES), lambda b, hp, qi: (b, kb + hp)),
                  pl.BlockSpec((seq, LANES), lambda b, hp, qi: (b, vb + hp)),
                  pl.BlockSpec((seq, LANES), lambda b, hp, qi: (b, 0))],
        out_specs=pl.BlockSpec((tq, LANES), lambda b, hp, qi: (b * nq + qi, hp)),
        out_shape=jax.ShapeDtypeStruct((bsz * seq, N_HEADS * HEAD_DIM), BF16),
        scratch_shapes=[pltpu.VMEM((2, seq, LANES), BF16), pltpu.VMEM((2, seq, LANES), BF16),
                        pltpu.VMEM((LANES, seq), BF16),
                        pltpu.VMEM((2, 1, tq), F32), pltpu.VMEM((2, 1, tq), F32),
                        pltpu.VMEM((LANES, tq), F32),
                        pltpu.VMEM((2, tk, tq), F32), pltpu.VMEM((2, tk, tq), F32)],
        compiler_params=_cparams(("arbitrary", "arbitrary", "arbitrary")),
        name="attn",
    )(proj, proj, proj, fcol)


def _mix_kernel(tiles_per_seq, xc_ref, cb_ref, cc_ref, hxc_ref, hcc_ref, att_ref, gc_ref, ga_ref,
                x_ref, mod_ref, g2_ref, cw_ref, wc_ref, wa_ref, wo_ref, rw_ref, rb_ref,
                x1_ref, h2_ref, lg_ref):
    i = pl.program_id(0)
    u = cc_ref[...].astype(F32) * xc_ref[...].astype(F32)
    hu = hcc_ref[...].astype(F32) * hxc_ref[...].astype(F32)
    hu = jnp.where(i % tiles_per_seq == 0, 0.0, hu)
    nh = hu.shape[0]
    row = lax.broadcasted_iota(jnp.int32, u.shape, 0)
    u1 = jnp.where(row == 0, hu[nh - 1:nh, :], pltpu.roll(u, 1, axis=0))
    u2 = jnp.where(row == 0, hu[nh - 2:nh - 1, :],
                   jnp.where(row == 1, hu[nh - 1:nh, :], pltpu.roll(u, 2, axis=0)))
    conv = cw_ref[0:1, :] * u2 + cw_ref[1:2, :] * u1 + cw_ref[2:3, :] * u
    z = (cb_ref[...].astype(F32) * conv).astype(BF16)
    y_conv = jnp.dot(z, wc_ref[...], preferred_element_type=F32)
    y_attn = jnp.dot(att_ref[...], wa_ref[...], preferred_element_type=F32)
    merged = (jax.nn.sigmoid(gc_ref[...].astype(F32)) * y_conv
              + jax.nn.sigmoid(ga_ref[...].astype(F32)) * y_attn)
    o = jnp.dot(merged.astype(BF16), wo_ref[...], preferred_element_type=F32)
    x1 = x_ref[...] + mod_ref[0, 2:3, :] * o
    x1_ref[...] = x1
    h2 = _rmsnorm_mod(x1, g2_ref[...], mod_ref[0, 4:5, :], mod_ref[0, 3:4, :])
    h2_ref[...] = h2
    h_hi, h_lo, _ = _split3(h2)
    w_hi, w_lo, _ = _split3(rw_ref[...])
    lg_ref[...] = (jnp.dot(h_hi, w_hi, preferred_element_type=F32)
                   + (jnp.dot(h_hi, w_lo, preferred_element_type=F32)
                      + jnp.dot(h_lo, w_hi, preferred_element_type=F32))) + rb_ref[...]


def _mix(proj, gates, att, x2, mod3, g2, conv_w, wc, wa, wo, rw, rb, seq, d_conv):
    n, d = x2.shape
    tm = 256
    halo = 16
    tiles_per_seq = seq // tm
    d_attn = att.shape[1]

    def resident(shape):
        return pl.BlockSpec(shape, lambda i: (0,) * len(shape), pipeline_mode=pl.Buffered(1))

    return pl.pallas_call(
        functools.partial(_mix_kernel, tiles_per_seq),
        grid=(n // tm,),
        in_specs=[pl.BlockSpec((tm, d_conv), lambda i: (i, 0)),
                  pl.BlockSpec((tm, d_conv), lambda i: (i, 1)),
                  pl.BlockSpec((tm, d_conv), lambda i: (i, 2)),
                  pl.BlockSpec((halo, d_conv), lambda i: (jnp.maximum(i * (tm // halo) - 1, 0), 0)),
                  pl.BlockSpec((halo, d_conv), lambda i: (jnp.maximum(i * (tm // halo) - 1, 0), 2)),
                  pl.BlockSpec((tm, d_attn), lambda i: (i, 0)),
                  pl.BlockSpec((tm, d), lambda i: (i, 0)),
                  pl.BlockSpec((tm, d), lambda i: (i, 1)),
                  pl.BlockSpec((tm, d), lambda i: (i, 0)),
                  pl.BlockSpec((1, 6, d), lambda i: (i // tiles_per_seq, 0, 0)),
                  resident((1, d)),
                  resident((CONV_WIDTH, d_conv)),
                  resident((d_conv, d)),
                  resident((d_attn, d)),
                  resident((d, d)),
                  resident((d, LANES)),
                  resident((1, LANES))],
        out_specs=[pl.BlockSpec((tm, d), lambda i: (i, 0)),
                   pl.BlockSpec((tm, d), lambda i: (i, 0)),
                   pl.BlockSpec((tm, LANES), lambda i: (i, 0))],
        out_shape=[jax.ShapeDtypeStruct((n, d), F32),
                   jax.ShapeDtypeStruct((n, d), F32),
                   jax.ShapeDtypeStruct((n, LANES), F32)],
        compiler_params=_cparams(("arbitrary",)),
        name="mix",
    )(proj, proj, proj, proj, proj, att, gates, gates, x2, mod3, g2.reshape(1, d), conv_w,
      wc, wa, wo, rw, rb)


def _route_kernel(lg_ref, idx_ref, prob_ref, cnt_ref, carry_ref):
    i = pl.program_id(0)

    @pl.when(i == 0)
    def _():
        carry_ref[...] = jnp.zeros_like(carry_ref)

    tt = lg_ref.shape[0]
    lane = lax.broadcasted_iota(jnp.int32, (tt, LANES), 1)
    lane_f = lane.astype(F32)
    l = jnp.where(lane < N_EXPERTS, lg_ref[...], -jnp.inf)
    onehot = jnp.zeros((tt, LANES), F32)
    vals, ids = [], []
    for _ in range(TOP_K):
        m = jnp.max(l, axis=1, keepdims=True)
        idx = jnp.min(jnp.where(l == m, lane_f, float(LANES)), axis=1, keepdims=True)
        sel = lane_f == idx
        vals.append(m)
        ids.append(idx)
        onehot = jnp.where(sel, 1.0, onehot)
        l = jnp.where(sel, -jnp.inf, l)
    es = [jnp.exp(v - vals[0]) for v in vals]
    denom = es[0] + es[1] + es[2] + es[3]
    r = lax.broadcasted_iota(jnp.int32, (tt, tt), 0)
    c = lax.broadcasted_iota(jnp.int32, (tt, tt), 1)
    before = jnp.where(c < r, 1.0, 0.0).astype(BF16)
    cnt_before = jnp.dot(before, onehot.astype(BF16), preferred_element_type=F32) + carry_ref[...]
    idx_out = jnp.zeros((tt, LANES), jnp.int32)
    prob_out = jnp.zeros((tt, LANES), F32)
    for k in range(TOP_K):
        rank = jnp.sum(jnp.where(lane_f == ids[k], cnt_before, 0.0), axis=1, keepdims=True)
        idx_out = jnp.where(lane == k, ids[k].astype(jnp.int32), idx_out)
        idx_out = jnp.where(lane == TOP_K + k, rank.astype(jnp.int32), idx_out)
        prob_out = jnp.where(lane == k, es[k] / denom, prob_out)
    idx_ref[...] = idx_out
    prob_ref[...] = prob_out
    carry_ref[...] = carry_ref[...] + jnp.sum(onehot, axis=0, keepdims=True)
    cnt_ref[...] = carry_ref[...].astype(jnp.int32)


def _route(logits):
    n = logits.shape[0]
    tt = 512
    return pl.pallas_call(
        _route_kernel,
        grid=(n // tt,),
        in_specs=[pl.BlockSpec((tt, LANES), lambda i: (i, 0))],
        out_specs=[pl.BlockSpec((tt, LANES), lambda i: (i, 0)),
                   pl.BlockSpec((tt, LANES), lambda i: (i, 0)),
                   pl.BlockSpec((1, LANES), lambda i: (0, 0))],
        out_shape=[jax.ShapeDtypeStruct((n, LANES), jnp.int32),
                   jax.ShapeDtypeStruct((n, LANES), F32),
                   jax.ShapeDtypeStruct((1, LANES), jnp.int32)],
        scratch_shapes=[pltpu.VMEM((1, LANES), F32)],
        compiler_params=_cparams(("arbitrary",)),
        name="route",
    )(logits)


def _zero_fill_rows(zero_rows, dst_hbm, row0, n_rows, sem, start):
    off = row0
    size = 1 << (zero_rows.shape[0].bit_length() - 1)
    while size >= ROW_ALIGN:
        @pl.when((n_rows & size) != 0)
        def _(off=off, size=size):
            cp = pltpu.make_async_copy(zero_rows.at[pl.ds(0, size), :],
                                       dst_hbm.at[pl.ds(pl.multiple_of(off, ROW_ALIGN), size), :], sem)
            if start:
                cp.start()
            else:
                cp.wait()
        off = off + (n_rows & size)
        size //= 2


def _dispatch_kernel(gs_ref, cnt_ref, gend_ref, dest_ref, h_ref, xs_hbm, zbuf, sem):
    j = pl.program_id(0)
    tt = h_ref.shape[0]
    n_total = xs_hbm.shape[0]

    @pl.when(j == 0)
    def _():
        zbuf[...] = jnp.zeros(zbuf.shape, F32)
        tail0 = gend_ref[0]

        def pad_rows(start):
            def group(e, carry):
                cnt = cnt_ref[e]

                def row(r, c2):
                    cp = pltpu.make_async_copy(zbuf.at[pl.ds(0, 1), :],
                                               xs_hbm.at[pl.ds(gs_ref[e] + r, 1), :], sem.at[2])
                    if start:
                        cp.start()
                    else:
                        cp.wait()
                    return c2
                lax.fori_loop(cnt, (cnt + ROW_ALIGN - 1) & (-ROW_ALIGN), row, 0)
                return carry
            lax.fori_loop(0, N_EXPERTS, group, 0)

        for start in (True, False):
            _zero_fill_rows(zbuf, xs_hbm, tail0, n_total - tail0, sem.at[2], start)
            pad_rows(start)

    def tok(r, carry):
        for k in range(TOP_K):
            d = dest_ref[0, 0, r * TOP_K + k]
            pltpu.make_async_copy(h_ref.at[pl.ds(r, 1), :], xs_hbm.at[pl.ds(d, 1), :],
                                  sem.at[0]).start()
        return carry
    lax.fori_loop(0, tt, tok, 0, unroll=4)

    for k in range(TOP_K):
        pltpu.make_async_copy(h_ref, xs_hbm.at[pl.ds(0, tt), :], sem.at[0]).wait()


def _dispatch(h2, dest, g_start, cnt, g_end, n_rows_total):
    n, d = h2.shape
    tt = DISP_TOKENS
    dest3 = dest.reshape(n // tt, 1, tt * TOP_K)
    grid_spec = pltpu.PrefetchScalarGridSpec(
        num_scalar_prefetch=3,
        grid=(n // tt,),
        in_specs=[pl.BlockSpec((1, 1, tt * TOP_K), lambda j, gs, ct, ge: (j, 0, 0),
                               memory_space=pltpu.SMEM),
                  pl.BlockSpec((tt, d), lambda j, gs, ct, ge: (j, 0))],
        out_specs=pl.BlockSpec(memory_space=pl.ANY),
        scratch_shapes=[pltpu.VMEM((ZERO_ROWS, d), F32), pltpu.SemaphoreType.DMA((3,))],
    )
    return pl.pallas_call(
        _dispatch_kernel,
        grid_spec=grid_spec,
        out_shape=jax.ShapeDtypeStruct((n_rows_total, d), F32),
        compiler_params=_cparams(("arbitrary",)),
        name="dispatch",
    )(g_start, cnt, g_end, dest3, h2)


def _for_blocks(n_blk, n_max, pre, body):
    for n in range(4, n_max + 1):
        @pl.when(n_blk == n)
        def _(n=n):
            pre()
            for rb in range(n):
                body(rb)

    @pl.when(n_blk < 4)
    def _():
        pre()

        @pl.when(n_blk >= 2)
        def _():
            body(0)
            body(1)

        @pl.when(n_blk % 2 == 1)
        def _():
            body(n_blk - 1)


def _expert_kernel(n_act, n_down, sbe_ref, sbr_ref, sbn_ref, ns_ref, gend_ref,
                   xs_hbm, wg_ref, wu_ref, wd_ref, bg_ref, bu_ref, bd_ref, ys_hbm,
                   stage, xbf, act, out, wgu_ref, wdb_ref, ld_sem, st_sem):
    s = pl.program_id(0)
    c = pl.program_id(1)
    th = EXP_TH
    n_super = ns_ref[0]
    n_blocks = SB_ROWS // EXP_TM

    def slot_ref(slot):
        return stage.at[pl.ds(slot * EXP_TM, EXP_TM), :]

    def x_copy(sb, rb, slot):
        r0 = pl.multiple_of(sbr_ref[sb] + rb * EXP_TM, ROW_ALIGN)
        return pltpu.make_async_copy(xs_hbm.at[pl.ds(r0, EXP_TM), :], slot_ref(slot), ld_sem.at[slot])

    def y_copy(sb, rb):
        r0 = pl.multiple_of(sbr_ref[sb] + rb * EXP_TM, ROW_ALIGN)
        return pltpu.make_async_copy(out.at[pl.ds(rb * EXP_TM, EXP_TM), :],
                                     ys_hbm.at[pl.ds(r0, EXP_TM), :], st_sem.at[rb])

    @pl.when(jnp.logical_and(s == 0, c == 0))
    def _():
        stage[...] = jnp.zeros(stage.shape, F32)
        tail0 = gend_ref[0]
        for start in (True, False):
            _zero_fill_rows(stage, ys_hbm, tail0, ys_hbm.shape[0] - tail0, st_sem.at[0], start)
        x_copy(0, 0, 0).start()

    @pl.when(s < n_super)
    def _():
        n_blk = sbn_ref[s]
        n_blk_prev = sbn_ref[jnp.maximum(s - 1, 0)]
        expert = sbe_ref[s]

        @pl.when(c == 0)
        def _():
            for rb in range(n_blocks):
                @pl.when(rb < n_blk)
                def _(rb=rb):
                    if rb + 1 < n_blocks:
                        @pl.when(rb + 1 < n_blk)
                        def _():
                            x_copy(s, rb + 1, (rb + 1) % 2).start()
                    x_copy(s, rb, rb % 2).wait()
                    xbf[pl.ds(rb * EXP_TM, EXP_TM), :] = slot_ref(rb % 2)[...].astype(BF16)

        @pl.when(c < n_act)
        def _():
            cols = pl.ds(pl.multiple_of(c * th, th), th)
            bg = bg_ref[pl.ds(expert, 1), cols]
            bu = bu_ref[pl.ds(expert, 1), cols]

            def cast_weights():
                wgu_ref[:, :th] = wg_ref[0].astype(BF16)
                wgu_ref[:, th:] = wu_ref[0].astype(BF16)

            def act_block(rb):
                rows = pl.ds(pl.multiple_of(rb * EXP_TM, EXP_TM), EXP_TM)
                gu = jnp.dot(xbf[rows, :], wgu_ref[...], preferred_element_type=F32)
                g = jnp.minimum(gu[:, :th] + bg, SWIGLU_LIMIT)
                u = jnp.clip(gu[:, th:] + bu, -SWIGLU_LIMIT, SWIGLU_LIMIT)
                a = (u + 1.0) * g * jax.nn.sigmoid(SWIGLU_ALPHA * g)
                act[rows, cols] = a.astype(BF16)

            _for_blocks(n_blk, n_blocks, cast_weights, act_block)

        @pl.when(c >= n_act)
        def _():
            cols = pl.ds(pl.multiple_of((c - n_act) * EXP_TN, EXP_TN), EXP_TN)
            bd = bd_ref[pl.ds(expert, 1), cols]
            for rb in range(n_blocks):
                @pl.when(jnp.logical_and(c == n_act, jnp.logical_and(s > 0, rb < n_blk_prev)))
                def _(rb=rb):
                    y_copy(s - 1, rb).wait()

            def cast_weights():
                wdb_ref[...] = wd_ref[0].astype(BF16)

            def down_block(rb):
                rows = pl.ds(pl.multiple_of(rb * EXP_TM, EXP_TM), EXP_TM)
                out[rows, cols] = jnp.dot(act[rows, :], wdb_ref[...],
                                          preferred_element_type=F32) + bd

            _for_blocks(n_blk, n_blocks, cast_weights, down_block)

            @pl.when(c == n_act + n_down - 1)
            def _():
                for rb in range(n_blocks):
                    @pl.when(rb < n_blk)
                    def _(rb=rb):
                        y_copy(s, rb).start()

                @pl.when(s + 1 < n_super)
                def _():
                    x_copy(s + 1, 0, 0).start()

                @pl.when(s + 1 == n_super)
                def _():
                    for rb in range(n_blocks):
                        @pl.when(rb < n_blk)
                        def _(rb=rb):
                            y_copy(s, rb).wait()


def _expert(xs, sb_expert, sb_row0, sb_nblk, n_super, g_end, wg, wu, wd, bg, bu, bd):
    n_rows_total, d = xs.shape
    s_max = sb_expert.shape[0]
    d_exp = wg.shape[2]
    assert d_exp % EXP_TH == 0 and d % EXP_TN == 0 and ZERO_ROWS == 2 * EXP_TM
    n_act = d_exp // EXP_TH
    n_down = d // EXP_TN
    ne = wg.shape[0]

    def e_eff(s, sbe, ns):
        return sbe[jnp.minimum(s, ns[0] - 1)]

    def c_act(s, c, ns):
        return jnp.where(s < ns[0], jnp.minimum(c, n_act - 1), n_act - 1)

    def c_down(s, c, ns):
        return jnp.where(s < ns[0], jnp.maximum(c - n_act, 0), n_down - 1)

    act_w = pl.BlockSpec((1, d, EXP_TH),
                         lambda s, c, sbe, sbr, sbn, ns, ge: (e_eff(s, sbe, ns), 0, c_act(s, c, ns)))
    bias = pl.BlockSpec((ne, d_exp), lambda s, c, sbe, sbr, sbn, ns, ge: (0, 0))
    grid_spec = pltpu.PrefetchScalarGridSpec(
        num_scalar_prefetch=5,
        grid=(s_max, n_act + n_down),
        in_specs=[
            pl.BlockSpec(memory_space=pl.ANY),
            act_w, act_w,
            pl.BlockSpec((1, d_exp, EXP_TN),
                         lambda s, c, sbe, sbr, sbn, ns, ge: (e_eff(s, sbe, ns), 0, c_down(s, c, ns))),
            bias, bias,
            pl.BlockSpec((ne, d), lambda s, c, sbe, sbr, sbn, ns, ge: (0, 0)),
        ],
        out_specs=pl.BlockSpec(memory_space=pl.ANY),
        scratch_shapes=[pltpu.VMEM((ZERO_ROWS, d), F32), pltpu.VMEM((SB_ROWS, d), BF16),
                        pltpu.VMEM((SB_ROWS, d_exp), BF16), pltpu.VMEM((SB_ROWS, d), F32),
                        pltpu.VMEM((d, 2 * EXP_TH), BF16), pltpu.VMEM((d_exp, EXP_TN), BF16),
                        pltpu.SemaphoreType.DMA((2,)),
                        pltpu.SemaphoreType.DMA((SB_ROWS // EXP_TM,))],
    )
    return pl.pallas_call(
        functools.partial(_expert_kernel, n_act, n_down),
        grid_spec=grid_spec,
        out_shape=jax.ShapeDtypeStruct((n_rows_total, d), F32),
        compiler_params=_cparams(("arbitrary", "arbitrary")),
        name="expert",
    )(sb_expert, sb_row0, sb_nblk, n_super, g_end, xs, wg, wu, wd, bg, bu, bd)


def _combine_start(y_hbm, pos_ref, stage_slot, sem):
    tt = stage_slot.shape[1]

    def body(r, carry):
        for k in range(TOP_K):
            p = pos_ref[0, 0, r * TOP_K + k]
            pltpu.make_async_copy(y_hbm.at[pl.ds(p, 1), :], stage_slot.at[k, pl.ds(r, 1), :],
                                  sem).start()
        return carry
    lax.fori_loop(0, tt, body, 0, unroll=4)


def _combine_kernel(final_norm, pos_cur_ref, pos_nxt_ref, y_hbm, prob_ref, x1_ref, mod_ref, g_ref,
                    o_ref, stage, sem):
    j = pl.program_id(0)
    nj = pl.num_programs(0)
    tt = stage.shape[2]

    @pl.when(j == 0)
    def _():
        _combine_start(y_hbm, pos_cur_ref, stage.at[0], sem.at[0])

    @pl.when(j + 1 < nj)
    def _():
        nxt = (j + 1) % 2
        _combine_start(y_hbm, pos_nxt_ref, stage.at[nxt], sem.at[nxt])

    cur = j % 2
    for k in range(TOP_K):
        pltpu.make_async_copy(y_hbm.at[pl.ds(0, tt), :], stage.at[cur, k], sem.at[cur]).wait()
    prob = prob_ref[...]
    y = prob[:, 0:1] * stage[cur, 0]
    for k in range(1, TOP_K):
        y = y + prob[:, k:k + 1] * stage[cur, k]
    x2 = x1_ref[...] + mod_ref[0, 5:6, :] * y
    if final_norm:
        r = lax.rsqrt(jnp.mean(x2 * x2, axis=-1, keepdims=True) + RMS_EPS)
        x2 = (x2 * r) * g_ref[...]
    o_ref[...] = x2


def _combine(yb, pos, probs, x1, mod3, g, seq, final_norm):
    n, d = x1.shape
    tt = COMB_ROWS
    nt = n // tt
    tiles_per_seq = seq // tt
    pos3 = pos.reshape(nt, 1, tt * TOP_K)
    return pl.pallas_call(
        functools.partial(_combine_kernel, final_norm),
        grid=(nt,),
        in_specs=[pl.BlockSpec((1, 1, tt * TOP_K), lambda j: (j, 0, 0), memory_space=pltpu.SMEM),
                  pl.BlockSpec((1, 1, tt * TOP_K), lambda j: (jnp.minimum(j + 1, nt - 1), 0, 0),
                               memory_space=pltpu.SMEM),
                  pl.BlockSpec(memory_space=pl.ANY),
                  pl.BlockSpec((tt, LANES), lambda j: (j, 0)),
                  pl.BlockSpec((tt, d), lambda j: (j, 0)),
                  pl.BlockSpec((1, 6, d), lambda j: (j // tiles_per_seq, 0, 0)),
                  pl.BlockSpec((1, d), lambda j: (0, 0))],
        out_specs=pl.BlockSpec((tt, d), lambda j: (j, 0)),
        out_shape=jax.ShapeDtypeStruct((n, d), F32),
        scratch_shapes=[pltpu.VMEM((2, TOP_K, tt, d), F32), pltpu.SemaphoreType.DMA((2,))],
        compiler_params=_cparams(("arbitrary",)),
        name="combine",
    )(pos3, pos3, yb, probs, x1, mod3, g.reshape(1, d))


def _count_le(sorted_ends, q):
    return jnp.sum((sorted_ends[None, :] <= q[:, None]).astype(jnp.int32), axis=1)


def _plan(counts, n_tok):
    s_max = -(-n_tok * TOP_K // SB_ROWS) + N_EXPERTS
    cnt = counts[0, :N_EXPERTS]
    cnt_al = (cnt + ROW_ALIGN - 1) // ROW_ALIGN * ROW_ALIGN
    g_end = jnp.cumsum(cnt_al)
    g_start = g_end - cnt_al
    nsb_e = (cnt + SB_ROWS - 1) // SB_ROWS
    sb_end = jnp.cumsum(nsb_e)
    sb_start = sb_end - nsb_e
    n_super = sb_end[-1]
    s_ids = jnp.arange(s_max, dtype=jnp.int32)
    sb_expert = jnp.minimum(_count_le(sb_end, s_ids), N_EXPERTS - 1)
    j_in = s_ids - sb_start[sb_expert]
    live = s_ids < n_super
    sb_rows = jnp.where(live, jnp.clip(cnt[sb_expert] - j_in * SB_ROWS, 0, SB_ROWS), 0)
    sb_row0 = jnp.where(live, g_start[sb_expert] + j_in * SB_ROWS, 0)
    sb_nblk = (sb_rows + EXP_TM - 1) // EXP_TM
    i32 = lambda a: a.astype(jnp.int32)
    return (i32(g_start), i32(cnt), i32(g_end[-1:]), i32(sb_expert), i32(sb_row0), i32(sb_nblk),
            i32(n_super.reshape(1)))


def _dest_kernel(ir_ref, gs_ref, o_ref):
    ir = ir_ref[...].astype(F32)
    lane = lax.broadcasted_iota(jnp.int32, ir.shape, 1)
    lane_f = lane.astype(F32)
    out = jnp.zeros(ir.shape, F32)
    for k in range(TOP_K):
        e = jnp.sum(jnp.where(lane == k, ir, 0.0), axis=1, keepdims=True)
        r = jnp.sum(jnp.where(lane == TOP_K + k, ir, 0.0), axis=1, keepdims=True)
        g = jnp.sum(jnp.where(lane_f == e, gs_ref[...], 0.0), axis=1, keepdims=True)
        out = jnp.where(lane == k, g + r, out)
    o_ref[...] = out.astype(jnp.int32)


def _dest(idx_rank, g_start):
    n = idx_rank.shape[0]
    tt = 1024
    gs = jnp.pad(g_start.astype(F32), (0, LANES - N_EXPERTS)).reshape(1, LANES)
    return pl.pallas_call(
        _dest_kernel,
        grid=(n // tt,),
        in_specs=[pl.BlockSpec((tt, LANES), lambda i: (i, 0)),
                  pl.BlockSpec((1, LANES), lambda i: (0, 0))],
        out_specs=pl.BlockSpec((tt, LANES), lambda i: (i, 0)),
        out_shape=jax.ShapeDtypeStruct((n, LANES), jnp.int32),
        compiler_params=_cparams(("arbitrary",)),
        name="dest",
    )(idx_rank, gs)


def kernel(x, c, ada_w, ada_b, norm_mix_g, w_in, b_forget, conv_w, w_conv_out, w_attn_out, w_out,
           norm_ffn_g, router_w, router_b, exp_w_gate, exp_b_gate, exp_w_up, exp_b_up,
           exp_w_down, exp_b_down, final_norm_g):
    bsz, seq, d = x.shape
    n = bsz * seq
    d_conv = conv_w.shape[2]
    d_attn = w_attn_out.shape[1]
    n_main = 3 * d_conv + 3 * d_attn
    x2 = x.reshape(n, d)
    out = x2
    for l in range(ada_w.shape[0]):
        mod3 = _mod(c, ada_w[l], ada_b[l]).reshape(bsz, 6, d)
        proj, gates, f_logit = _inproj(out, mod3, norm_mix_g[l], w_in[l].T, n_main, N_HEADS,
                                       w_in.shape[2] - n_main - N_HEADS, seq,
                                       (3 * d_conv, 3 * d_conv + d_attn))
        fcol = _cumf(f_logit, b_forget[l], seq)
        att = _attn(proj, fcol, bsz, seq, 3 * d_conv, 3 * d_conv + d_attn, 3 * d_conv + 2 * d_attn)
        rw = jnp.pad(router_w[l], ((0, 0), (0, LANES - N_EXPERTS)))
        rb = jnp.pad(router_b[l], (0, LANES - N_EXPERTS)).reshape(1, LANES)
        x1, h2, logits = _mix(proj, gates, att, out, mod3, norm_ffn_g[l], conv_w[l],
                              _cast_bf16(w_conv_out[l]), _cast_bf16(w_attn_out[l]), _cast_bf16(w_out[l]),
                              rw, rb, seq, d_conv)
        idx_rank, probs, counts = _route(logits)
        g_start, cnt, g_end, sb_expert, sb_row0, sb_nblk, n_super = _plan(counts, n)
        dest = _dest(idx_rank, g_start)[:, :TOP_K].reshape(-1)
        n_rows_total = n * TOP_K + N_EXPERTS * ROW_ALIGN + EXP_TM
        xs = _dispatch(h2, dest, g_start, cnt, g_end, n_rows_total)
        ys = _expert(xs, sb_expert, sb_row0, sb_nblk, n_super, g_end, exp_w_gate[l], exp_w_up[l],
                     exp_w_down[l], exp_b_gate[l], exp_b_up[l], exp_b_down[l])
        out = _combine(ys, dest, probs, x1, mod3, final_norm_g, seq,
                       final_norm=(l == ada_w.shape[0] - 1))
    return out.reshape(bsz, seq, d)
```

```python
import functools

import jax
import jax.numpy as jnp
from jax import lax
from jax.experimental import pallas as pl
from jax.experimental.pallas import tpu as pltpu

F32 = jnp.float32
BF16 = jnp.bfloat16

N_HEADS = 16
HEAD_DIM = 64
N_EXPERTS = 32
TOP_K = 4
SWIGLU_LIMIT = 7.0
SWIGLU_ALPHA = 1.702
RMS_EPS = 1e-6
CONV_WIDTH = 3

LOG2E = 1.4426950408889634
QK_SCALE = HEAD_DIM ** -0.5 * LOG2E

LANES = 128
VMEM_LIMIT = 58 * 1024 * 1024

ROW_ALIGN = 8

EXP_TM = 272
SB_ROWS = 4 * EXP_TM
EXP_TH = 512
EXP_TN = 512
ZERO_ROWS = 2 * EXP_TM
DISP_TOKENS = 256
COMB_ROWS = 128


def _cparams(sem, vmem=VMEM_LIMIT):
    return pltpu.CompilerParams(dimension_semantics=sem, vmem_limit_bytes=vmem)


def _mod_kernel(cb_ref, w_ref, b_ref, o_ref):
    w = w_ref[...]
    tn = w.shape[1]
    rows = []
    for b in range(cb_ref.shape[0]):
        cv = cb_ref[b]
        ca = cv * jax.nn.sigmoid(cv)
        cols = [jnp.sum(w[:, j * LANES:(j + 1) * LANES] * ca, axis=0, keepdims=True)
                for j in range(tn // LANES)]
        rows.append(jnp.concatenate(cols, axis=1))
    o_ref[...] = jnp.concatenate(rows, axis=0) + b_ref[...]


def _mod(c, w, bias):
    bsz, d = c.shape
    n_out = w.shape[1]
    tn = 1024
    cb = jnp.broadcast_to(c[:, :, None], (bsz, d, LANES))
    return pl.pallas_call(
        _mod_kernel,
        grid=(n_out // tn,),
        in_specs=[pl.BlockSpec((bsz, d, LANES), lambda j: (0, 0, 0)),
                  pl.BlockSpec((d, tn), lambda j: (0, j)),
                  pl.BlockSpec((1, tn), lambda j: (0, j))],
        out_specs=pl.BlockSpec((bsz, tn), lambda j: (0, j)),
        out_shape=jax.ShapeDtypeStruct((bsz, n_out), F32),
        compiler_params=_cparams(("arbitrary",)),
        name="mod",
    )(cb, w, bias.reshape(1, n_out))


def _cast_kernel(x_ref, o_ref):
    o_ref[...] = x_ref[...].astype(o_ref.dtype)


def _cast_bf16(w):
    r, c = w.shape
    tr = 512
    return pl.pallas_call(
        _cast_kernel,
        grid=(r // tr,),
        in_specs=[pl.BlockSpec((tr, c), lambda i: (i, 0))],
        out_specs=pl.BlockSpec((tr, c), lambda i: (i, 0)),
        out_shape=jax.ShapeDtypeStruct((r, c), BF16),
        compiler_params=_cparams(("arbitrary",)),
        name="cast_bf16",
    )(w)


def _rmsnorm_mod(x, g, scale, shift):
    r = lax.rsqrt(jnp.mean(x * x, axis=-1, keepdims=True) + RMS_EPS)
    return (x * r) * g * (1.0 + scale) + shift


def _inproj_kernel(nj_main, jq, x_ref, mod_ref, g_ref, wm_ref, wg_ref, wf_ref,
                   om_ref, og_ref, of_ref, h_ref, wfp_ref):
    j = pl.program_id(1)

    @pl.when(j == 0)
    def _():
        wfp_ref[...] = jnp.zeros(wfp_ref.shape, BF16)
        wfp_ref[0:wf_ref.shape[0], :] = wf_ref[...].astype(BF16)
        rows = 256

        def body(rb, carry):
            r0 = pl.multiple_of(rb * rows, rows)
            h = _rmsnorm_mod(x_ref[pl.ds(r0, rows), :], g_ref[...], mod_ref[0, 1:2, :],
                             mod_ref[0, 0:1, :])
            hb = h.astype(BF16)
            h_ref[pl.ds(r0, rows), :] = hb
            of_ref[pl.ds(r0, rows), :] = _dot_nt(hb, wfp_ref[...])
            return carry

        lax.fori_loop(0, x_ref.shape[0] // rows, body, 0)

    @pl.when(j < nj_main)
    def _():
        scale = jnp.where(jnp.logical_and(j >= jq[0], j < jq[1]), QK_SCALE, 1.0)
        om_ref[...] = (_dot_nt(h_ref[...], wm_ref[...].astype(BF16)) * scale).astype(om_ref.dtype)

    @pl.when(j >= nj_main)
    def _():
        og_ref[...] = _dot_nt(h_ref[...], wg_ref[...].astype(BF16)).astype(og_ref.dtype)


def _inproj(x2, mod3, g, w_in_t, n_main, n_f, n_gates, seq, q_cols):
    n, d = x2.shape
    tm, tn = 1024, 512
    nj_main = n_main // tn
    assert q_cols[0] % tn == 0 and q_cols[1] % tn == 0
    jq = (q_cols[0] // tn, q_cols[1] // tn)
    nj_g = n_gates // tn
    gate_row0 = n_main + n_f
    tiles_per_seq = seq // tm
    return pl.pallas_call(
        functools.partial(_inproj_kernel, nj_main, jq),
        grid=(n // tm, nj_main + nj_g),
        in_specs=[pl.BlockSpec((tm, d), lambda i, j: (i, 0)),
                  pl.BlockSpec((1, 6, d), lambda i, j: (i // tiles_per_seq, 0, 0)),
                  pl.BlockSpec((1, d), lambda i, j: (0, 0)),
                  pl.BlockSpec((tn, d), lambda i, j: (jnp.minimum(j, nj_main - 1), 0)),
                  pl.BlockSpec((pl.Element(tn), pl.Element(d)),
                               lambda i, j: (pl.multiple_of(gate_row0 + tn * jnp.maximum(j - nj_main, 0), n_f), 0)),
                  pl.BlockSpec((n_f, d), lambda i, j: (n_main // n_f, 0))],
        out_specs=[pl.BlockSpec((tm, tn), lambda i, j: (i, jnp.minimum(j, nj_main - 1))),
                   pl.BlockSpec((tm, tn), lambda i, j: (i, jnp.maximum(j - nj_main, 0))),
                   pl.BlockSpec((tm, LANES), lambda i, j: (i, 0))],
        out_shape=[jax.ShapeDtypeStruct((n, n_main), BF16),
                   jax.ShapeDtypeStruct((n, n_gates), BF16),
                   jax.ShapeDtypeStruct((n, LANES), F32)],
        scratch_shapes=[pltpu.VMEM((tm, d), BF16), pltpu.VMEM((LANES, d), BF16)],
        compiler_params=_cparams(("arbitrary", "arbitrary")),
        name="inproj",
    )(x2, mod3, g.reshape(1, d), w_in_t, w_in_t, w_in_t)


def _cumf_kernel(blocks_per_seq, f_ref, bf_ref, fc_ref, carry_ref):
    i = pl.program_id(0)

    @pl.when(i % blocks_per_seq == 0)
    def _():
        carry_ref[...] = jnp.zeros_like(carry_ref)

    z = f_ref[...] + bf_ref[...]
    lf = -(jnp.maximum(-z, 0.0) + jnp.log1p(jnp.exp(-jnp.abs(z))))
    t = z.shape[0]
    row = lax.broadcasted_iota(jnp.int32, (t, t), 0)
    col = lax.broadcasted_iota(jnp.int32, (t, t), 1)
    tri = jnp.where(col <= row, 1.0, 0.0).astype(BF16)
    parts = [jnp.dot(tri, p, preferred_element_type=F32) for p in _split3(lf)]
    fb = (parts[0] + (parts[1] + parts[2])) + carry_ref[...]
    fc_ref[...] = fb
    carry_ref[...] = fb[t - 1:t, :]


def _cumf(f_logit, b_forget, seq):
    n = f_logit.shape[0]
    t = 512
    bf = jnp.pad(b_forget, (0, LANES - b_forget.shape[0])).reshape(1, LANES)
    return pl.pallas_call(
        functools.partial(_cumf_kernel, seq // t),
        grid=(n // t,),
        in_specs=[pl.BlockSpec((t, LANES), lambda i: (i, 0)),
                  pl.BlockSpec((1, LANES), lambda i: (0, 0))],
        out_specs=pl.BlockSpec((t, LANES), lambda i: (i, 0)),
        out_shape=jax.ShapeDtypeStruct((n, LANES), F32),
        scratch_shapes=[pltpu.VMEM((1, LANES), F32)],
        compiler_params=_cparams(("arbitrary",)),
        name="cumf",
    )(f_logit, bf)


def _dot_nt(a, b):
    return lax.dot_general(a, b, (((1,), (1,)), ((), ())), preferred_element_type=F32)


def _split3(x):
    hi = x.astype(BF16)
    r1 = x - hi.astype(F32)
    mid = r1.astype(BF16)
    lo = (r1 - mid.astype(F32)).astype(BF16)
    return hi, mid, lo


def _bias_lanes(f, head, lane0, sign):
    r = lax.broadcasted_iota(jnp.int32, (LANES, LANES), 0)
    c = lax.broadcasted_iota(jnp.int32, (LANES, LANES), 1)
    out = None
    for j, part in enumerate(_split3(f)):
        sel = jnp.where(jnp.logical_and(r == head, c == lane0 + j), sign, 0.0).astype(BF16)
        term = jnp.dot(part, sel, preferred_element_type=F32)
        out = term if out is None else out + term
    return out


def _attn_kernel(tq, tk, q_ref, k_ref, v_ref, fc_ref, o_ref, kp_ref, qp_ref, vt_ref, m_ref, l_ref,
                 acc_ref, sa_ref, sb_ref):
    hp = pl.program_id(1)
    qi = pl.program_id(2)
    seq = k_ref.shape[0]
    lane = lax.broadcasted_iota(jnp.int32, (1, LANES), 1)
    own = (lane < HEAD_DIM, lane >= HEAD_DIM)
    spare = (HEAD_DIM, 0)
    ones_k = tuple(jnp.where(jnp.logical_and(lane >= spare[h] + 3, lane < spare[h] + 6), 1.0, 0.0)
                   for h in range(2))
    ones_q = tuple(jnp.where(jnp.logical_and(lane >= spare[h], lane < spare[h] + 3), 1.0, 0.0)
                   for h in range(2))

    @pl.when(qi == 0)
    def _():
        def prep(ci, carry):
            r0 = pl.multiple_of(ci * tk, tk)
            f = fc_ref[pl.ds(r0, tk), :] * LOG2E
            k = k_ref[pl.ds(r0, tk), :]
            q = q_ref[pl.ds(r0, tk), :]
            for h in range(2):
                aux = _bias_lanes(f, 2 * hp + h, spare[h], -1.0) + ones_k[h]
                kp_ref[h, pl.ds(r0, tk), :] = jnp.where(own[h], k, aux.astype(BF16))
                aux = _bias_lanes(f, 2 * hp + h, spare[h] + 3, 1.0) + ones_q[h]
                qp_ref[h, pl.ds(r0, tk), :] = jnp.where(own[h], q, aux.astype(BF16))
            vt_ref[:, pl.ds(r0, tk)] = v_ref[pl.ds(r0, tk), :].astype(F32).T.astype(BF16)
            return carry
        lax.fori_loop(0, seq // tk, prep, 0)

    q0 = pl.multiple_of(qi * tq, tq)
    qp = [qp_ref[h, pl.ds(q0, tq), :] for h in range(2)]

    m_ref[...] = jnp.full(m_ref.shape, -jnp.inf, F32)
    l_ref[...] = jnp.zeros(l_ref.shape, F32)
    acc_ref[...] = jnp.zeros(acc_ref.shape, F32)

    def scores(kt, s_ref):
        k0 = pl.multiple_of(kt * tk, tk)
        for h in range(2):
            s_ref[h] = _dot_nt(kp_ref[h, pl.ds(k0, tk), :], qp[h])

    def update(kt, s_ref, masked):
        k0 = pl.multiple_of(kt * tk, tk)
        for h in range(2):
            s = s_ref[h]
            if masked:
                kr = lax.broadcasted_iota(jnp.int32, (tk, tq), 0)
                qc = lax.broadcasted_iota(jnp.int32, (tk, tq), 1)
                s = jnp.where(kr <= qc, s, -jnp.inf)
            m_old = m_ref[h]
            m_new = jnp.maximum(m_old, jnp.max(s, axis=0, keepdims=True))
            alpha = jnp.exp2(m_old - m_new)
            p = jnp.exp2(s - m_new)
            l_ref[h] = l_ref[h] * alpha + jnp.sum(p, axis=0, keepdims=True)
            m_ref[h] = m_new
            rows = pl.ds(h * HEAD_DIM, HEAD_DIM)
            pv = jnp.dot(vt_ref[rows, pl.ds(k0, tk)], p.astype(BF16), preferred_element_type=F32)
            acc_ref[rows, :] = acc_ref[rows, :] * alpha + pv

    scores(0, sa_ref)

    def pair(j, carry):
        scores(2 * j + 1, sb_ref)
        update(2 * j, sa_ref, False)
        scores(2 * j + 2, sa_ref)
        update(2 * j + 1, sb_ref, False)
        return carry

    lax.fori_loop(0, qi // 2, pair, 0)

    @pl.when(qi % 2 == 0)
    def _():
        update(qi, sa_ref, True)

    @pl.when(qi % 2 == 1)
    def _():
        scores(qi, sb_ref)
        update(qi - 1, sa_ref, False)
        update(qi, sb_ref, True)

    out_t = jnp.concatenate([acc_ref[pl.ds(h * HEAD_DIM, HEAD_DIM), :] / l_ref[h] for h in range(2)],
                            axis=0)
    o_ref[...] = out_t.T.astype(o_ref.dtype)


def _attn(proj, fcol, bsz, seq, q_col0, k_col0, v_col0):
    tq = tk = 512
    nq = seq // tq
    n_pairs = N_HEADS * HEAD_DIM // LANES
    qb, kb, vb = q_col0 // LANES, k_col0 // LANES, v_col0 // LANES
    return pl.pallas_call(
        functools.partial(_attn_kernel, tq, tk),
        grid=(bsz, n_pairs, nq),
        in_specs=[pl.BlockSpec((seq, LANES), lambda b, hp, qi: (b, qb + hp)),
                  pl.BlockSpec((seq, LANES), lambda b, hp, qi: (b, kb + hp)),
                  pl.BlockSpec((seq, LANES), lambda b, hp, qi: (b, vb + hp)),
                  pl.BlockSpec((seq, LANES), lambda b, hp, qi: (b, 0))],
        out_specs=pl.BlockSpec((tq, LANES), lambda b, hp, qi: (b * nq + qi, hp)),
        out_shape=jax.ShapeDtypeStruct((bsz * seq, N_HEADS * HEAD_DIM), BF16),
        scratch_shapes=[pltpu.VMEM((2, seq, LANES), BF16), pltpu.VMEM((2, seq, LANES), BF16),
                        pltpu.VMEM((LANES, seq), BF16),
                        pltpu.VMEM((2, 1, tq), F32), pltpu.VMEM((2, 1, tq), F32),
                        pltpu.VMEM((LANES, tq), F32),
                        pltpu.VMEM((2, tk, tq), F32), pltpu.VMEM((2, tk, tq), F32)],
        compiler_params=_cparams(("arbitrary", "arbitrary", "arbitrary")),
        name="attn",
    )(proj, proj, proj, fcol)


def _mix_kernel(tiles_per_seq, xc_ref, cb_ref, cc_ref, hxc_ref, hcc_ref, att_ref, gc_ref, ga_ref,
                x_ref, mod_ref, g2_ref, cw_ref, wc_ref, wa_ref, wo_ref, rw_ref, rb_ref,
                x1_ref, h2_ref, lg_ref):
    i = pl.program_id(0)
    u = cc_ref[...].astype(F32) * xc_ref[...].astype(F32)
    hu = hcc_ref[...].astype(F32) * hxc_ref[...].astype(F32)
    hu = jnp.where(i % tiles_per_seq == 0, 0.0, hu)
    nh = hu.shape[0]
    row = lax.broadcasted_iota(jnp.int32, u.shape, 0)
    u1 = jnp.where(row == 0, hu[nh - 1:nh, :], pltpu.roll(u, 1, axis=0))
    u2 = jnp.where(row == 0, hu[nh - 2:nh - 1, :],
                   jnp.where(row == 1, hu[nh - 1:nh, :], pltpu.roll(u, 2, axis=0)))
    conv = cw_ref[0:1, :] * u2 + cw_ref[1:2, :] * u1 + cw_ref[2:3, :] * u
    z = (cb_ref[...].astype(F32) * conv).astype(BF16)
    y_conv = jnp.dot(z, wc_ref[...], preferred_element_type=F32)
    y_attn = jnp.dot(att_ref[...], wa_ref[...], preferred_element_type=F32)
    merged = (jax.nn.sigmoid(gc_ref[...].astype(F32)) * y_conv
              + jax.nn.sigmoid(ga_ref[...].astype(F32)) * y_attn)
    o = jnp.dot(merged.astype(BF16), wo_ref[...], preferred_element_type=F32)
    x1 = x_ref[...] + mod_ref[0, 2:3, :] * o
    x1_ref[...] = x1
    h2 = _rmsnorm_mod(x1, g2_ref[...], mod_ref[0, 4:5, :], mod_ref[0, 3:4, :])
    h2_ref[...] = h2
    h_hi, h_lo, _ = _split3(h2)
    w_hi, w_lo, _ = _split3(rw_ref[...])
    lg_ref[...] = (jnp.dot(h_hi, w_hi, preferred_element_type=F32)
                   + (jnp.dot(h_hi, w_lo, preferred_element_type=F32)
                      + jnp.dot(h_lo, w_hi, preferred_element_type=F32))) + rb_ref[...]


def _mix(proj, gates, att, x2, mod3, g2, conv_w, wc, wa, wo, rw, rb, seq, d_conv):
    n, d = x2.shape
    tm = 256
    halo = 16
    tiles_per_seq = seq // tm
    d_attn = att.shape[1]

    def resident(shape):
        return pl.BlockSpec(shape, lambda i: (0,) * len(shape), pipeline_mode=pl.Buffered(1))

    return pl.pallas_call(
        functools.partial(_mix_kernel, tiles_per_seq),
        grid=(n // tm,),
        in_specs=[pl.BlockSpec((tm, d_conv), lambda i: (i, 0)),
                  pl.BlockSpec((tm, d_conv), lambda i: (i, 1)),
                  pl.BlockSpec((tm, d_conv), lambda i: (i, 2)),
                  pl.BlockSpec((halo, d_conv), lambda i: (jnp.maximum(i * (tm // halo) - 1, 0), 0)),
                  pl.BlockSpec((halo, d_conv), lambda i: (jnp.maximum(i * (tm // halo) - 1, 0), 2)),
                  pl.BlockSpec((tm, d_attn), lambda i: (i, 0)),
                  pl.BlockSpec((tm, d), lambda i: (i, 0)),
                  pl.BlockSpec((tm, d), lambda i: (i, 1)),
                  pl.BlockSpec((tm, d), lambda i: (i, 0)),
                  pl.BlockSpec((1, 6, d), lambda i: (i // tiles_per_seq, 0, 0)),
                  resident((1, d)),
                  resident((CONV_WIDTH, d_conv)),
                  resident((d_conv, d)),
                  resident((d_attn, d)),
                  resident((d, d)),
                  resident((d, LANES)),
                  resident((1, LANES))],
        out_specs=[pl.BlockSpec((tm, d), lambda i: (i, 0)),
                   pl.BlockSpec((tm, d), lambda i: (i, 0)),
                   pl.BlockSpec((tm, LANES), lambda i: (i, 0))],
        out_shape=[jax.ShapeDtypeStruct((n, d), F32),
                   jax.ShapeDtypeStruct((n, d), F32),
                   jax.ShapeDtypeStruct((n, LANES), F32)],
        compiler_params=_cparams(("arbitrary",)),
        name="mix",
    )(proj, proj, proj, proj, proj, att, gates, gates, x2, mod3, g2.reshape(1, d), conv_w,
      wc, wa, wo, rw, rb)


def _route_kernel(lg_ref, idx_ref, prob_ref, cnt_ref, carry_ref):
    i = pl.program_id(0)

    @pl.when(i == 0)
    def _():
        carry_ref[...] = jnp.zeros_like(carry_ref)

    tt = lg_ref.shape[0]
    lane = lax.broadcasted_iota(jnp.int32, (tt, LANES), 1)
    lane_f = lane.astype(F32)
    l = jnp.where(lane < N_EXPERTS, lg_ref[...], -jnp.inf)
    onehot = jnp.zeros((tt, LANES), F32)
    vals, ids = [], []
    for _ in range(TOP_K):
        m = jnp.max(l, axis=1, keepdims=True)
        idx = jnp.min(jnp.where(l == m, lane_f, float(LANES)), axis=1, keepdims=True)
        sel = lane_f == idx
        vals.append(m)
        ids.append(idx)
        onehot = jnp.where(sel, 1.0, onehot)
        l = jnp.where(sel, -jnp.inf, l)
    es = [jnp.exp(v - vals[0]) for v in vals]
    denom = es[0] + es[1] + es[2] + es[3]
    r = lax.broadcasted_iota(jnp.int32, (tt, tt), 0)
    c = lax.broadcasted_iota(jnp.int32, (tt, tt), 1)
    before = jnp.where(c < r, 1.0, 0.0).astype(BF16)
    cnt_before = jnp.dot(before, onehot.astype(BF16), preferred_element_type=F32) + carry_ref[...]
    idx_out = jnp.zeros((tt, LANES), jnp.int32)
    prob_out = jnp.zeros((tt, LANES), F32)
    for k in range(TOP_K):
        rank = jnp.sum(jnp.where(lane_f == ids[k], cnt_before, 0.0), axis=1, keepdims=True)
        idx_out = jnp.where(lane == k, ids[k].astype(jnp.int32), idx_out)
        idx_out = jnp.where(lane == TOP_K + k, rank.astype(jnp.int32), idx_out)
        prob_out = jnp.where(lane == k, es[k] / denom, prob_out)
    idx_ref[...] = idx_out
    prob_ref[...] = prob_out
    carry_ref[...] = carry_ref[...] + jnp.sum(onehot, axis=0, keepdims=True)
    cnt_ref[...] = carry_ref[...].astype(jnp.int32)


def _route(logits):
    n = logits.shape[0]
    tt = 512
    return pl.pallas_call(
        _route_kernel,
        grid=(n // tt,),
        in_specs=[pl.BlockSpec((tt, LANES), lambda i: (i, 0))],
        out_specs=[pl.BlockSpec((tt, LANES), lambda i: (i, 0)),
                   pl.BlockSpec((tt, LANES), lambda i: (i, 0)),
                   pl.BlockSpec((1, LANES), lambda i: (0, 0))],
        out_shape=[jax.ShapeDtypeStruct((n, LANES), jnp.int32),
                   jax.ShapeDtypeStruct((n, LANES), F32),
                   jax.ShapeDtypeStruct((1, LANES), jnp.int32)],
        scratch_shapes=[pltpu.VMEM((1, LANES), F32)],
        compiler_params=_cparams(("arbitrary",)),
        name="route",
    )(logits)


def _zero_fill_rows(zero_rows, dst_hbm, row0, n_rows, sem, start):
    off = row0
    size = 1 << (zero_rows.shape[0].bit_length() - 1)
    while size >= ROW_ALIGN:
        @pl.when((n_rows & size) != 0)
        def _(off=off, size=size):
            cp = pltpu.make_async_copy(zero_rows.at[pl.ds(0, size), :],
                                       dst_hbm.at[pl.ds(pl.multiple_of(off, ROW_ALIGN), size), :], sem)
            if start:
                cp.start()
            else:
                cp.wait()
        off = off + (n_rows & size)
        size //= 2


def _dispatch_kernel(gs_ref, cnt_ref, gend_ref, dest_ref, h_ref, xs_hbm, zbuf, sem):
    j = pl.program_id(0)
    tt = h_ref.shape[0]
    n_total = xs_hbm.shape[0]

    @pl.when(j == 0)
    def _():
        zbuf[...] = jnp.zeros(zbuf.shape, F32)
        tail0 = gend_ref[0]

        def pad_rows(start):
            def group(e, carry):
                cnt = cnt_ref[e]

                def row(r, c2):
                    cp = pltpu.make_async_copy(zbuf.at[pl.ds(0, 1), :],
                                               xs_hbm.at[pl.ds(gs_ref[e] + r, 1), :], sem.at[2])
                    if start:
                        cp.start()
                    else:
                        cp.wait()
                    return c2
                lax.fori_loop(cnt, (cnt + ROW_ALIGN - 1) & (-ROW_ALIGN), row, 0)
                return carry
            lax.fori_loop(0, N_EXPERTS, group, 0)

        for start in (True, False):
            _zero_fill_rows(zbuf, xs_hbm, tail0, n_total - tail0, sem.at[2], start)
            pad_rows(start)

    def tok(r, carry):
        for k in range(TOP_K):
            d = dest_ref[0, 0, r * TOP_K + k]
            pltpu.make_async_copy(h_ref.at[pl.ds(r, 1), :], xs_hbm.at[pl.ds(d, 1), :],
                                  sem.at[0]).start(priority=k % 2)
        return carry
    lax.fori_loop(0, tt, tok, 0, unroll=4)

    for k in range(TOP_K):
        pltpu.make_async_copy(h_ref, xs_hbm.at[pl.ds(0, tt), :], sem.at[0]).wait()


def _dispatch(h2, dest, g_start, cnt, g_end, n_rows_total):
    n, d = h2.shape
    tt = DISP_TOKENS
    dest3 = dest.reshape(n // tt, 1, tt * TOP_K)
    grid_spec = pltpu.PrefetchScalarGridSpec(
        num_scalar_prefetch=3,
        grid=(n // tt,),
        in_specs=[pl.BlockSpec((1, 1, tt * TOP_K), lambda j, gs, ct, ge: (j, 0, 0),
                               memory_space=pltpu.SMEM),
                  pl.BlockSpec((tt, d), lambda j, gs, ct, ge: (j, 0))],
        out_specs=pl.BlockSpec(memory_space=pl.ANY),
        scratch_shapes=[pltpu.VMEM((ZERO_ROWS, d), F32), pltpu.SemaphoreType.DMA((3,))],
    )
    return pl.pallas_call(
        _dispatch_kernel,
        grid_spec=grid_spec,
        out_shape=jax.ShapeDtypeStruct((n_rows_total, d), F32),
        compiler_params=_cparams(("arbitrary",)),
        name="dispatch",
    )(g_start, cnt, g_end, dest3, h2)


def _for_blocks(n_blk, n_max, pre, body):
    for n in range(4, n_max + 1):
        @pl.when(n_blk == n)
        def _(n=n):
            pre()
            for rb in range(n):
                body(rb)

    @pl.when(n_blk < 4)
    def _():
        pre()

        @pl.when(n_blk >= 2)
        def _():
            body(0)
            body(1)

        @pl.when(n_blk % 2 == 1)
        def _():
            body(n_blk - 1)


def _expert_kernel(n_act, n_down, sbe_ref, sbr_ref, sbn_ref, ns_ref, gend_ref,
                   xs_hbm, wg_ref, wu_ref, wd_ref, bg_ref, bu_ref, bd_ref, ys_hbm,
                   stage, xbf, act, out, wgu_ref, wdb_ref, ld_sem, st_sem):
    s = pl.program_id(0)
    c = pl.program_id(1)
    th = EXP_TH
    n_super = ns_ref[0]
    n_blocks = SB_ROWS // EXP_TM

    def slot_ref(slot):
        return stage.at[pl.ds(slot * EXP_TM, EXP_TM), :]

    def x_copy(sb, rb, slot):
        r0 = pl.multiple_of(sbr_ref[sb] + rb * EXP_TM, ROW_ALIGN)
        return pltpu.make_async_copy(xs_hbm.at[pl.ds(r0, EXP_TM), :], slot_ref(slot), ld_sem.at[slot])

    def y_copy(sb, rb):
        r0 = pl.multiple_of(sbr_ref[sb] + rb * EXP_TM, ROW_ALIGN)
        return pltpu.make_async_copy(out.at[pl.ds(rb * EXP_TM, EXP_TM), :],
                                     ys_hbm.at[pl.ds(r0, EXP_TM), :], st_sem.at[rb])

    @pl.when(jnp.logical_and(s == 0, c == 0))
    def _():
        stage[...] = jnp.zeros(stage.shape, F32)
        tail0 = gend_ref[0]
        for start in (True, False):
            _zero_fill_rows(stage, ys_hbm, tail0, ys_hbm.shape[0] - tail0, st_sem.at[0], start)
        x_copy(0, 0, 0).start()

    @pl.when(s < n_super)
    def _():
        n_blk = sbn_ref[s]
        n_blk_prev = sbn_ref[jnp.maximum(s - 1, 0)]
        expert = sbe_ref[s]

        @pl.when(c == 0)
        def _():
            for rb in range(n_blocks):
                @pl.when(rb < n_blk)
                def _(rb=rb):
                    if rb + 1 < n_blocks:
                        @pl.when(rb + 1 < n_blk)
                        def _():
                            x_copy(s, rb + 1, (rb + 1) % 2).start()
                    x_copy(s, rb, rb % 2).wait()
                    xbf[pl.ds(rb * EXP_TM, EXP_TM), :] = slot_ref(rb % 2)[...].astype(BF16)

        @pl.when(c < n_act)
        def _():
            cols = pl.ds(pl.multiple_of(c * th, th), th)
            bg = bg_ref[pl.ds(expert, 1), cols]
            bu = bu_ref[pl.ds(expert, 1), cols]

            def cast_weights():
                wgu_ref[:, :th] = wg_ref[0].astype(BF16)
                wgu_ref[:, th:] = wu_ref[0].astype(BF16)

            def act_block(rb):
                rows = pl.ds(pl.multiple_of(rb * EXP_TM, EXP_TM), EXP_TM)
                gu = jnp.dot(xbf[rows, :], wgu_ref[...], preferred_element_type=F32)
                g = jnp.minimum(gu[:, :th] + bg, SWIGLU_LIMIT)
                u = jnp.clip(gu[:, th:] + bu, -SWIGLU_LIMIT, SWIGLU_LIMIT)
                a = (u + 1.0) * g * jax.nn.sigmoid(SWIGLU_ALPHA * g)
                act[rows, cols] = a.astype(BF16)

            _for_blocks(n_blk, n_blocks, cast_weights, act_block)

        @pl.when(c >= n_act)
        def _():
            cols = pl.ds(pl.multiple_of((c - n_act) * EXP_TN, EXP_TN), EXP_TN)
            bd = bd_ref[pl.ds(expert, 1), cols]
            for rb in range(n_blocks):
                @pl.when(jnp.logical_and(c == n_act, jnp.logical_and(s > 0, rb < n_blk_prev)))
                def _(rb=rb):
                    y_copy(s - 1, rb).wait()

            def cast_weights():
                wdb_ref[...] = wd_ref[0].astype(BF16)

            def down_block(rb):
                rows = pl.ds(pl.multiple_of(rb * EXP_TM, EXP_TM), EXP_TM)
                out[rows, cols] = jnp.dot(act[rows, :], wdb_ref[...],
                                          preferred_element_type=F32) + bd

            _for_blocks(n_blk, n_blocks, cast_weights, down_block)

            @pl.when(c == n_act + n_down - 1)
            def _():
                for rb in range(n_blocks):
                    @pl.when(rb < n_blk)
                    def _(rb=rb):
                        y_copy(s, rb).start()

                @pl.when(s + 1 < n_super)
                def _():
                    x_copy(s + 1, 0, 0).start()

                @pl.when(s + 1 == n_super)
                def _():
                    for rb in range(n_blocks):
                        @pl.when(rb < n_blk)
                        def _(rb=rb):
                            y_copy(s, rb).wait()


def _expert(xs, sb_expert, sb_row0, sb_nblk, n_super, g_end, wg, wu, wd, bg, bu, bd):
    n_rows_total, d = xs.shape
    s_max = sb_expert.shape[0]
    d_exp = wg.shape[2]
    assert d_exp % EXP_TH == 0 and d % EXP_TN == 0 and ZERO_ROWS == 2 * EXP_TM
    n_act = d_exp // EXP_TH
    n_down = d // EXP_TN
    ne = wg.shape[0]

    def e_eff(s, sbe, ns):
        return sbe[jnp.minimum(s, ns[0] - 1)]

    def c_act(s, c, ns):
        return jnp.where(s < ns[0], jnp.minimum(c, n_act - 1), n_act - 1)

    def c_down(s, c, ns):
        return jnp.where(s < ns[0], jnp.maximum(c - n_act, 0), n_down - 1)

    act_w = pl.BlockSpec((1, d, EXP_TH),
                         lambda s, c, sbe, sbr, sbn, ns, ge: (e_eff(s, sbe, ns), 0, c_act(s, c, ns)))
    bias = pl.BlockSpec((ne, d_exp), lambda s, c, sbe, sbr, sbn, ns, ge: (0, 0))
    grid_spec = pltpu.PrefetchScalarGridSpec(
        num_scalar_prefetch=5,
        grid=(s_max, n_act + n_down),
        in_specs=[
            pl.BlockSpec(memory_space=pl.ANY),
            act_w, act_w,
            pl.BlockSpec((1, d_exp, EXP_TN),
                         lambda s, c, sbe, sbr, sbn, ns, ge: (e_eff(s, sbe, ns), 0, c_down(s, c, ns))),
            bias, bias,
            pl.BlockSpec((ne, d), lambda s, c, sbe, sbr, sbn, ns, ge: (0, 0)),
        ],
        out_specs=pl.BlockSpec(memory_space=pl.ANY),
        scratch_shapes=[pltpu.VMEM((ZERO_ROWS, d), F32), pltpu.VMEM((SB_ROWS, d), BF16),
                        pltpu.VMEM((SB_ROWS, d_exp), BF16), pltpu.VMEM((SB_ROWS, d), F32),
                        pltpu.VMEM((d, 2 * EXP_TH), BF16), pltpu.VMEM((d_exp, EXP_TN), BF16),
                        pltpu.SemaphoreType.DMA((2,)),
                        pltpu.SemaphoreType.DMA((SB_ROWS // EXP_TM,))],
    )
    return pl.pallas_call(
        functools.partial(_expert_kernel, n_act, n_down),
        grid_spec=grid_spec,
        out_shape=jax.ShapeDtypeStruct((n_rows_total, d), F32),
        compiler_params=_cparams(("arbitrary", "arbitrary")),
        name="expert",
    )(sb_expert, sb_row0, sb_nblk, n_super, g_end, xs, wg, wu, wd, bg, bu, bd)


def _combine_start(y_hbm, pos_ref, stage_slot, sem):
    tt = stage_slot.shape[1]

    def body(r, carry):
        for k in range(TOP_K):
            p = pos_ref[0, 0, r * TOP_K + k]
            pltpu.make_async_copy(y_hbm.at[pl.ds(p, 1), :], stage_slot.at[k, pl.ds(r, 1), :],
                                  sem).start(priority=k % 2)
        return carry
    lax.fori_loop(0, tt, body, 0, unroll=4)


def _combine_kernel(final_norm, pos_cur_ref, pos_nxt_ref, y_hbm, prob_ref, x1_ref, mod_ref, g_ref,
                    o_ref, stage, sem):
    j = pl.program_id(0)
    nj = pl.num_programs(0)
    tt = stage.shape[2]

    @pl.when(j == 0)
    def _():
        _combine_start(y_hbm, pos_cur_ref, stage.at[0], sem.at[0])

    @pl.when(j + 1 < nj)
    def _():
        nxt = (j + 1) % 2
        _combine_start(y_hbm, pos_nxt_ref, stage.at[nxt], sem.at[nxt])

    cur = j % 2
    for k in range(TOP_K):
        pltpu.make_async_copy(y_hbm.at[pl.ds(0, tt), :], stage.at[cur, k], sem.at[cur]).wait()
    prob = prob_ref[...]
    y = prob[:, 0:1] * stage[cur, 0]
    for k in range(1, TOP_K):
        y = y + prob[:, k:k + 1] * stage[cur, k]
    x2 = x1_ref[...] + mod_ref[0, 5:6, :] * y
    if final_norm:
        r = lax.rsqrt(jnp.mean(x2 * x2, axis=-1, keepdims=True) + RMS_EPS)
        x2 = (x2 * r) * g_ref[...]
    o_ref[...] = x2


def _combine(yb, pos, probs, x1, mod3, g, seq, final_norm):
    n, d = x1.shape
    tt = COMB_ROWS
    nt = n // tt
    tiles_per_seq = seq // tt
    pos3 = pos.reshape(nt, 1, tt * TOP_K)
    return pl.pallas_call(
        functools.partial(_combine_kernel, final_norm),
        grid=(nt,),
        in_specs=[pl.BlockSpec((1, 1, tt * TOP_K), lambda j: (j, 0, 0), memory_space=pltpu.SMEM),
                  pl.BlockSpec((1, 1, tt * TOP_K), lambda j: (jnp.minimum(j + 1, nt - 1), 0, 0),
                               memory_space=pltpu.SMEM),
                  pl.BlockSpec(memory_space=pl.ANY),
                  pl.BlockSpec((tt, LANES), lambda j: (j, 0)),
                  pl.BlockSpec((tt, d), lambda j: (j, 0)),
                  pl.BlockSpec((1, 6, d), lambda j: (j // tiles_per_seq, 0, 0)),
                  pl.BlockSpec((1, d), lambda j: (0, 0))],
        out_specs=pl.BlockSpec((tt, d), lambda j: (j, 0)),
        out_shape=jax.ShapeDtypeStruct((n, d), F32),
        scratch_shapes=[pltpu.VMEM((2, TOP_K, tt, d), F32), pltpu.SemaphoreType.DMA((2,))],
        compiler_params=_cparams(("arbitrary",)),
        name="combine",
    )(pos3, pos3, yb, probs, x1, mod3, g.reshape(1, d))


def _count_le(sorted_ends, q):
    return jnp.sum((sorted_ends[None, :] <= q[:, None]).astype(jnp.int32), axis=1)


def _plan(counts, n_tok):
    s_max = -(-n_tok * TOP_K // SB_ROWS) + N_EXPERTS
    cnt = counts[0, :N_EXPERTS]
    cnt_al = (cnt + ROW_ALIGN - 1) // ROW_ALIGN * ROW_ALIGN
    g_end = jnp.cumsum(cnt_al)
    g_start = g_end - cnt_al
    nsb_e = (cnt + SB_ROWS - 1) // SB_ROWS
    sb_end = jnp.cumsum(nsb_e)
    sb_start = sb_end - nsb_e
    n_super = sb_end[-1]
    s_ids = jnp.arange(s_max, dtype=jnp.int32)
    sb_expert = jnp.minimum(_count_le(sb_end, s_ids), N_EXPERTS - 1)
    j_in = s_ids - sb_start[sb_expert]
    live = s_ids < n_super
    sb_rows = jnp.where(live, jnp.clip(cnt[sb_expert] - j_in * SB_ROWS, 0, SB_ROWS), 0)
    sb_row0 = jnp.where(live, g_start[sb_expert] + j_in * SB_ROWS, 0)
    sb_nblk = (sb_rows + EXP_TM - 1) // EXP_TM
    i32 = lambda a: a.astype(jnp.int32)
    return (i32(g_start), i32(cnt), i32(g_end[-1:]), i32(sb_expert), i32(sb_row0), i32(sb_nblk),
            i32(n_super.reshape(1)))


def _dest_kernel(ir_ref, gs_ref, o_ref):
    ir = ir_ref[...].astype(F32)
    lane = lax.broadcasted_iota(jnp.int32, ir.shape, 1)
    lane_f = lane.astype(F32)
    out = jnp.zeros(ir.shape, F32)
    for k in range(TOP_K):
        e = jnp.sum(jnp.where(lane == k, ir, 0.0), axis=1, keepdims=True)
        r = jnp.sum(jnp.where(lane == TOP_K + k, ir, 0.0), axis=1, keepdims=True)
        g = jnp.sum(jnp.where(lane_f == e, gs_ref[...], 0.0), axis=1, keepdims=True)
        out = jnp.where(lane == k, g + r, out)
    o_ref[...] = out.astype(jnp.int32)


def _dest(idx_rank, g_start):
    n = idx_rank.shape[0]
    tt = 1024
    gs = jnp.pad(g_start.astype(F32), (0, LANES - N_EXPERTS)).reshape(1, LANES)
    return pl.pallas_call(
        _dest_kernel,
        grid=(n // tt,),
        in_specs=[pl.BlockSpec((tt, LANES), lambda i: (i, 0)),
                  pl.BlockSpec((1, LANES), lambda i: (0, 0))],
        out_specs=pl.BlockSpec((tt, LANES), lambda i: (i, 0)),
        out_shape=jax.ShapeDtypeStruct((n, LANES), jnp.int32),
        compiler_params=_cparams(("arbitrary",)),
        name="dest",
    )(idx_rank, gs)


def kernel(x, c, ada_w, ada_b, norm_mix_g, w_in, b_forget, conv_w, w_conv_out, w_attn_out, w_out,
           norm_ffn_g, router_w, router_b, exp_w_gate, exp_b_gate, exp_w_up, exp_b_up,
           exp_w_down, exp_b_down, final_norm_g):
    bsz, seq, d = x.shape
    n = bsz * seq
    d_conv = conv_w.shape[2]
    d_attn = w_attn_out.shape[1]
    n_main = 3 * d_conv + 3 * d_attn
    x2 = x.reshape(n, d)
    out = x2
    for l in range(ada_w.shape[0]):
        mod3 = _mod(c, ada_w[l], ada_b[l]).reshape(bsz, 6, d)
        proj, gates, f_logit = _inproj(out, mod3, norm_mix_g[l], w_in[l].T, n_main, N_HEADS,
                                       w_in.shape[2] - n_main - N_HEADS, seq,
                                       (3 * d_conv, 3 * d_conv + d_attn))
        fcol = _cumf(f_logit, b_forget[l], seq)
        att = _attn(proj, fcol, bsz, seq, 3 * d_conv, 3 * d_conv + d_attn, 3 * d_conv + 2 * d_attn)
        rw = jnp.pad(router_w[l], ((0, 0), (0, LANES - N_EXPERTS)))
        rb = jnp.pad(router_b[l], (0, LANES - N_EXPERTS)).reshape(1, LANES)
        x1, h2, logits = _mix(proj, gates, att, out, mod3, norm_ffn_g[l], conv_w[l],
                              _cast_bf16(w_conv_out[l]), _cast_bf16(w_attn_out[l]), _cast_bf16(w_out[l]),
                              rw, rb, seq, d_conv)
        idx_rank, probs, counts = _route(logits)
        g_start, cnt, g_end, sb_expert, sb_row0, sb_nblk, n_super = _plan(counts, n)
        dest = _dest(idx_rank, g_start)[:, :TOP_K].reshape(-1)
        n_rows_total = n * TOP_K + N_EXPERTS * ROW_ALIGN + EXP_TM
        xs = _dispatch(h2, dest, g_start, cnt, g_end, n_rows_total)
        ys = _expert(xs, sb_expert, sb_row0, sb_nblk, n_super, g_end, exp_w_gate[l], exp_w_up[l],
                     exp_w_down[l], exp_b_gate[l], exp_b_up[l], exp_b_down[l])
        out = _combine(ys, dest, probs, x1, mod3, final_norm_g, seq,
                       final_norm=(l == ada_w.shape[0] - 1))
    return out.reshape(bsz, seq, d)
```

```python
import functools

import jax
import jax.numpy as jnp
from jax import lax
from jax.experimental import pallas as pl
from jax.experimental.pallas import tpu as pltpu

F32 = jnp.float32
BF16 = jnp.bfloat16

N_HEADS = 16
HEAD_DIM = 64
N_EXPERTS = 32
TOP_K = 4
SWIGLU_LIMIT = 7.0
SWIGLU_ALPHA = 1.702
RMS_EPS = 1e-6
CONV_WIDTH = 3

LOG2E = 1.4426950408889634
QK_SCALE = HEAD_DIM ** -0.5 * LOG2E

LANES = 128
VMEM_LIMIT = 58 * 1024 * 1024

ROW_ALIGN = 8

EXP_TM = 272
SB_ROWS = 6 * EXP_TM
EXP_TH = 256
EXP_TN = 512
ZERO_ROWS = 2 * EXP_TM
DISP_TOKENS = 256
COMB_ROWS = 128


def _cparams(sem, vmem=VMEM_LIMIT):
    return pltpu.CompilerParams(dimension_semantics=sem, vmem_limit_bytes=vmem)


def _mod_kernel(cb_ref, w_ref, b_ref, o_ref):
    w = w_ref[...]
    tn = w.shape[1]
    rows = []
    for b in range(cb_ref.shape[0]):
        cv = cb_ref[b]
        ca = cv * jax.nn.sigmoid(cv)
        cols = [jnp.sum(w[:, j * LANES:(j + 1) * LANES] * ca, axis=0, keepdims=True)
                for j in range(tn // LANES)]
        rows.append(jnp.concatenate(cols, axis=1))
    o_ref[...] = jnp.concatenate(rows, axis=0) + b_ref[...]


def _mod(c, w, bias):
    bsz, d = c.shape
    n_out = w.shape[1]
    tn = 1024
    cb = jnp.broadcast_to(c[:, :, None], (bsz, d, LANES))
    return pl.pallas_call(
        _mod_kernel,
        grid=(n_out // tn,),
        in_specs=[pl.BlockSpec((bsz, d, LANES), lambda j: (0, 0, 0)),
                  pl.BlockSpec((d, tn), lambda j: (0, j)),
                  pl.BlockSpec((1, tn), lambda j: (0, j))],
        out_specs=pl.BlockSpec((bsz, tn), lambda j: (0, j)),
        out_shape=jax.ShapeDtypeStruct((bsz, n_out), F32),
        compiler_params=_cparams(("arbitrary",)),
        name="mod",
    )(cb, w, bias.reshape(1, n_out))


def _cast_kernel(x_ref, o_ref):
    o_ref[...] = x_ref[...].astype(o_ref.dtype)


def _cast_bf16(w):
    r, c = w.shape
    tr = 512
    return pl.pallas_call(
        _cast_kernel,
        grid=(r // tr,),
        in_specs=[pl.BlockSpec((tr, c), lambda i: (i, 0))],
        out_specs=pl.BlockSpec((tr, c), lambda i: (i, 0)),
        out_shape=jax.ShapeDtypeStruct((r, c), BF16),
        compiler_params=_cparams(("arbitrary",)),
        name="cast_bf16",
    )(w)


def _rmsnorm_mod(x, g, scale, shift):
    r = lax.rsqrt(jnp.mean(x * x, axis=-1, keepdims=True) + RMS_EPS)
    return (x * r) * g * (1.0 + scale) + shift


def _inproj_kernel(nj_main, jq, x_ref, mod_ref, g_ref, w_ref, wf_ref,
                   om_ref, og_ref, of_ref, h_ref, wfp_ref):
    j = pl.program_id(1)

    @pl.when(j == 0)
    def _():
        wfp_ref[...] = jnp.zeros(wfp_ref.shape, BF16)
        wfp_ref[0:wf_ref.shape[0], :] = wf_ref[...].astype(BF16)
        rows = 256

        def body(rb, carry):
            r0 = pl.multiple_of(rb * rows, rows)
            h = _rmsnorm_mod(x_ref[pl.ds(r0, rows), :], g_ref[...], mod_ref[0, 1:2, :],
                             mod_ref[0, 0:1, :])
            hb = h.astype(BF16)
            h_ref[pl.ds(r0, rows), :] = hb
            of_ref[pl.ds(r0, rows), :] = _dot_nt(hb, wfp_ref[...])
            return carry

        lax.fori_loop(0, x_ref.shape[0] // rows, body, 0)

    @pl.when(j < nj_main)
    def _():
        scale = jnp.where(jnp.logical_and(j >= jq[0], j < jq[1]), QK_SCALE, 1.0)
        om_ref[...] = (_dot_nt(h_ref[...], w_ref[...].astype(BF16)) * scale).astype(om_ref.dtype)

    @pl.when(j >= nj_main)
    def _():
        og_ref[...] = _dot_nt(h_ref[...], w_ref[...].astype(BF16)).astype(og_ref.dtype)


def _inproj(x2, mod3, g, w_in_t, n_main, n_f, n_gates, seq, q_cols):
    n, d = x2.shape
    tm, tn = 1024, 1024
    nj_main = n_main // tn
    assert n_main % tn == 0 and n_gates % tn == 0 and q_cols[0] % tn == 0 and q_cols[1] % tn == 0
    jq = (q_cols[0] // tn, q_cols[1] // tn)
    nj_g = n_gates // tn
    gate_row0 = n_main + n_f
    tiles_per_seq = seq // tm
    return pl.pallas_call(
        functools.partial(_inproj_kernel, nj_main, jq),
        grid=(n // tm, nj_main + nj_g),
        in_specs=[pl.BlockSpec((tm, d), lambda i, j: (i, 0)),
                  pl.BlockSpec((1, 6, d), lambda i, j: (i // tiles_per_seq, 0, 0)),
                  pl.BlockSpec((1, d), lambda i, j: (0, 0)),
                  pl.BlockSpec((pl.Element(tn), pl.Element(d)),
                               lambda i, j: (pl.multiple_of(
                                   jnp.where(j < nj_main, tn * j, gate_row0 + tn * (j - nj_main)), n_f), 0)),
                  pl.BlockSpec((n_f, d), lambda i, j: (n_main // n_f, 0))],
        out_specs=[pl.BlockSpec((tm, tn), lambda i, j: (i, jnp.minimum(j, nj_main - 1))),
                   pl.BlockSpec((tm, tn), lambda i, j: (i, jnp.maximum(j - nj_main, 0))),
                   pl.BlockSpec((tm, LANES), lambda i, j: (i, 0))],
        out_shape=[jax.ShapeDtypeStruct((n, n_main), BF16),
                   jax.ShapeDtypeStruct((n, n_gates), BF16),
                   jax.ShapeDtypeStruct((n, LANES), F32)],
        scratch_shapes=[pltpu.VMEM((tm, d), BF16), pltpu.VMEM((LANES, d), BF16)],
        compiler_params=_cparams(("arbitrary", "arbitrary")),
        name="inproj",
    )(x2, mod3, g.reshape(1, d), w_in_t, w_in_t)


def _cumf_kernel(blocks_per_seq, f_ref, bf_ref, fc_ref, carry_ref):
    i = pl.program_id(0)

    @pl.when(i % blocks_per_seq == 0)
    def _():
        carry_ref[...] = jnp.zeros_like(carry_ref)

    z = f_ref[...] + bf_ref[...]
    lf = -(jnp.maximum(-z, 0.0) + jnp.log1p(jnp.exp(-jnp.abs(z))))
    t = z.shape[0]
    row = lax.broadcasted_iota(jnp.int32, (t, t), 0)
    col = lax.broadcasted_iota(jnp.int32, (t, t), 1)
    tri = jnp.where(col <= row, 1.0, 0.0).astype(BF16)
    parts = [jnp.dot(tri, p, preferred_element_type=F32) for p in _split3(lf)]
    fb = (parts[0] + (parts[1] + parts[2])) + carry_ref[...]
    fc_ref[...] = fb
    carry_ref[...] = fb[t - 1:t, :]


def _cumf(f_logit, b_forget, seq):
    n = f_logit.shape[0]
    t = 512
    bf = jnp.pad(b_forget, (0, LANES - b_forget.shape[0])).reshape(1, LANES)
    return pl.pallas_call(
        functools.partial(_cumf_kernel, seq // t),
        grid=(n // t,),
        in_specs=[pl.BlockSpec((t, LANES), lambda i: (i, 0)),
                  pl.BlockSpec((1, LANES), lambda i: (0, 0))],
        out_specs=pl.BlockSpec((t, LANES), lambda i: (i, 0)),
        out_shape=jax.ShapeDtypeStruct((n, LANES), F32),
        scratch_shapes=[pltpu.VMEM((1, LANES), F32)],
        compiler_params=_cparams(("arbitrary",)),
        name="cumf",
    )(f_logit, bf)


def _dot_nt(a, b):
    return lax.dot_general(a, b, (((1,), (1,)), ((), ())), preferred_element_type=F32)


def _split3(x):
    hi = x.astype(BF16)
    r1 = x - hi.astype(F32)
    mid = r1.astype(BF16)
    lo = (r1 - mid.astype(F32)).astype(BF16)
    return hi, mid, lo


def _bias_lanes(f, head, lane0, sign):
    r = lax.broadcasted_iota(jnp.int32, (LANES, LANES), 0)
    c = lax.broadcasted_iota(jnp.int32, (LANES, LANES), 1)
    out = None
    for j, part in enumerate(_split3(f)):
        sel = jnp.where(jnp.logical_and(r == head, c == lane0 + j), sign, 0.0).astype(BF16)
        term = jnp.dot(part, sel, preferred_element_type=F32)
        out = term if out is None else out + term
    return out


def _attn_kernel(tq, tk, q_ref, k_ref, v_ref, fc_ref, o_ref, kp_ref, qp_ref, vt_ref, m_ref, l_ref,
                 acc_ref, sa_ref, sb_ref):
    hp = pl.program_id(1)
    qi = pl.program_id(2)
    seq = k_ref.shape[0]
    lane = lax.broadcasted_iota(jnp.int32, (1, LANES), 1)
    own = (lane < HEAD_DIM, lane >= HEAD_DIM)
    spare = (HEAD_DIM, 0)
    ones_k = tuple(jnp.where(jnp.logical_and(lane >= spare[h] + 3, lane < spare[h] + 6), 1.0, 0.0)
                   for h in range(2))
    ones_q = tuple(jnp.where(jnp.logical_and(lane >= spare[h], lane < spare[h] + 3), 1.0, 0.0)
                   for h in range(2))

    @pl.when(qi == 0)
    def _():
        def prep(ci, carry):
            r0 = pl.multiple_of(ci * tk, tk)
            f = fc_ref[pl.ds(r0, tk), :] * LOG2E
            k = k_ref[pl.ds(r0, tk), :]
            q = q_ref[pl.ds(r0, tk), :]
            for h in range(2):
                aux = _bias_lanes(f, 2 * hp + h, spare[h], -1.0) + ones_k[h]
                kp_ref[h, pl.ds(r0, tk), :] = jnp.where(own[h], k, aux.astype(BF16))
                aux = _bias_lanes(f, 2 * hp + h, spare[h] + 3, 1.0) + ones_q[h]
                qp_ref[h, pl.ds(r0, tk), :] = jnp.where(own[h], q, aux.astype(BF16))
            vt_ref[:, pl.ds(r0, tk)] = v_ref[pl.ds(r0, tk), :].astype(F32).T.astype(BF16)
            return carry
        lax.fori_loop(0, seq // tk, prep, 0)

    q0 = pl.multiple_of(qi * tq, tq)
    qp = [qp_ref[h, pl.ds(q0, tq), :] for h in range(2)]

    m_ref[...] = jnp.full(m_ref.shape, -jnp.inf, F32)
    l_ref[...] = jnp.zeros(l_ref.shape, F32)
    acc_ref[...] = jnp.zeros(acc_ref.shape, F32)

    def scores(kt, s_ref):
        k0 = pl.multiple_of(kt * tk, tk)
        for h in range(2):
            s_ref[h] = _dot_nt(kp_ref[h, pl.ds(k0, tk), :], qp[h])

    def update(kt, s_ref, masked):
        k0 = pl.multiple_of(kt * tk, tk)
        for h in range(2):
            s = s_ref[h]
            if masked:
                kr = lax.broadcasted_iota(jnp.int32, (tk, tq), 0)
                qc = lax.broadcasted_iota(jnp.int32, (tk, tq), 1)
                s = jnp.where(kr <= qc, s, -jnp.inf)
            m_old = m_ref[h]
            m_new = jnp.maximum(m_old, jnp.max(s, axis=0, keepdims=True))
            alpha = jnp.exp2(m_old - m_new)
            p = jnp.exp2(s - m_new)
            l_ref[h] = l_ref[h] * alpha + jnp.sum(p, axis=0, keepdims=True)
            m_ref[h] = m_new
            rows = pl.ds(h * HEAD_DIM, HEAD_DIM)
            pv = jnp.dot(vt_ref[rows, pl.ds(k0, tk)], p.astype(BF16), preferred_element_type=F32)
            acc_ref[rows, :] = acc_ref[rows, :] * alpha + pv

    scores(0, sa_ref)

    def pair(j, carry):
        scores(2 * j + 1, sb_ref)
        update(2 * j, sa_ref, False)
        scores(2 * j + 2, sa_ref)
        update(2 * j + 1, sb_ref, False)
        return carry

    lax.fori_loop(0, qi // 2, pair, 0)

    @pl.when(qi % 2 == 0)
    def _():
        update(qi, sa_ref, True)

    @pl.when(qi % 2 == 1)
    def _():
        scores(qi, sb_ref)
        update(qi - 1, sa_ref, False)
        update(qi, sb_ref, True)

    out_t = jnp.concatenate([acc_ref[pl.ds(h * HEAD_DIM, HEAD_DIM), :] / l_ref[h] for h in range(2)],
                            axis=0)
    o_ref[...] = out_t.T.astype(o_ref.dtype)


def _attn(proj, fcol, bsz, seq, q_col0, k_col0, v_col0):
    tq = tk = 512
    nq = seq // tq
    n_pairs = N_HEADS * HEAD_DIM // LANES
    qb, kb, vb = q_col0 // LANES, k_col0 // LANES, v_col0 // LANES
    return pl.pallas_call(
        functools.partial(_attn_kernel, tq, tk),
        grid=(bsz, n_pairs, nq),
        in_specs=[pl.BlockSpec((seq, LANES), lambda b, hp, qi: (b, qb + hp)),
                  pl.BlockSpec((seq, LANES), lambda b, hp, qi: (b, kb + hp)),
                  pl.BlockSpec((seq, LANES), lambda b, hp, qi: (b, vb + hp)),
                  pl.BlockSpec((seq, LANES), lambda b, hp, qi: (b, 0))],
        out_specs=pl.BlockSpec((tq, LANES), lambda b, hp, qi: (b * nq + qi, hp)),
        out_shape=jax.ShapeDtypeStruct((bsz * seq, N_HEADS * HEAD_DIM), BF16),
        scratch_shapes=[pltpu.VMEM((2, seq, LANES), BF16), pltpu.VMEM((2, seq, LANES), BF16),
                        pltpu.VMEM((LANES, seq), BF16),
                        pltpu.VMEM((2, 1, tq), F32), pltpu.VMEM((2, 1, tq), F32),
                        pltpu.VMEM((LANES, tq), F32),
                        pltpu.VMEM((2, tk, tq), F32), pltpu.VMEM((2, tk, tq), F32)],
        compiler_params=_cparams(("arbitrary", "arbitrary", "arbitrary")),
        name="attn",
    )(proj, proj, proj, fcol)


def _mix_kernel(tiles_per_seq, xc_ref, cb_ref, cc_ref, hxc_ref, hcc_ref, att_ref, gc_ref, ga_ref,
                x_ref, mod_ref, g2_ref, cw_ref, wc_ref, wa_ref, wo_ref, rw_ref, rb_ref,
                x1_ref, h2_ref, lg_ref):
    i = pl.program_id(0)
    u = cc_ref[...].astype(F32) * xc_ref[...].astype(F32)
    hu = hcc_ref[...].astype(F32) * hxc_ref[...].astype(F32)
    hu = jnp.where(i % tiles_per_seq == 0, 0.0, hu)
    nh = hu.shape[0]
    row = lax.broadcasted_iota(jnp.int32, u.shape, 0)
    u1 = jnp.where(row == 0, hu[nh - 1:nh, :], pltpu.roll(u, 1, axis=0))
    u2 = jnp.where(row == 0, hu[nh - 2:nh - 1, :],
                   jnp.where(row == 1, hu[nh - 1:nh, :], pltpu.roll(u, 2, axis=0)))
    conv = cw_ref[0:1, :] * u2 + cw_ref[1:2, :] * u1 + cw_ref[2:3, :] * u
    z = (cb_ref[...].astype(F32) * conv).astype(BF16)
    y_conv = jnp.dot(z, wc_ref[...], preferred_element_type=F32)
    y_attn = jnp.dot(att_ref[...], wa_ref[...], preferred_element_type=F32)
    merged = (jax.nn.sigmoid(gc_ref[...].astype(F32)) * y_conv
              + jax.nn.sigmoid(ga_ref[...].astype(F32)) * y_attn)
    o = jnp.dot(merged.astype(BF16), wo_ref[...], preferred_element_type=F32)
    x1 = x_ref[...] + mod_ref[0, 2:3, :] * o
    x1_ref[...] = x1
    h2 = _rmsnorm_mod(x1, g2_ref[...], mod_ref[0, 4:5, :], mod_ref[0, 3:4, :])
    h2_ref[...] = h2
    h_hi, h_lo, _ = _split3(h2)
    w_hi, w_lo, _ = _split3(rw_ref[...])
    lg_ref[...] = (jnp.dot(h_hi, w_hi, preferred_element_type=F32)
                   + (jnp.dot(h_hi, w_lo, preferred_element_type=F32)
                      + jnp.dot(h_lo, w_hi, preferred_element_type=F32))) + rb_ref[...]


def _mix(proj, gates, att, x2, mod3, g2, conv_w, wc, wa, wo, rw, rb, seq, d_conv):
    n, d = x2.shape
    tm = 256
    halo = 16
    tiles_per_seq = seq // tm
    d_attn = att.shape[1]

    def resident(shape):
        return pl.BlockSpec(shape, lambda i: (0,) * len(shape), pipeline_mode=pl.Buffered(1))

    return pl.pallas_call(
        functools.partial(_mix_kernel, tiles_per_seq),
        grid=(n // tm,),
        in_specs=[pl.BlockSpec((tm, d_conv), lambda i: (i, 0)),
                  pl.BlockSpec((tm, d_conv), lambda i: (i, 1)),
                  pl.BlockSpec((tm, d_conv), lambda i: (i, 2)),
                  pl.BlockSpec((halo, d_conv), lambda i: (jnp.maximum(i * (tm // halo) - 1, 0), 0)),
                  pl.BlockSpec((halo, d_conv), lambda i: (jnp.maximum(i * (tm // halo) - 1, 0), 2)),
                  pl.BlockSpec((tm, d_attn), lambda i: (i, 0)),
                  pl.BlockSpec((tm, d), lambda i: (i, 0)),
                  pl.BlockSpec((tm, d), lambda i: (i, 1)),
                  pl.BlockSpec((tm, d), lambda i: (i, 0)),
                  pl.BlockSpec((1, 6, d), lambda i: (i // tiles_per_seq, 0, 0)),
                  resident((1, d)),
                  resident((CONV_WIDTH, d_conv)),
                  resident((d_conv, d)),
                  resident((d_attn, d)),
                  resident((d, d)),
                  resident((d, LANES)),
                  resident((1, LANES))],
        out_specs=[pl.BlockSpec((tm, d), lambda i: (i, 0)),
                   pl.BlockSpec((tm, d), lambda i: (i, 0)),
                   pl.BlockSpec((tm, LANES), lambda i: (i, 0))],
        out_shape=[jax.ShapeDtypeStruct((n, d), F32),
                   jax.ShapeDtypeStruct((n, d), F32),
                   jax.ShapeDtypeStruct((n, LANES), F32)],
        compiler_params=_cparams(("arbitrary",)),
        name="mix",
    )(proj, proj, proj, proj, proj, att, gates, gates, x2, mod3, g2.reshape(1, d), conv_w,
      wc, wa, wo, rw, rb)


def _route_kernel(lg_ref, idx_ref, prob_ref, cnt_ref, carry_ref):
    i = pl.program_id(0)

    @pl.when(i == 0)
    def _():
        carry_ref[...] = jnp.zeros_like(carry_ref)

    tt = lg_ref.shape[0]
    lane = lax.broadcasted_iota(jnp.int32, (tt, LANES), 1)
    lane_f = lane.astype(F32)
    l = jnp.where(lane < N_EXPERTS, lg_ref[...], -jnp.inf)
    onehot = jnp.zeros((tt, LANES), F32)
    vals, ids = [], []
    for _ in range(TOP_K):
        m = jnp.max(l, axis=1, keepdims=True)
        idx = jnp.min(jnp.where(l == m, lane_f, float(LANES)), axis=1, keepdims=True)
        sel = lane_f == idx
        vals.append(m)
        ids.append(idx)
        onehot = jnp.where(sel, 1.0, onehot)
        l = jnp.where(sel, -jnp.inf, l)
    es = [jnp.exp(v - vals[0]) for v in vals]
    denom = es[0] + es[1] + es[2] + es[3]
    r = lax.broadcasted_iota(jnp.int32, (tt, tt), 0)
    c = lax.broadcasted_iota(jnp.int32, (tt, tt), 1)
    before = jnp.where(c < r, 1.0, 0.0).astype(BF16)
    cnt_before = jnp.dot(before, onehot.astype(BF16), preferred_element_type=F32) + carry_ref[...]
    idx_out = jnp.zeros((tt, LANES), jnp.int32)
    prob_out = jnp.zeros((tt, LANES), F32)
    for k in range(TOP_K):
        rank = jnp.sum(jnp.where(lane_f == ids[k], cnt_before, 0.0), axis=1, keepdims=True)
        idx_out = jnp.where(lane == k, ids[k].astype(jnp.int32), idx_out)
        idx_out = jnp.where(lane == TOP_K + k, rank.astype(jnp.int32), idx_out)
        prob_out = jnp.where(lane == k, es[k] / denom, prob_out)
    idx_ref[...] = idx_out
    prob_ref[...] = prob_out
    carry_ref[...] = carry_ref[...] + jnp.sum(onehot, axis=0, keepdims=True)
    cnt_ref[...] = carry_ref[...].astype(jnp.int32)


def _route(logits):
    n = logits.shape[0]
    tt = 512
    return pl.pallas_call(
        _route_kernel,
        grid=(n // tt,),
        in_specs=[pl.BlockSpec((tt, LANES), lambda i: (i, 0))],
        out_specs=[pl.BlockSpec((tt, LANES), lambda i: (i, 0)),
                   pl.BlockSpec((tt, LANES), lambda i: (i, 0)),
                   pl.BlockSpec((1, LANES), lambda i: (0, 0))],
        out_shape=[jax.ShapeDtypeStruct((n, LANES), jnp.int32),
                   jax.ShapeDtypeStruct((n, LANES), F32),
                   jax.ShapeDtypeStruct((1, LANES), jnp.int32)],
        scratch_shapes=[pltpu.VMEM((1, LANES), F32)],
        compiler_params=_cparams(("arbitrary",)),
        name="route",
    )(logits)


def _zero_fill_rows(zero_rows, dst_hbm, row0, n_rows, sem, start):
    off = row0
    size = 1 << (zero_rows.shape[0].bit_length() - 1)
    while size >= ROW_ALIGN:
        @pl.when((n_rows & size) != 0)
        def _(off=off, size=size):
            cp = pltpu.make_async_copy(zero_rows.at[pl.ds(0, size), :],
                                       dst_hbm.at[pl.ds(pl.multiple_of(off, ROW_ALIGN), size), :], sem)
            if start:
                cp.start()
            else:
                cp.wait()
        off = off + (n_rows & size)
        size //= 2


def _dispatch_kernel(gs_ref, cnt_ref, gend_ref, dest_ref, h_ref, xs_hbm, zbuf, sem):
    j = pl.program_id(0)
    tt = h_ref.shape[0]
    n_total = xs_hbm.shape[0]

    @pl.when(j == 0)
    def _():
        zbuf[...] = jnp.zeros(zbuf.shape, F32)
        tail0 = gend_ref[0]

        def pad_rows(start):
            def group(e, carry):
                cnt = cnt_ref[e]

                def row(r, c2):
                    cp = pltpu.make_async_copy(zbuf.at[pl.ds(0, 1), :],
                                               xs_hbm.at[pl.ds(gs_ref[e] + r, 1), :], sem.at[2])
                    if start:
                        cp.start()
                    else:
                        cp.wait()
                    return c2
                lax.fori_loop(cnt, (cnt + ROW_ALIGN - 1) & (-ROW_ALIGN), row, 0)
                return carry
            lax.fori_loop(0, N_EXPERTS, group, 0)

        for start in (True, False):
            _zero_fill_rows(zbuf, xs_hbm, tail0, n_total - tail0, sem.at[2], start)
            pad_rows(start)

    def tok(r, carry):
        for k in range(TOP_K):
            d = dest_ref[0, 0, r * TOP_K + k]
            pltpu.make_async_copy(h_ref.at[pl.ds(r, 1), :], xs_hbm.at[pl.ds(d, 1), :],
                                  sem.at[0]).start(priority=k % 2)
        return carry
    lax.fori_loop(0, tt, tok, 0, unroll=True)

    for k in range(TOP_K):
        pltpu.make_async_copy(h_ref, xs_hbm.at[pl.ds(0, tt), :], sem.at[0]).wait()


def _dispatch(h2, dest, g_start, cnt, g_end, n_rows_total):
    n, d = h2.shape
    tt = DISP_TOKENS
    dest3 = dest.reshape(n // tt, 1, tt * TOP_K)
    grid_spec = pltpu.PrefetchScalarGridSpec(
        num_scalar_prefetch=3,
        grid=(n // tt,),
        in_specs=[pl.BlockSpec((1, 1, tt * TOP_K), lambda j, gs, ct, ge: (j, 0, 0),
                               memory_space=pltpu.SMEM),
                  pl.BlockSpec((tt, d), lambda j, gs, ct, ge: (j, 0))],
        out_specs=pl.BlockSpec(memory_space=pl.ANY),
        scratch_shapes=[pltpu.VMEM((ZERO_ROWS, d), F32), pltpu.SemaphoreType.DMA((3,))],
    )
    return pl.pallas_call(
        _dispatch_kernel,
        grid_spec=grid_spec,
        out_shape=jax.ShapeDtypeStruct((n_rows_total, d), F32),
        compiler_params=_cparams(("arbitrary",)),
        name="dispatch",
    )(g_start, cnt, g_end, dest3, h2)


def _for_blocks(n_blk, n_max, pre, body):
    for n in range(4, n_max + 1):
        @pl.when(n_blk == n)
        def _(n=n):
            pre()
            for rb in range(n):
                body(rb)

    @pl.when(n_blk < 4)
    def _():
        pre()

        @pl.when(n_blk >= 2)
        def _():
            body(0)
            body(1)

        @pl.when(n_blk % 2 == 1)
        def _():
            body(n_blk - 1)


def _expert_kernel(n_act, n_down, sbe_ref, sbr_ref, sbn_ref, ns_ref, gend_ref,
                   xs_hbm, wg_ref, wu_ref, wd_ref, bg_ref, bu_ref, bd_ref, ys_hbm,
                   stage, xbf, act, out, wgu_ref, wdb_ref, ld_sem, st_sem):
    s = pl.program_id(0)
    c = pl.program_id(1)
    th = EXP_TH
    n_super = ns_ref[0]
    n_blocks = SB_ROWS // EXP_TM

    def slot_ref(slot):
        return stage.at[pl.ds(slot * EXP_TM, EXP_TM), :]

    def x_copy(sb, rb, slot):
        r0 = pl.multiple_of(sbr_ref[sb] + rb * EXP_TM, ROW_ALIGN)
        return pltpu.make_async_copy(xs_hbm.at[pl.ds(r0, EXP_TM), :], slot_ref(slot), ld_sem.at[slot])

    def y_copy(sb, rb):
        r0 = pl.multiple_of(sbr_ref[sb] + rb * EXP_TM, ROW_ALIGN)
        return pltpu.make_async_copy(out.at[pl.ds(rb * EXP_TM, EXP_TM), :],
                                     ys_hbm.at[pl.ds(r0, EXP_TM), :], st_sem.at[rb])

    @pl.when(jnp.logical_and(s == 0, c == 0))
    def _():
        stage[...] = jnp.zeros(stage.shape, F32)
        tail0 = gend_ref[0]
        for start in (True, False):
            _zero_fill_rows(stage, ys_hbm, tail0, ys_hbm.shape[0] - tail0, st_sem.at[0], start)
        x_copy(0, 0, 0).start()

    @pl.when(s < n_super)
    def _():
        n_blk = sbn_ref[s]
        n_blk_prev = sbn_ref[jnp.maximum(s - 1, 0)]
        expert = sbe_ref[s]

        @pl.when(c == 0)
        def _():
            for rb in range(n_blocks):
                @pl.when(rb < n_blk)
                def _(rb=rb):
                    if rb + 1 < n_blocks:
                        @pl.when(rb + 1 < n_blk)
                        def _():
                            x_copy(s, rb + 1, (rb + 1) % 2).start()
                    x_copy(s, rb, rb % 2).wait()
                    xbf[pl.ds(rb * EXP_TM, EXP_TM), :] = slot_ref(rb % 2)[...].astype(BF16)

        @pl.when(c < n_act)
        def _():
            cols = pl.ds(pl.multiple_of(c * th, th), th)
            bg = bg_ref[pl.ds(expert, 1), cols]
            bu = bu_ref[pl.ds(expert, 1), cols]

            def cast_weights():
                wgu_ref[:, :th] = wg_ref[0].astype(BF16)
                wgu_ref[:, th:] = wu_ref[0].astype(BF16)

            def act_block(rb):
                rows = pl.ds(pl.multiple_of(rb * EXP_TM, EXP_TM), EXP_TM)
                gu = jnp.dot(xbf[rows, :], wgu_ref[...], preferred_element_type=F32)
                g = jnp.minimum(gu[:, :th] + bg, SWIGLU_LIMIT)
                u = jnp.clip(gu[:, th:] + bu, -SWIGLU_LIMIT, SWIGLU_LIMIT)
                a = (u + 1.0) * g * jax.nn.sigmoid(SWIGLU_ALPHA * g)
                act[rows, cols] = a.astype(BF16)

            _for_blocks(n_blk, n_blocks, cast_weights, act_block)

        @pl.when(c >= n_act)
        def _():
            cols = pl.ds(pl.multiple_of((c - n_act) * EXP_TN, EXP_TN), EXP_TN)
            bd = bd_ref[pl.ds(expert, 1), cols]
            for rb in range(n_blocks):
                @pl.when(jnp.logical_and(c == n_act, jnp.logical_and(s > 0, rb < n_blk_prev)))
                def _(rb=rb):
                    y_copy(s - 1, rb).wait()

            def cast_weights():
                wdb_ref[...] = wd_ref[0].astype(BF16)

            def down_block(rb):
                rows = pl.ds(pl.multiple_of(rb * EXP_TM, EXP_TM), EXP_TM)
                out[rows, cols] = jnp.dot(act[rows, :], wdb_ref[...],
                                          preferred_element_type=F32) + bd

            _for_blocks(n_blk, n_blocks, cast_weights, down_block)

            @pl.when(c == n_act + n_down - 1)
            def _():
                for rb in range(n_blocks):
                    @pl.when(rb < n_blk)
                    def _(rb=rb):
                        y_copy(s, rb).start()

                @pl.when(s + 1 < n_super)
                def _():
                    x_copy(s + 1, 0, 0).start()

                @pl.when(s + 1 == n_super)
                def _():
                    for rb in range(n_blocks):
                        @pl.when(rb < n_blk)
                        def _(rb=rb):
                            y_copy(s, rb).wait()


def _expert(xs, sb_expert, sb_row0, sb_nblk, n_super, g_end, wg, wu, wd, bg, bu, bd):
    n_rows_total, d = xs.shape
    s_max = sb_expert.shape[0]
    d_exp = wg.shape[2]
    assert d_exp % EXP_TH == 0 and d % EXP_TN == 0 and ZERO_ROWS == 2 * EXP_TM
    n_act = d_exp // EXP_TH
    n_down = d // EXP_TN
    ne = wg.shape[0]

    def e_eff(s, sbe, ns):
        return sbe[jnp.minimum(s, ns[0] - 1)]

    def c_act(s, c, ns):
        return jnp.where(s < ns[0], jnp.minimum(c, n_act - 1), n_act - 1)

    def c_down(s, c, ns):
        return jnp.where(s < ns[0], jnp.maximum(c - n_act, 0), n_down - 1)

    act_w = pl.BlockSpec((1, d, EXP_TH),
                         lambda s, c, sbe, sbr, sbn, ns, ge: (e_eff(s, sbe, ns), 0, c_act(s, c, ns)))
    bias = pl.BlockSpec((ne, d_exp), lambda s, c, sbe, sbr, sbn, ns, ge: (0, 0))
    grid_spec = pltpu.PrefetchScalarGridSpec(
        num_scalar_prefetch=5,
        grid=(s_max, n_act + n_down),
        in_specs=[
            pl.BlockSpec(memory_space=pl.ANY),
            act_w, act_w,
            pl.BlockSpec((1, d_exp, EXP_TN),
                         lambda s, c, sbe, sbr, sbn, ns, ge: (e_eff(s, sbe, ns), 0, c_down(s, c, ns))),
            bias, bias,
            pl.BlockSpec((ne, d), lambda s, c, sbe, sbr, sbn, ns, ge: (0, 0)),
        ],
        out_specs=pl.BlockSpec(memory_space=pl.ANY),
        scratch_shapes=[pltpu.VMEM((ZERO_ROWS, d), F32), pltpu.VMEM((SB_ROWS, d), BF16),
                        pltpu.VMEM((SB_ROWS, d_exp), BF16), pltpu.VMEM((SB_ROWS, d), F32),
                        pltpu.VMEM((d, 2 * EXP_TH), BF16), pltpu.VMEM((d_exp, EXP_TN), BF16),
                        pltpu.SemaphoreType.DMA((2,)),
                        pltpu.SemaphoreType.DMA((SB_ROWS // EXP_TM,))],
    )
    return pl.pallas_call(
        functools.partial(_expert_kernel, n_act, n_down),
        grid_spec=grid_spec,
        out_shape=jax.ShapeDtypeStruct((n_rows_total, d), F32),
        compiler_params=_cparams(("arbitrary", "arbitrary")),
        name="expert",
    )(sb_expert, sb_row0, sb_nblk, n_super, g_end, xs, wg, wu, wd, bg, bu, bd)


def _combine_start(y_hbm, pos_ref, stage_slot, sem):
    tt = stage_slot.shape[1]

    def body(r, carry):
        for k in range(TOP_K):
            p = pos_ref[0, 0, r * TOP_K + k]
            pltpu.make_async_copy(y_hbm.at[pl.ds(p, 1), :], stage_slot.at[k, pl.ds(r, 1), :],
                                  sem).start(priority=k % 2)
        return carry
    lax.fori_loop(0, tt, body, 0, unroll=True)


def _combine_kernel(final_norm, pos_cur_ref, pos_nxt_ref, y_hbm, prob_ref, x1_ref, mod_ref, g_ref,
                    o_ref, stage, sem):
    j = pl.program_id(0)
    nj = pl.num_programs(0)
    tt = stage.shape[2]

    @pl.when(j == 0)
    def _():
        _combine_start(y_hbm, pos_cur_ref, stage.at[0], sem.at[0])

    @pl.when(j + 1 < nj)
    def _():
        nxt = (j + 1) % 2
        _combine_start(y_hbm, pos_nxt_ref, stage.at[nxt], sem.at[nxt])

    cur = j % 2
    for k in range(TOP_K):
        pltpu.make_async_copy(y_hbm.at[pl.ds(0, tt), :], stage.at[cur, k], sem.at[cur]).wait()
    prob = prob_ref[...]
    y = prob[:, 0:1] * stage[cur, 0]
    for k in range(1, TOP_K):
        y = y + prob[:, k:k + 1] * stage[cur, k]
    x2 = x1_ref[...] + mod_ref[0, 5:6, :] * y
    if final_norm:
        r = lax.rsqrt(jnp.mean(x2 * x2, axis=-1, keepdims=True) + RMS_EPS)
        x2 = (x2 * r) * g_ref[...]
    o_ref[...] = x2


def _combine(yb, pos, probs, x1, mod3, g, seq, final_norm):
    n, d = x1.shape
    tt = COMB_ROWS
    nt = n // tt
    tiles_per_seq = seq // tt
    pos3 = pos.reshape(nt, 1, tt * TOP_K)
    return pl.pallas_call(
        functools.partial(_combine_kernel, final_norm),
        grid=(nt,),
        in_specs=[pl.BlockSpec((1, 1, tt * TOP_K), lambda j: (j, 0, 0), memory_space=pltpu.SMEM),
                  pl.BlockSpec((1, 1, tt * TOP_K), lambda j: (jnp.minimum(j + 1, nt - 1), 0, 0),
                               memory_space=pltpu.SMEM),
                  pl.BlockSpec(memory_space=pl.ANY),
                  pl.BlockSpec((tt, LANES), lambda j: (j, 0)),
                  pl.BlockSpec((tt, d), lambda j: (j, 0)),
                  pl.BlockSpec((1, 6, d), lambda j: (j // tiles_per_seq, 0, 0)),
                  pl.BlockSpec((1, d), lambda j: (0, 0))],
        out_specs=pl.BlockSpec((tt, d), lambda j: (j, 0)),
        out_shape=jax.ShapeDtypeStruct((n, d), F32),
        scratch_shapes=[pltpu.VMEM((2, TOP_K, tt, d), F32), pltpu.SemaphoreType.DMA((2,))],
        compiler_params=_cparams(("arbitrary",)),
        name="combine",
    )(pos3, pos3, yb, probs, x1, mod3, g.reshape(1, d))


def _count_le(sorted_ends, q):
    return jnp.sum((sorted_ends[None, :] <= q[:, None]).astype(jnp.int32), axis=1)


def _plan(counts, n_tok):
    s_max = -(-n_tok * TOP_K // SB_ROWS) + N_EXPERTS
    cnt = counts[0, :N_EXPERTS]
    cnt_al = (cnt + ROW_ALIGN - 1) // ROW_ALIGN * ROW_ALIGN
    g_end = jnp.cumsum(cnt_al)
    g_start = g_end - cnt_al
    nsb_e = (cnt + SB_ROWS - 1) // SB_ROWS
    sb_end = jnp.cumsum(nsb_e)
    sb_start = sb_end - nsb_e
    n_super = sb_end[-1]
    s_ids = jnp.arange(s_max, dtype=jnp.int32)
    sb_expert = jnp.minimum(_count_le(sb_end, s_ids), N_EXPERTS - 1)
    j_in = s_ids - sb_start[sb_expert]
    live = s_ids < n_super
    sb_rows = jnp.where(live, jnp.clip(cnt[sb_expert] - j_in * SB_ROWS, 0, SB_ROWS), 0)
    sb_row0 = jnp.where(live, g_start[sb_expert] + j_in * SB_ROWS, 0)
    sb_nblk = (sb_rows + EXP_TM - 1) // EXP_TM
    i32 = lambda a: a.astype(jnp.int32)
    return (i32(g_start), i32(cnt), i32(g_end[-1:]), i32(sb_expert), i32(sb_row0), i32(sb_nblk),
            i32(n_super.reshape(1)))


def _dest_kernel(ir_ref, gs_ref, o_ref):
    ir = ir_ref[...].astype(F32)
    lane = lax.broadcasted_iota(jnp.int32, ir.shape, 1)
    lane_f = lane.astype(F32)
    out = jnp.zeros(ir.shape, F32)
    for k in range(TOP_K):
        e = jnp.sum(jnp.where(lane == k, ir, 0.0), axis=1, keepdims=True)
        r = jnp.sum(jnp.where(lane == TOP_K + k, ir, 0.0), axis=1, keepdims=True)
        g = jnp.sum(jnp.where(lane_f == e, gs_ref[...], 0.0), axis=1, keepdims=True)
        out = jnp.where(lane == k, g + r, out)
    o_ref[...] = out.astype(jnp.int32)


def _dest(idx_rank, g_start):
    n = idx_rank.shape[0]
    tt = 1024
    gs = jnp.pad(g_start.astype(F32), (0, LANES - N_EXPERTS)).reshape(1, LANES)
    return pl.pallas_call(
        _dest_kernel,
        grid=(n // tt,),
        in_specs=[pl.BlockSpec((tt, LANES), lambda i: (i, 0)),
                  pl.BlockSpec((1, LANES), lambda i: (0, 0))],
        out_specs=pl.BlockSpec((tt, LANES), lambda i: (i, 0)),
        out_shape=jax.ShapeDtypeStruct((n, LANES), jnp.int32),
        compiler_params=_cparams(("arbitrary",)),
        name="dest",
    )(idx_rank, gs)


def kernel(x, c, ada_w, ada_b, norm_mix_g, w_in, b_forget, conv_w, w_conv_out, w_attn_out, w_out,
           norm_ffn_g, router_w, router_b, exp_w_gate, exp_b_gate, exp_w_up, exp_b_up,
           exp_w_down, exp_b_down, final_norm_g):
    bsz, seq, d = x.shape
    n = bsz * seq
    d_conv = conv_w.shape[2]
    d_attn = w_attn_out.shape[1]
    n_main = 3 * d_conv + 3 * d_attn
    x2 = x.reshape(n, d)
    out = x2
    for l in range(ada_w.shape[0]):
        mod3 = _mod(c, ada_w[l], ada_b[l]).reshape(bsz, 6, d)
        proj, gates, f_logit = _inproj(out, mod3, norm_mix_g[l], w_in[l].T, n_main, N_HEADS,
                                       w_in.shape[2] - n_main - N_HEADS, seq,
                                       (3 * d_conv, 3 * d_conv + d_attn))
        fcol = _cumf(f_logit, b_forget[l], seq)
        att = _attn(proj, fcol, bsz, seq, 3 * d_conv, 3 * d_conv + d_attn, 3 * d_conv + 2 * d_attn)
        rw = jnp.pad(router_w[l], ((0, 0), (0, LANES - N_EXPERTS)))
        rb = jnp.pad(router_b[l], (0, LANES - N_EXPERTS)).reshape(1, LANES)
        x1, h2, logits = _mix(proj, gates, att, out, mod3, norm_ffn_g[l], conv_w[l],
                              _cast_bf16(w_conv_out[l]), _cast_bf16(w_attn_out[l]), _cast_bf16(w_out[l]),
                              rw, rb, seq, d_conv)
        idx_rank, probs, counts = _route(logits)
        g_start, cnt, g_end, sb_expert, sb_row0, sb_nblk, n_super = _plan(counts, n)
        dest = _dest(idx_rank, g_start)[:, :TOP_K].reshape(-1)
        n_rows_total = n * TOP_K + N_EXPERTS * ROW_ALIGN + EXP_TM
        xs = _dispatch(h2, dest, g_start, cnt, g_end, n_rows_total)
        ys = _expert(xs, sb_expert, sb_row0, sb_nblk, n_super, g_end, exp_w_gate[l], exp_w_up[l],
                     exp_w_down[l], exp_b_gate[l], exp_b_up[l], exp_b_down[l])
        out = _combine(ys, dest, probs, x1, mod3, final_norm_g, seq,
                       final_norm=(l == ada_w.shape[0] - 1))
    return out.reshape(bsz, seq, d)
```

```python
import functools

import jax
import jax.numpy as jnp
from jax import lax
from jax.experimental import pallas as pl
from jax.experimental.pallas import tpu as pltpu

F32 = jnp.float32
BF16 = jnp.bfloat16

N_HEADS = 16
HEAD_DIM = 64
N_EXPERTS = 32
TOP_K = 4
SWIGLU_LIMIT = 7.0
SWIGLU_ALPHA = 1.702
RMS_EPS = 1e-6
CONV_WIDTH = 3

LOG2E = 1.4426950408889634
QK_SCALE = HEAD_DIM ** -0.5 * LOG2E

LANES = 128
VMEM_LIMIT = 58 * 1024 * 1024

ROW_ALIGN = 8

EXP_TM = 272
SB_ROWS = 6 * EXP_TM
EXP_TH = 256
EXP_TN = 512
ZERO_ROWS = 2 * EXP_TM
DISP_TOKENS = 256
COMB_ROWS = 128


def _cparams(sem, vmem=VMEM_LIMIT):
    return pltpu.CompilerParams(dimension_semantics=sem, vmem_limit_bytes=vmem)


def _mod_kernel(cb_ref, w_ref, b_ref, o_ref):
    w = w_ref[...]
    tn = w.shape[1]
    rows = []
    for b in range(cb_ref.shape[0]):
        cv = cb_ref[b]
        ca = cv * jax.nn.sigmoid(cv)
        cols = [jnp.sum(w[:, j * LANES:(j + 1) * LANES] * ca, axis=0, keepdims=True)
                for j in range(tn // LANES)]
        rows.append(jnp.concatenate(cols, axis=1))
    o_ref[...] = jnp.concatenate(rows, axis=0) + b_ref[...]


def _mod(c, w, bias):
    bsz, d = c.shape
    n_out = w.shape[1]
    tn = 1024
    cb = jnp.broadcast_to(c[:, :, None], (bsz, d, LANES))
    return pl.pallas_call(
        _mod_kernel,
        grid=(n_out // tn,),
        in_specs=[pl.BlockSpec((bsz, d, LANES), lambda j: (0, 0, 0)),
                  pl.BlockSpec((d, tn), lambda j: (0, j)),
                  pl.BlockSpec((1, tn), lambda j: (0, j))],
        out_specs=pl.BlockSpec((bsz, tn), lambda j: (0, j)),
        out_shape=jax.ShapeDtypeStruct((bsz, n_out), F32),
        compiler_params=_cparams(("arbitrary",)),
        name="mod",
    )(cb, w, bias.reshape(1, n_out))


def _cast_kernel(x_ref, o_ref):
    o_ref[...] = x_ref[...].astype(o_ref.dtype)


def _cast_bf16(w):
    r, c = w.shape
    tr = 512
    return pl.pallas_call(
        _cast_kernel,
        grid=(r // tr,),
        in_specs=[pl.BlockSpec((tr, c), lambda i: (i, 0))],
        out_specs=pl.BlockSpec((tr, c), lambda i: (i, 0)),
        out_shape=jax.ShapeDtypeStruct((r, c), BF16),
        compiler_params=_cparams(("arbitrary",)),
        name="cast_bf16",
    )(w)


def _rmsnorm_mod(x, g, scale, shift):
    r = lax.rsqrt(jnp.mean(x * x, axis=-1, keepdims=True) + RMS_EPS)
    return (x * r) * g * (1.0 + scale) + shift


def _inproj_kernel(nj_main, jq, x_ref, mod_ref, g_ref, w_ref, wf_ref,
                   om_ref, og_ref, of_ref, h_ref, wfp_ref):
    j = pl.program_id(1)

    @pl.when(j == 0)
    def _():
        wfp_ref[...] = jnp.zeros(wfp_ref.shape, BF16)
        wfp_ref[0:wf_ref.shape[0], :] = wf_ref[...].astype(BF16)
        rows = 256

        def body(rb, carry):
            r0 = pl.multiple_of(rb * rows, rows)
            h = _rmsnorm_mod(x_ref[pl.ds(r0, rows), :], g_ref[...], mod_ref[0, 1:2, :],
                             mod_ref[0, 0:1, :])
            hb = h.astype(BF16)
            h_ref[pl.ds(r0, rows), :] = hb
            of_ref[pl.ds(r0, rows), :] = _dot_nt(hb, wfp_ref[...])
            return carry

        lax.fori_loop(0, x_ref.shape[0] // rows, body, 0)

    @pl.when(j < nj_main)
    def _():
        scale = jnp.where(jnp.logical_and(j >= jq[0], j < jq[1]), QK_SCALE, 1.0)
        om_ref[...] = (_dot_nt(h_ref[...], w_ref[...].astype(BF16)) * scale).astype(om_ref.dtype)

    @pl.when(j >= nj_main)
    def _():
        og_ref[...] = _dot_nt(h_ref[...], w_ref[...].astype(BF16)).astype(og_ref.dtype)


def _inproj(x2, mod3, g, w_in_t, n_main, n_f, n_gates, seq, q_cols):
    n, d = x2.shape
    tm, tn = 1024, 1024
    nj_main = n_main // tn
    assert n_main % tn == 0 and n_gates % tn == 0 and q_cols[0] % tn == 0 and q_cols[1] % tn == 0
    jq = (q_cols[0] // tn, q_cols[1] // tn)
    nj_g = n_gates // tn
    gate_row0 = n_main + n_f
    tiles_per_seq = seq // tm
    return pl.pallas_call(
        functools.partial(_inproj_kernel, nj_main, jq),
        grid=(n // tm, nj_main + nj_g),
        in_specs=[pl.BlockSpec((tm, d), lambda i, j: (i, 0)),
                  pl.BlockSpec((1, 6, d), lambda i, j: (i // tiles_per_seq, 0, 0)),
                  pl.BlockSpec((1, d), lambda i, j: (0, 0)),
                  pl.BlockSpec((pl.Element(tn), pl.Element(d)),
                               lambda i, j: (pl.multiple_of(
                                   jnp.where(j < nj_main, tn * j, gate_row0 + tn * (j - nj_main)), n_f), 0)),
                  pl.BlockSpec((n_f, d), lambda i, j: (n_main // n_f, 0))],
        out_specs=[pl.BlockSpec((tm, tn), lambda i, j: (i, jnp.minimum(j, nj_main - 1))),
                   pl.BlockSpec((tm, tn), lambda i, j: (i, jnp.maximum(j - nj_main, 0))),
                   pl.BlockSpec((tm, LANES), lambda i, j: (i, 0))],
        out_shape=[jax.ShapeDtypeStruct((n, n_main), BF16),
                   jax.ShapeDtypeStruct((n, n_gates), BF16),
                   jax.ShapeDtypeStruct((n, LANES), F32)],
        scratch_shapes=[pltpu.VMEM((tm, d), BF16), pltpu.VMEM((LANES, d), BF16)],
        compiler_params=_cparams(("arbitrary", "arbitrary")),
        name="inproj",
    )(x2, mod3, g.reshape(1, d), w_in_t, w_in_t)


def _cumf_kernel(blocks_per_seq, f_ref, bf_ref, fc_ref, carry_ref):
    i = pl.program_id(0)

    @pl.when(i % blocks_per_seq == 0)
    def _():
        carry_ref[...] = jnp.zeros_like(carry_ref)

    z = f_ref[...] + bf_ref[...]
    lf = -(jnp.maximum(-z, 0.0) + jnp.log1p(jnp.exp(-jnp.abs(z))))
    t = z.shape[0]
    row = lax.broadcasted_iota(jnp.int32, (t, t), 0)
    col = lax.broadcasted_iota(jnp.int32, (t, t), 1)
    tri = jnp.where(col <= row, 1.0, 0.0).astype(BF16)
    parts = [jnp.dot(tri, p, preferred_element_type=F32) for p in _split3(lf)]
    fb = (parts[0] + (parts[1] + parts[2])) + carry_ref[...]
    fc_ref[...] = fb
    carry_ref[...] = fb[t - 1:t, :]


def _cumf(f_logit, b_forget, seq):
    n = f_logit.shape[0]
    t = 512
    bf = jnp.pad(b_forget, (0, LANES - b_forget.shape[0])).reshape(1, LANES)
    return pl.pallas_call(
        functools.partial(_cumf_kernel, seq // t),
        grid=(n // t,),
        in_specs=[pl.BlockSpec((t, LANES), lambda i: (i, 0)),
                  pl.BlockSpec((1, LANES), lambda i: (0, 0))],
        out_specs=pl.BlockSpec((t, LANES), lambda i: (i, 0)),
        out_shape=jax.ShapeDtypeStruct((n, LANES), F32),
        scratch_shapes=[pltpu.VMEM((1, LANES), F32)],
        compiler_params=_cparams(("arbitrary",)),
        name="cumf",
    )(f_logit, bf)


def _dot_nt(a, b):
    return lax.dot_general(a, b, (((1,), (1,)), ((), ())), preferred_element_type=F32)


def _split3(x):
    hi = x.astype(BF16)
    r1 = x - hi.astype(F32)
    mid = r1.astype(BF16)
    lo = (r1 - mid.astype(F32)).astype(BF16)
    return hi, mid, lo


def _bias_lanes(parts, head, lane0):
    r = lax.broadcasted_iota(jnp.int32, (LANES, LANES), 0)
    c = lax.broadcasted_iota(jnp.int32, (LANES, LANES), 1)
    out = None
    for j, part in enumerate(parts):
        sel = jnp.where(r == head, jnp.where(c == lane0 + j, -1.0, jnp.where(c == lane0 + 3 + j, 1.0, 0.0)),
                        0.0).astype(BF16)
        term = jnp.dot(part, sel, preferred_element_type=F32)
        out = term if out is None else out + term
    return out


def _attn_kernel(tq, tk, q_ref, k_ref, v_ref, fc_ref, o_ref, kp_ref, qp_ref, vt_ref, m_ref, l_ref,
                 acc_ref, sa_ref, sb_ref):
    hp = pl.program_id(1)
    qi = pl.program_id(2)
    seq = k_ref.shape[0]
    lane = lax.broadcasted_iota(jnp.int32, (1, LANES), 1)
    own = (lane < HEAD_DIM, lane >= HEAD_DIM)
    spare = (HEAD_DIM, 0)
    ones_k = tuple(jnp.logical_and(lane >= spare[h] + 3, lane < spare[h] + 6) for h in range(2))
    ones_q = tuple(jnp.logical_and(lane >= spare[h], lane < spare[h] + 3) for h in range(2))

    @pl.when(qi == 0)
    def _():
        def prep(ci, carry):
            r0 = pl.multiple_of(ci * tk, tk)
            f = fc_ref[pl.ds(r0, tk), :] * LOG2E
            k = k_ref[pl.ds(r0, tk), :]
            q = q_ref[pl.ds(r0, tk), :]
            parts = _split3(f)
            for h in range(2):
                both = _bias_lanes(parts, 2 * hp + h, spare[h])
                kp_ref[h, pl.ds(r0, tk), :] = jnp.where(own[h], k, jnp.where(ones_k[h], 1.0, both).astype(BF16))
                qp_ref[h, pl.ds(r0, tk), :] = jnp.where(own[h], q, jnp.where(ones_q[h], 1.0, both).astype(BF16))
            vt_ref[:, pl.ds(r0, tk)] = v_ref[pl.ds(r0, tk), :].astype(F32).T.astype(BF16)
            return carry
        lax.fori_loop(0, seq // tk, prep, 0)

    q0 = pl.multiple_of(qi * tq, tq)
    qp = [qp_ref[h, pl.ds(q0, tq), :] for h in range(2)]

    m_ref[...] = jnp.full(m_ref.shape, -jnp.inf, F32)
    l_ref[...] = jnp.zeros(l_ref.shape, F32)
    acc_ref[...] = jnp.zeros(acc_ref.shape, F32)

    def scores(kt, s_ref):
        k0 = pl.multiple_of(kt * tk, tk)
        for h in range(2):
            s_ref[h] = _dot_nt(kp_ref[h, pl.ds(k0, tk), :], qp[h])

    def update(kt, s_ref, masked):
        k0 = pl.multiple_of(kt * tk, tk)
        for h in range(2):
            s = s_ref[h]
            if masked:
                kr = lax.broadcasted_iota(jnp.int32, (tk, tq), 0)
                qc = lax.broadcasted_iota(jnp.int32, (tk, tq), 1)
                s = jnp.where(kr <= qc, s, -jnp.inf)
            m_old = m_ref[h]
            m_new = jnp.maximum(m_old, jnp.max(s, axis=0, keepdims=True))
            alpha = jnp.exp2(m_old - m_new)
            p = jnp.exp2(s - m_new)
            l_ref[h] = l_ref[h] * alpha + jnp.sum(p, axis=0, keepdims=True)
            m_ref[h] = m_new
            rows = pl.ds(h * HEAD_DIM, HEAD_DIM)
            pv = jnp.dot(vt_ref[rows, pl.ds(k0, tk)], p.astype(BF16), preferred_element_type=F32)
            acc_ref[rows, :] = acc_ref[rows, :] * alpha + pv

    scores(0, sa_ref)

    def pair(j, carry):
        scores(2 * j + 1, sb_ref)
        update(2 * j, sa_ref, False)
        scores(2 * j + 2, sa_ref)
        update(2 * j + 1, sb_ref, False)
        return carry

    lax.fori_loop(0, qi // 2, pair, 0)

    @pl.when(qi % 2 == 0)
    def _():
        update(qi, sa_ref, True)

    @pl.when(qi % 2 == 1)
    def _():
        scores(qi, sb_ref)
        update(qi - 1, sa_ref, False)
        update(qi, sb_ref, True)

    out_t = jnp.concatenate([acc_ref[pl.ds(h * HEAD_DIM, HEAD_DIM), :] / l_ref[h] for h in range(2)],
                            axis=0)
    o_ref[...] = out_t.T.astype(o_ref.dtype)


def _attn(proj, fcol, bsz, seq, q_col0, k_col0, v_col0):
    tq = tk = 512
    nq = seq // tq
    n_pairs = N_HEADS * HEAD_DIM // LANES
    qb, kb, vb = q_col0 // LANES, k_col0 // LANES, v_col0 // LANES
    return pl.pallas_call(
        functools.partial(_attn_kernel, tq, tk),
        grid=(bsz, n_pairs, nq),
        in_specs=[pl.BlockSpec((seq, LANES), lambda b, hp, qi: (b, qb + hp)),
                  pl.BlockSpec((seq, LANES), lambda b, hp, qi: (b, kb + hp)),
                  pl.BlockSpec((seq, LANES), lambda b, hp, qi: (b, vb + hp)),
                  pl.BlockSpec((seq, LANES), lambda b, hp, qi: (b, 0))],
        out_specs=pl.BlockSpec((tq, LANES), lambda b, hp, qi: (b * nq + qi, hp)),
        out_shape=jax.ShapeDtypeStruct((bsz * seq, N_HEADS * HEAD_DIM), BF16),
        scratch_shapes=[pltpu.VMEM((2, seq, LANES), BF16), pltpu.VMEM((2, seq, LANES), BF16),
                        pltpu.VMEM((LANES, seq), BF16),
                        pltpu.VMEM((2, 1, tq), F32), pltpu.VMEM((2, 1, tq), F32),
                        pltpu.VMEM((LANES, tq), F32),
                        pltpu.VMEM((2, tk, tq), F32), pltpu.VMEM((2, tk, tq), F32)],
        compiler_params=_cparams(("arbitrary", "arbitrary", "arbitrary")),
        name="attn",
    )(proj, proj, proj, fcol)


def _mix_kernel(tiles_per_seq, xc_ref, cb_ref, cc_ref, hxc_ref, hcc_ref, att_ref, gc_ref, ga_ref,
                x_ref, mod_ref, g2_ref, cw_ref, wc_ref, wa_ref, wo_ref, rw_ref, rb_ref,
                x1_ref, h2_ref, lg_ref):
    i = pl.program_id(0)
    u = cc_ref[...].astype(F32) * xc_ref[...].astype(F32)
    hu = hcc_ref[...].astype(F32) * hxc_ref[...].astype(F32)
    hu = jnp.where(i % tiles_per_seq == 0, 0.0, hu)
    nh = hu.shape[0]
    row = lax.broadcasted_iota(jnp.int32, u.shape, 0)
    u1 = jnp.where(row == 0, hu[nh - 1:nh, :], pltpu.roll(u, 1, axis=0))
    u2 = jnp.where(row == 0, hu[nh - 2:nh - 1, :],
                   jnp.where(row == 1, hu[nh - 1:nh, :], pltpu.roll(u, 2, axis=0)))
    conv = cw_ref[0:1, :] * u2 + cw_ref[1:2, :] * u1 + cw_ref[2:3, :] * u
    z = (cb_ref[...].astype(F32) * conv).astype(BF16)
    y_conv = jnp.dot(z, wc_ref[...], preferred_element_type=F32)
    y_attn = jnp.dot(att_ref[...], wa_ref[...], preferred_element_type=F32)
    merged = (jax.nn.sigmoid(gc_ref[...].astype(F32)) * y_conv
              + jax.nn.sigmoid(ga_ref[...].astype(F32)) * y_attn)
    o = jnp.dot(merged.astype(BF16), wo_ref[...], preferred_element_type=F32)
    x1 = x_ref[...] + mod_ref[0, 2:3, :] * o
    x1_ref[...] = x1
    h2 = _rmsnorm_mod(x1, g2_ref[...], mod_ref[0, 4:5, :], mod_ref[0, 3:4, :])
    h2_ref[...] = h2
    h_hi, h_lo, _ = _split3(h2)
    w_hi, w_lo, _ = _split3(rw_ref[...])
    lg_ref[...] = (jnp.dot(h_hi, w_hi, preferred_element_type=F32)
                   + (jnp.dot(h_hi, w_lo, preferred_element_type=F32)
                      + jnp.dot(h_lo, w_hi, preferred_element_type=F32))) + rb_ref[...]


def _mix(proj, gates, att, x2, mod3, g2, conv_w, wc, wa, wo, rw, rb, seq, d_conv):
    n, d = x2.shape
    tm = 256
    halo = 16
    tiles_per_seq = seq // tm
    d_attn = att.shape[1]

    def resident(shape):
        return pl.BlockSpec(shape, lambda i: (0,) * len(shape), pipeline_mode=pl.Buffered(1))

    return pl.pallas_call(
        functools.partial(_mix_kernel, tiles_per_seq),
        grid=(n // tm,),
        in_specs=[pl.BlockSpec((tm, d_conv), lambda i: (i, 0)),
                  pl.BlockSpec((tm, d_conv), lambda i: (i, 1)),
                  pl.BlockSpec((tm, d_conv), lambda i: (i, 2)),
                  pl.BlockSpec((halo, d_conv), lambda i: (jnp.maximum(i * (tm // halo) - 1, 0), 0)),
                  pl.BlockSpec((halo, d_conv), lambda i: (jnp.maximum(i * (tm // halo) - 1, 0), 2)),
                  pl.BlockSpec((tm, d_attn), lambda i: (i, 0)),
                  pl.BlockSpec((tm, d), lambda i: (i, 0)),
                  pl.BlockSpec((tm, d), lambda i: (i, 1)),
                  pl.BlockSpec((tm, d), lambda i: (i, 0)),
                  pl.BlockSpec((1, 6, d), lambda i: (i // tiles_per_seq, 0, 0)),
                  resident((1, d)),
                  resident((CONV_WIDTH, d_conv)),
                  resident((d_conv, d)),
                  resident((d_attn, d)),
                  resident((d, d)),
                  resident((d, LANES)),
                  resident((1, LANES))],
        out_specs=[pl.BlockSpec((tm, d), lambda i: (i, 0)),
                   pl.BlockSpec((tm, d), lambda i: (i, 0)),
                   pl.BlockSpec((tm, LANES), lambda i: (i, 0))],
        out_shape=[jax.ShapeDtypeStruct((n, d), F32),
                   jax.ShapeDtypeStruct((n, d), F32),
                   jax.ShapeDtypeStruct((n, LANES), F32)],
        compiler_params=_cparams(("arbitrary",)),
        name="mix",
    )(proj, proj, proj, proj, proj, att, gates, gates, x2, mod3, g2.reshape(1, d), conv_w,
      wc, wa, wo, rw, rb)


def _route_kernel(lg_ref, idx_ref, prob_ref, cnt_ref, carry_ref):
    i = pl.program_id(0)

    @pl.when(i == 0)
    def _():
        carry_ref[...] = jnp.zeros_like(carry_ref)

    tt = lg_ref.shape[0]
    lane = lax.broadcasted_iota(jnp.int32, (tt, LANES), 1)
    lane_f = lane.astype(F32)
    l = jnp.where(lane < N_EXPERTS, lg_ref[...], -jnp.inf)
    onehot = jnp.zeros((tt, LANES), F32)
    vals, ids = [], []
    for _ in range(TOP_K):
        m = jnp.max(l, axis=1, keepdims=True)
        idx = jnp.min(jnp.where(l == m, lane_f, float(LANES)), axis=1, keepdims=True)
        sel = lane_f == idx
        vals.append(m)
        ids.append(idx)
        onehot = jnp.where(sel, 1.0, onehot)
        l = jnp.where(sel, -jnp.inf, l)
    es = [jnp.exp(v - vals[0]) for v in vals]
    denom = es[0] + es[1] + es[2] + es[3]
    r = lax.broadcasted_iota(jnp.int32, (tt, tt), 0)
    c = lax.broadcasted_iota(jnp.int32, (tt, tt), 1)
    before = jnp.where(c < r, 1.0, 0.0).astype(BF16)
    cnt_before = jnp.dot(before, onehot.astype(BF16), preferred_element_type=F32) + carry_ref[...]
    idx_out = jnp.zeros((tt, LANES), jnp.int32)
    prob_out = jnp.zeros((tt, LANES), F32)
    for k in range(TOP_K):
        rank = jnp.sum(jnp.where(lane_f == ids[k], cnt_before, 0.0), axis=1, keepdims=True)
        idx_out = jnp.where(lane == k, ids[k].astype(jnp.int32), idx_out)
        idx_out = jnp.where(lane == TOP_K + k, rank.astype(jnp.int32), idx_out)
        prob_out = jnp.where(lane == k, es[k] / denom, prob_out)
    idx_ref[...] = idx_out
    prob_ref[...] = prob_out
    carry_ref[...] = carry_ref[...] + jnp.sum(onehot, axis=0, keepdims=True)
    cnt_ref[...] = carry_ref[...].astype(jnp.int32)


def _route(logits):
    n = logits.shape[0]
    tt = 512
    return pl.pallas_call(
        _route_kernel,
        grid=(n // tt,),
        in_specs=[pl.BlockSpec((tt, LANES), lambda i: (i, 0))],
        out_specs=[pl.BlockSpec((tt, LANES), lambda i: (i, 0)),
                   pl.BlockSpec((tt, LANES), lambda i: (i, 0)),
                   pl.BlockSpec((1, LANES), lambda i: (0, 0))],
        out_shape=[jax.ShapeDtypeStruct((n, LANES), jnp.int32),
                   jax.ShapeDtypeStruct((n, LANES), F32),
                   jax.ShapeDtypeStruct((1, LANES), jnp.int32)],
        scratch_shapes=[pltpu.VMEM((1, LANES), F32)],
        compiler_params=_cparams(("arbitrary",)),
        name="route",
    )(logits)


def _zero_fill_rows(zero_rows, dst_hbm, row0, n_rows, sem, start):
    off = row0
    size = 1 << (zero_rows.shape[0].bit_length() - 1)
    while size >= ROW_ALIGN:
        @pl.when((n_rows & size) != 0)
        def _(off=off, size=size):
            cp = pltpu.make_async_copy(zero_rows.at[pl.ds(0, size), :],
                                       dst_hbm.at[pl.ds(pl.multiple_of(off, ROW_ALIGN), size), :], sem)
            if start:
                cp.start()
            else:
                cp.wait()
        off = off + (n_rows & size)
        size //= 2


def _dispatch_kernel(gs_ref, cnt_ref, gend_ref, dest_ref, h_ref, xs_hbm, zbuf, sem):
    j = pl.program_id(0)
    tt = h_ref.shape[0]
    n_total = xs_hbm.shape[0]

    @pl.when(j == 0)
    def _():
        zbuf[...] = jnp.zeros(zbuf.shape, F32)
        tail0 = gend_ref[0]

        def pad_rows(start):
            def group(e, carry):
                cnt = cnt_ref[e]

                def row(r, c2):
                    cp = pltpu.make_async_copy(zbuf.at[pl.ds(0, 1), :],
                                               xs_hbm.at[pl.ds(gs_ref[e] + r, 1), :], sem.at[2])
                    if start:
                        cp.start()
                    else:
                        cp.wait()
                    return c2
                lax.fori_loop(cnt, (cnt + ROW_ALIGN - 1) & (-ROW_ALIGN), row, 0)
                return carry
            lax.fori_loop(0, N_EXPERTS, group, 0)

        for start in (True, False):
            _zero_fill_rows(zbuf, xs_hbm, tail0, n_total - tail0, sem.at[2], start)
            pad_rows(start)

    def tok(r, carry):
        for k in range(TOP_K):
            d = dest_ref[0, 0, r * TOP_K + k]
            pltpu.make_async_copy(h_ref.at[pl.ds(r, 1), :], xs_hbm.at[pl.ds(d, 1), :],
                                  sem.at[0]).start(priority=k % 2)
        return carry
    lax.fori_loop(0, tt, tok, 0, unroll=True)

    for k in range(TOP_K):
        pltpu.make_async_copy(h_ref, xs_hbm.at[pl.ds(0, tt), :], sem.at[0]).wait()


def _dispatch(h2, dest, g_start, cnt, g_end, n_rows_total):
    n, d = h2.shape
    tt = DISP_TOKENS
    dest3 = dest.reshape(n // tt, 1, tt * TOP_K)
    grid_spec = pltpu.PrefetchScalarGridSpec(
        num_scalar_prefetch=3,
        grid=(n // tt,),
        in_specs=[pl.BlockSpec((1, 1, tt * TOP_K), lambda j, gs, ct, ge: (j, 0, 0),
                               memory_space=pltpu.SMEM),
                  pl.BlockSpec((tt, d), lambda j, gs, ct, ge: (j, 0))],
        out_specs=pl.BlockSpec(memory_space=pl.ANY),
        scratch_shapes=[pltpu.VMEM((ZERO_ROWS, d), F32), pltpu.SemaphoreType.DMA((3,))],
    )
    return pl.pallas_call(
        _dispatch_kernel,
        grid_spec=grid_spec,
        out_shape=jax.ShapeDtypeStruct((n_rows_total, d), F32),
        compiler_params=_cparams(("arbitrary",)),
        name="dispatch",
    )(g_start, cnt, g_end, dest3, h2)


def _for_blocks(n_blk, n_max, pre, body):
    for n in range(4, n_max + 1):
        @pl.when(n_blk == n)
        def _(n=n):
            pre()
            for rb in range(n):
                body(rb, n)

    @pl.when(n_blk < 4)
    def _():
        pre()
        for rb in range(3):
            @pl.when(rb < n_blk)
            def _(rb=rb):
                body(rb, None)


def _expert_kernel(n_act, n_down, sbe_ref, sbr_ref, sbn_ref, ns_ref, gend_ref,
                   xs_hbm, wg_ref, wu_ref, wd_ref, bg_ref, bu_ref, bd_ref, ys_hbm,
                   stage, xbf, act, out, wgu_ref, wdb_ref, ld_sem, st_sem):
    s = pl.program_id(0)
    c = pl.program_id(1)
    th = EXP_TH
    n_super = ns_ref[0]
    n_blocks = SB_ROWS // EXP_TM

    def slot_ref(slot):
        return stage.at[pl.ds(slot * EXP_TM, EXP_TM), :]

    def x_copy(sb, rb, slot):
        r0 = pl.multiple_of(sbr_ref[sb] + rb * EXP_TM, ROW_ALIGN)
        return pltpu.make_async_copy(xs_hbm.at[pl.ds(r0, EXP_TM), :], slot_ref(slot), ld_sem.at[slot])

    def y_copy(sb, rb):
        r0 = pl.multiple_of(sbr_ref[sb] + rb * EXP_TM, ROW_ALIGN)
        return pltpu.make_async_copy(out.at[pl.ds(rb * EXP_TM, EXP_TM), :],
                                     ys_hbm.at[pl.ds(r0, EXP_TM), :], st_sem.at[rb])

    @pl.when(jnp.logical_and(s == 0, c == 0))
    def _():
        stage[...] = jnp.zeros(stage.shape, F32)
        tail0 = gend_ref[0]
        for start in (True, False):
            _zero_fill_rows(stage, ys_hbm, tail0, ys_hbm.shape[0] - tail0, st_sem.at[0], start)
        x_copy(0, 0, 0).start()

    @pl.when(s < n_super)
    def _():
        n_blk = sbn_ref[s]
        n_blk_prev = sbn_ref[jnp.maximum(s - 1, 0)]
        expert = sbe_ref[s]

        @pl.when(c < n_act)
        def _():
            cols = pl.ds(pl.multiple_of(c * th, th), th)
            bg = bg_ref[pl.ds(expert, 1), cols]
            bu = bu_ref[pl.ds(expert, 1), cols]

            def cast_weights():
                wgu_ref[:, :th] = wg_ref[0].astype(BF16)
                wgu_ref[:, th:] = wu_ref[0].astype(BF16)

            def act_block(rb, n):
                rows = pl.ds(rb * EXP_TM, EXP_TM)
                gu = jnp.dot(xbf[rows, :], wgu_ref[...], preferred_element_type=F32)
                g = jnp.minimum(gu[:, :th] + bg, SWIGLU_LIMIT)
                u = jnp.clip(gu[:, th:] + bu, -SWIGLU_LIMIT, SWIGLU_LIMIT)
                a = (u + 1.0) * g * jax.nn.sigmoid(SWIGLU_ALPHA * g)
                act[rows, cols] = a.astype(BF16)

            def first_block(rb, n):
                if n is None:
                    @pl.when(rb + 1 < n_blk)
                    def _():
                        x_copy(s, rb + 1, (rb + 1) % 2).start()
                elif rb + 1 < n:
                    x_copy(s, rb + 1, (rb + 1) % 2).start()
                x_copy(s, rb, rb % 2).wait()
                xbf[pl.ds(rb * EXP_TM, EXP_TM), :] = slot_ref(rb % 2)[...].astype(BF16)
                act_block(rb, n)

            @pl.when(c == 0)
            def _():
                _for_blocks(n_blk, n_blocks, cast_weights, first_block)

            @pl.when(c > 0)
            def _():
                _for_blocks(n_blk, n_blocks, cast_weights, act_block)

        @pl.when(c >= n_act)
        def _():
            cols = pl.ds(pl.multiple_of((c - n_act) * EXP_TN, EXP_TN), EXP_TN)
            bd = bd_ref[pl.ds(expert, 1), cols]
            for rb in range(n_blocks):
                @pl.when(jnp.logical_and(c == n_act, jnp.logical_and(s > 0, rb < n_blk_prev)))
                def _(rb=rb):
                    y_copy(s - 1, rb).wait()

            def cast_weights():
                wdb_ref[...] = wd_ref[0].astype(BF16)

            def down_block(rb, n):
                rows = pl.ds(rb * EXP_TM, EXP_TM)
                out[rows, cols] = jnp.dot(act[rows, :], wdb_ref[...],
                                          preferred_element_type=F32) + bd

            _for_blocks(n_blk, n_blocks, cast_weights, down_block)

            @pl.when(c == n_act + n_down - 1)
            def _():
                for rb in range(n_blocks):
                    @pl.when(rb < n_blk)
                    def _(rb=rb):
                        y_copy(s, rb).start()

                @pl.when(s + 1 < n_super)
                def _():
                    x_copy(s + 1, 0, 0).start()

                @pl.when(s + 1 == n_super)
                def _():
                    for rb in range(n_blocks):
                        @pl.when(rb < n_blk)
                        def _(rb=rb):
                            y_copy(s, rb).wait()


def _expert(xs, sb_expert, sb_row0, sb_nblk, n_super, g_end, wg, wu, wd, bg, bu, bd):
    n_rows_total, d = xs.shape
    s_max = sb_expert.shape[0]
    d_exp = wg.shape[2]
    assert d_exp % EXP_TH == 0 and d % EXP_TN == 0 and ZERO_ROWS == 2 * EXP_TM
    n_act = d_exp // EXP_TH
    n_down = d // EXP_TN
    ne = wg.shape[0]

    def e_eff(s, sbe, ns):
        return sbe[jnp.minimum(s, ns[0] - 1)]

    def c_act(s, c, ns):
        return jnp.where(s < ns[0], jnp.minimum(c, n_act - 1), n_act - 1)

    def c_down(s, c, ns):
        return jnp.where(s < ns[0], jnp.maximum(c - n_act, 0), n_down - 1)

    act_w = pl.BlockSpec((1, d, EXP_TH),
                         lambda s, c, sbe, sbr, sbn, ns, ge: (e_eff(s, sbe, ns), 0, c_act(s, c, ns)))
    bias = pl.BlockSpec((ne, d_exp), lambda s, c, sbe, sbr, sbn, ns, ge: (0, 0))
    grid_spec = pltpu.PrefetchScalarGridSpec(
        num_scalar_prefetch=5,
        grid=(s_max, n_act + n_down),
        in_specs=[
            pl.BlockSpec(memory_space=pl.ANY),
            act_w, act_w,
            pl.BlockSpec((1, d_exp, EXP_TN),
                         lambda s, c, sbe, sbr, sbn, ns, ge: (e_eff(s, sbe, ns), 0, c_down(s, c, ns))),
            bias, bias,
            pl.BlockSpec((ne, d), lambda s, c, sbe, sbr, sbn, ns, ge: (0, 0)),
        ],
        out_specs=pl.BlockSpec(memory_space=pl.ANY),
        scratch_shapes=[pltpu.VMEM((ZERO_ROWS, d), F32), pltpu.VMEM((SB_ROWS, d), BF16),
                        pltpu.VMEM((SB_ROWS, d_exp), BF16), pltpu.VMEM((SB_ROWS, d), F32),
                        pltpu.VMEM((d, 2 * EXP_TH), BF16), pltpu.VMEM((d_exp, EXP_TN), BF16),
                        pltpu.SemaphoreType.DMA((2,)),
                        pltpu.SemaphoreType.DMA((SB_ROWS // EXP_TM,))],
    )
    return pl.pallas_call(
        functools.partial(_expert_kernel, n_act, n_down),
        grid_spec=grid_spec,
        out_shape=jax.ShapeDtypeStruct((n_rows_total, d), F32),
        compiler_params=_cparams(("arbitrary", "arbitrary")),
        name="expert",
    )(sb_expert, sb_row0, sb_nblk, n_super, g_end, xs, wg, wu, wd, bg, bu, bd)


def _combine_start(y_hbm, pos_ref, stage_slot, sem):
    tt = stage_slot.shape[1]

    def body(r, carry):
        for k in range(TOP_K):
            p = pos_ref[0, 0, r * TOP_K + k]
            pltpu.make_async_copy(y_hbm.at[pl.ds(p, 1), :], stage_slot.at[k, pl.ds(r, 1), :],
                                  sem).start(priority=k % 2)
        return carry
    lax.fori_loop(0, tt, body, 0, unroll=True)


def _combine_kernel(final_norm, pos_cur_ref, pos_nxt_ref, y_hbm, prob_ref, x1_ref, mod_ref, g_ref,
                    o_ref, stage, sem):
    j = pl.program_id(0)
    nj = pl.num_programs(0)
    tt = stage.shape[2]

    @pl.when(j == 0)
    def _():
        _combine_start(y_hbm, pos_cur_ref, stage.at[0], sem.at[0])

    @pl.when(j + 1 < nj)
    def _():
        nxt = (j + 1) % 2
        _combine_start(y_hbm, pos_nxt_ref, stage.at[nxt], sem.at[nxt])

    cur = j % 2
    for k in range(TOP_K):
        pltpu.make_async_copy(y_hbm.at[pl.ds(0, tt), :], stage.at[cur, k], sem.at[cur]).wait()
    prob = prob_ref[...]
    y = prob[:, 0:1] * stage[cur, 0]
    for k in range(1, TOP_K):
        y = y + prob[:, k:k + 1] * stage[cur, k]
    x2 = x1_ref[...] + mod_ref[0, 5:6, :] * y
    if final_norm:
        r = lax.rsqrt(jnp.mean(x2 * x2, axis=-1, keepdims=True) + RMS_EPS)
        x2 = (x2 * r) * g_ref[...]
    o_ref[...] = x2


def _combine(yb, pos, probs, x1, mod3, g, seq, final_norm):
    n, d = x1.shape
    tt = COMB_ROWS
    nt = n // tt
    tiles_per_seq = seq // tt
    pos3 = pos.reshape(nt, 1, tt * TOP_K)
    return pl.pallas_call(
        functools.partial(_combine_kernel, final_norm),
        grid=(nt,),
        in_specs=[pl.BlockSpec((1, 1, tt * TOP_K), lambda j: (j, 0, 0), memory_space=pltpu.SMEM),
                  pl.BlockSpec((1, 1, tt * TOP_K), lambda j: (jnp.minimum(j + 1, nt - 1), 0, 0),
                               memory_space=pltpu.SMEM),
                  pl.BlockSpec(memory_space=pl.ANY),
                  pl.BlockSpec((tt, LANES), lambda j: (j, 0)),
                  pl.BlockSpec((tt, d), lambda j: (j, 0)),
                  pl.BlockSpec((1, 6, d), lambda j: (j // tiles_per_seq, 0, 0)),
                  pl.BlockSpec((1, d), lambda j: (0, 0))],
        out_specs=pl.BlockSpec((tt, d), lambda j: (j, 0)),
        out_shape=jax.ShapeDtypeStruct((n, d), F32),
        scratch_shapes=[pltpu.VMEM((2, TOP_K, tt, d), F32), pltpu.SemaphoreType.DMA((2,))],
        compiler_params=_cparams(("arbitrary",)),
        name="combine",
    )(pos3, pos3, yb, probs, x1, mod3, g.reshape(1, d))


def _count_le(sorted_ends, q):
    return jnp.sum((sorted_ends[None, :] <= q[:, None]).astype(jnp.int32), axis=1)


def _plan(counts, n_tok):
    s_max = -(-n_tok * TOP_K // SB_ROWS) + N_EXPERTS
    cnt = counts[0, :N_EXPERTS]
    cnt_al = (cnt + ROW_ALIGN - 1) // ROW_ALIGN * ROW_ALIGN
    g_end = jnp.cumsum(cnt_al)
    g_start = g_end - cnt_al
    nsb_e = (cnt + SB_ROWS - 1) // SB_ROWS
    sb_end = jnp.cumsum(nsb_e)
    sb_start = sb_end - nsb_e
    n_super = sb_end[-1]
    s_ids = jnp.arange(s_max, dtype=jnp.int32)
    sb_expert = jnp.minimum(_count_le(sb_end, s_ids), N_EXPERTS - 1)
    j_in = s_ids - sb_start[sb_expert]
    live = s_ids < n_super
    sb_rows = jnp.where(live, jnp.clip(cnt[sb_expert] - j_in * SB_ROWS, 0, SB_ROWS), 0)
    sb_row0 = jnp.where(live, g_start[sb_expert] + j_in * SB_ROWS, 0)
    sb_nblk = (sb_rows + EXP_TM - 1) // EXP_TM
    i32 = lambda a: a.astype(jnp.int32)
    return (i32(g_start), i32(cnt), i32(g_end[-1:]), i32(sb_expert), i32(sb_row0), i32(sb_nblk),
            i32(n_super.reshape(1)))


def _dest_kernel(ir_ref, gs_ref, o_ref):
    ir = ir_ref[...].astype(F32)
    lane = lax.broadcasted_iota(jnp.int32, ir.shape, 1)
    lane_f = lane.astype(F32)
    out = jnp.zeros(ir.shape, F32)
    for k in range(TOP_K):
        e = jnp.sum(jnp.where(lane == k, ir, 0.0), axis=1, keepdims=True)
        r = jnp.sum(jnp.where(lane == TOP_K + k, ir, 0.0), axis=1, keepdims=True)
        g = jnp.sum(jnp.where(lane_f == e, gs_ref[...], 0.0), axis=1, keepdims=True)
        out = jnp.where(lane == k, g + r, out)
    o_ref[...] = out.astype(jnp.int32)


def _dest(idx_rank, g_start):
    n = idx_rank.shape[0]
    tt = 1024
    gs = jnp.pad(g_start.astype(F32), (0, LANES - N_EXPERTS)).reshape(1, LANES)
    return pl.pallas_call(
        _dest_kernel,
        grid=(n // tt,),
        in_specs=[pl.BlockSpec((tt, LANES), lambda i: (i, 0)),
                  pl.BlockSpec((1, LANES), lambda i: (0, 0))],
        out_specs=pl.BlockSpec((tt, LANES), lambda i: (i, 0)),
        out_shape=jax.ShapeDtypeStruct((n, LANES), jnp.int32),
        compiler_params=_cparams(("arbitrary",)),
        name="dest",
    )(idx_rank, gs)


def kernel(x, c, ada_w, ada_b, norm_mix_g, w_in, b_forget, conv_w, w_conv_out, w_attn_out, w_out,
           norm_ffn_g, router_w, router_b, exp_w_gate, exp_b_gate, exp_w_up, exp_b_up,
           exp_w_down, exp_b_down, final_norm_g):
    bsz, seq, d = x.shape
    n = bsz * seq
    d_conv = conv_w.shape[2]
    d_attn = w_attn_out.shape[1]
    n_main = 3 * d_conv + 3 * d_attn
    x2 = x.reshape(n, d)
    out = x2
    for l in range(ada_w.shape[0]):
        mod3 = _mod(c, ada_w[l], ada_b[l]).reshape(bsz, 6, d)
        proj, gates, f_logit = _inproj(out, mod3, norm_mix_g[l], w_in[l].T, n_main, N_HEADS,
                                       w_in.shape[2] - n_main - N_HEADS, seq,
                                       (3 * d_conv, 3 * d_conv + d_attn))
        fcol = _cumf(f_logit, b_forget[l], seq)
        att = _attn(proj, fcol, bsz, seq, 3 * d_conv, 3 * d_conv + d_attn, 3 * d_conv + 2 * d_attn)
        rw = jnp.pad(router_w[l], ((0, 0), (0, LANES - N_EXPERTS)))
        rb = jnp.pad(router_b[l], (0, LANES - N_EXPERTS)).reshape(1, LANES)
        x1, h2, logits = _mix(proj, gates, att, out, mod3, norm_ffn_g[l], conv_w[l],
                              _cast_bf16(w_conv_out[l]), _cast_bf16(w_attn_out[l]), _cast_bf16(w_out[l]),
                              rw, rb, seq, d_conv)
        idx_rank, probs, counts = _route(logits)
        g_start, cnt, g_end, sb_expert, sb_row0, sb_nblk, n_super = _plan(counts, n)
        dest = _dest(idx_rank, g_start)[:, :TOP_K].reshape(-1)
        n_rows_total = n * TOP_K + N_EXPERTS * ROW_ALIGN + EXP_TM
        xs = _dispatch(h2, dest, g_start, cnt, g_end, n_rows_total)
        ys = _expert(xs, sb_expert, sb_row0, sb_nblk, n_super, g_end, exp_w_gate[l], exp_w_up[l],
                     exp_w_down[l], exp_b_gate[l], exp_b_up[l], exp_b_down[l])
        out = _combine(ys, dest, probs, x1, mod3, final_norm_g, seq,
                       final_norm=(l == ada_w.shape[0] - 1))
    return out.reshape(bsz, seq, d)
```

```python
import functools

import jax
import jax.numpy as jnp
from jax import lax
from jax.experimental import pallas as pl
from jax.experimental.pallas import tpu as pltpu

F32 = jnp.float32
BF16 = jnp.bfloat16

N_HEADS = 16
HEAD_DIM = 64
N_EXPERTS = 32
TOP_K = 4
SWIGLU_LIMIT = 7.0
SWIGLU_ALPHA = 1.702
RMS_EPS = 1e-6
CONV_WIDTH = 3

LOG2E = 1.4426950408889634
QK_SCALE = HEAD_DIM ** -0.5 * LOG2E

LANES = 128
VMEM_LIMIT = 58 * 1024 * 1024

ROW_ALIGN = 8

EXP_TM = 272
SB_ROWS = 6 * EXP_TM
EXP_TH = 256
EXP_TN = 512
ZERO_ROWS = 2 * EXP_TM
DISP_TOKENS = 256
COMB_ROWS = 128


def _cparams(sem, vmem=VMEM_LIMIT):
    return pltpu.CompilerParams(dimension_semantics=sem, vmem_limit_bytes=vmem)


def _mod_kernel(cb_ref, w_ref, b_ref, o_ref):
    w = w_ref[...]
    tn = w.shape[1]
    rows = []
    for b in range(cb_ref.shape[0]):
        cv = cb_ref[b]
        ca = cv * jax.nn.sigmoid(cv)
        cols = [jnp.sum(w[:, j * LANES:(j + 1) * LANES] * ca, axis=0, keepdims=True)
                for j in range(tn // LANES)]
        rows.append(jnp.concatenate(cols, axis=1))
    o_ref[...] = jnp.concatenate(rows, axis=0) + b_ref[...]


def _mod(c, w, bias):
    bsz, d = c.shape
    n_out = w.shape[1]
    tn = 1024
    cb = jnp.broadcast_to(c[:, :, None], (bsz, d, LANES))
    return pl.pallas_call(
        _mod_kernel,
        grid=(n_out // tn,),
        in_specs=[pl.BlockSpec((bsz, d, LANES), lambda j: (0, 0, 0)),
                  pl.BlockSpec((d, tn), lambda j: (0, j)),
                  pl.BlockSpec((1, tn), lambda j: (0, j))],
        out_specs=pl.BlockSpec((bsz, tn), lambda j: (0, j)),
        out_shape=jax.ShapeDtypeStruct((bsz, n_out), F32),
        compiler_params=_cparams(("arbitrary",)),
        name="mod",
    )(cb, w, bias.reshape(1, n_out))


def _cast_kernel(x_ref, o_ref):
    o_ref[...] = x_ref[...].astype(o_ref.dtype)


def _cast_bf16(w):
    r, c = w.shape
    tr = 512
    return pl.pallas_call(
        _cast_kernel,
        grid=(r // tr,),
        in_specs=[pl.BlockSpec((tr, c), lambda i: (i, 0))],
        out_specs=pl.BlockSpec((tr, c), lambda i: (i, 0)),
        out_shape=jax.ShapeDtypeStruct((r, c), BF16),
        compiler_params=_cparams(("arbitrary",)),
        name="cast_bf16",
    )(w)


def _rmsnorm_mod(x, g, scale, shift):
    r = lax.rsqrt(jnp.mean(x * x, axis=-1, keepdims=True) + RMS_EPS)
    return (x * r) * g * (1.0 + scale) + shift


def _inproj_kernel(nj_main, jq, x_ref, mod_ref, g_ref, w_ref, wf_ref,
                   om_ref, og_ref, of_ref, h_ref, wfp_ref):
    j = pl.program_id(1)

    @pl.when(j == 0)
    def _():
        wfp_ref[...] = jnp.zeros(wfp_ref.shape, BF16)
        wfp_ref[0:wf_ref.shape[0], :] = wf_ref[...].astype(BF16)
        rows = 256

        def body(rb, carry):
            r0 = pl.multiple_of(rb * rows, rows)
            h = _rmsnorm_mod(x_ref[pl.ds(r0, rows), :], g_ref[...], mod_ref[0, 1:2, :],
                             mod_ref[0, 0:1, :])
            hb = h.astype(BF16)
            h_ref[pl.ds(r0, rows), :] = hb
            of_ref[pl.ds(r0, rows), :] = _dot_nt(hb, wfp_ref[...])
            return carry

        lax.fori_loop(0, x_ref.shape[0] // rows, body, 0)

    @pl.when(j < nj_main)
    def _():
        scale = jnp.where(jnp.logical_and(j >= jq[0], j < jq[1]), QK_SCALE, 1.0)
        om_ref[...] = (_dot_nt(h_ref[...], w_ref[...].astype(BF16)) * scale).astype(om_ref.dtype)

    @pl.when(j >= nj_main)
    def _():
        og_ref[...] = _dot_nt(h_ref[...], w_ref[...].astype(BF16)).astype(og_ref.dtype)


def _inproj(x2, mod3, g, w_in_t, n_main, n_f, n_gates, seq, q_cols):
    n, d = x2.shape
    tm, tn = 1024, 1024
    nj_main = n_main // tn
    assert n_main % tn == 0 and n_gates % tn == 0 and q_cols[0] % tn == 0 and q_cols[1] % tn == 0
    jq = (q_cols[0] // tn, q_cols[1] // tn)
    nj_g = n_gates // tn
    gate_row0 = n_main + n_f
    tiles_per_seq = seq // tm
    return pl.pallas_call(
        functools.partial(_inproj_kernel, nj_main, jq),
        grid=(n // tm, nj_main + nj_g),
        in_specs=[pl.BlockSpec((tm, d), lambda i, j: (i, 0)),
                  pl.BlockSpec((1, 6, d), lambda i, j: (i // tiles_per_seq, 0, 0)),
                  pl.BlockSpec((1, d), lambda i, j: (0, 0)),
                  pl.BlockSpec((pl.Element(tn), pl.Element(d)),
                               lambda i, j: (pl.multiple_of(
                                   jnp.where(j < nj_main, tn * j, gate_row0 + tn * (j - nj_main)), n_f), 0)),
                  pl.BlockSpec((n_f, d), lambda i, j: (n_main // n_f, 0))],
        out_specs=[pl.BlockSpec((tm, tn), lambda i, j: (i, jnp.minimum(j, nj_main - 1))),
                   pl.BlockSpec((tm, tn), lambda i, j: (i, jnp.maximum(j - nj_main, 0))),
                   pl.BlockSpec((tm, LANES), lambda i, j: (i, 0))],
        out_shape=[jax.ShapeDtypeStruct((n, n_main), BF16),
                   jax.ShapeDtypeStruct((n, n_gates), BF16),
                   jax.ShapeDtypeStruct((n, LANES), F32)],
        scratch_shapes=[pltpu.VMEM((tm, d), BF16), pltpu.VMEM((LANES, d), BF16)],
        compiler_params=_cparams(("arbitrary", "arbitrary")),
        name="inproj",
    )(x2, mod3, g.reshape(1, d), w_in_t, w_in_t)


def _cumf_kernel(blocks_per_seq, f_ref, bf_ref, fc_ref, carry_ref):
    i = pl.program_id(0)

    @pl.when(i % blocks_per_seq == 0)
    def _():
        carry_ref[...] = jnp.zeros_like(carry_ref)

    z = f_ref[...] + bf_ref[...]
    lf = -(jnp.maximum(-z, 0.0) + jnp.log1p(jnp.exp(-jnp.abs(z))))
    t = z.shape[0]
    row = lax.broadcasted_iota(jnp.int32, (t, t), 0)
    col = lax.broadcasted_iota(jnp.int32, (t, t), 1)
    tri = jnp.where(col <= row, 1.0, 0.0).astype(BF16)
    parts = [jnp.dot(tri, p, preferred_element_type=F32) for p in _split3(lf)]
    fb = (parts[0] + (parts[1] + parts[2])) + carry_ref[...]
    fc_ref[...] = fb
    carry_ref[...] = fb[t - 1:t, :]


def _cumf(f_logit, b_forget, seq):
    n = f_logit.shape[0]
    t = 512
    bf = jnp.pad(b_forget, (0, LANES - b_forget.shape[0])).reshape(1, LANES)
    return pl.pallas_call(
        functools.partial(_cumf_kernel, seq // t),
        grid=(n // t,),
        in_specs=[pl.BlockSpec((t, LANES), lambda i: (i, 0)),
                  pl.BlockSpec((1, LANES), lambda i: (0, 0))],
        out_specs=pl.BlockSpec((t, LANES), lambda i: (i, 0)),
        out_shape=jax.ShapeDtypeStruct((n, LANES), F32),
        scratch_shapes=[pltpu.VMEM((1, LANES), F32)],
        compiler_params=_cparams(("arbitrary",)),
        name="cumf",
    )(f_logit, bf)


def _dot_nt(a, b):
    return lax.dot_general(a, b, (((1,), (1,)), ((), ())), preferred_element_type=F32)


def _split3(x):
    hi = x.astype(BF16)
    r1 = x - hi.astype(F32)
    mid = r1.astype(BF16)
    lo = (r1 - mid.astype(F32)).astype(BF16)
    return hi, mid, lo


def _bias_lanes(parts, head, lane0):
    r = lax.broadcasted_iota(jnp.int32, (LANES, LANES), 0)
    c = lax.broadcasted_iota(jnp.int32, (LANES, LANES), 1)
    out = None
    for j, part in enumerate(parts):
        sel = jnp.where(r == head, jnp.where(c == lane0 + j, -1.0, jnp.where(c == lane0 + 3 + j, 1.0, 0.0)),
                        0.0).astype(BF16)
        term = jnp.dot(part, sel, preferred_element_type=F32)
        out = term if out is None else out + term
    return out


def _attn_kernel(tq, tk, q_ref, k_ref, v_ref, fc_ref, o_ref, kp_ref, qp_ref, vt_ref, m_ref, l_ref,
                 acc_ref, sa_ref, sb_ref):
    hp = pl.program_id(1)
    qi = pl.program_id(2)
    seq = k_ref.shape[0]
    lane = lax.broadcasted_iota(jnp.int32, (1, LANES), 1)
    own = (lane < HEAD_DIM, lane >= HEAD_DIM)
    spare = (HEAD_DIM, 0)
    ones_k = tuple(jnp.logical_and(lane >= spare[h] + 3, lane < spare[h] + 6) for h in range(2))
    ones_q = tuple(jnp.logical_and(lane >= spare[h], lane < spare[h] + 3) for h in range(2))

    @pl.when(qi == 0)
    def _():
        def prep(ci, carry):
            r0 = pl.multiple_of(ci * tk, tk)
            f = fc_ref[pl.ds(r0, tk), :] * LOG2E
            k = k_ref[pl.ds(r0, tk), :]
            q = q_ref[pl.ds(r0, tk), :]
            parts = _split3(f)
            for h in range(2):
                both = _bias_lanes(parts, 2 * hp + h, spare[h])
                kp_ref[h, pl.ds(r0, tk), :] = jnp.where(own[h], k, jnp.where(ones_k[h], 1.0, both).astype(BF16))
                qp_ref[h, pl.ds(r0, tk), :] = jnp.where(own[h], q, jnp.where(ones_q[h], 1.0, both).astype(BF16))
            vt_ref[:, pl.ds(r0, tk)] = v_ref[pl.ds(r0, tk), :].astype(F32).T.astype(BF16)
            return carry
        lax.fori_loop(0, seq // tk, prep, 0)

    q0 = pl.multiple_of(qi * tq, tq)
    qp = [qp_ref[h, pl.ds(q0, tq), :] for h in range(2)]

    m_ref[...] = jnp.full(m_ref.shape, -jnp.inf, F32)
    l_ref[...] = jnp.zeros(l_ref.shape, F32)
    acc_ref[...] = jnp.zeros(acc_ref.shape, F32)

    def scores(kt, s_ref):
        k0 = kt * tk
        for h in range(2):
            s_ref[h] = _dot_nt(kp_ref[h, pl.ds(k0, tk), :], qp[h])

    def update(kt, s_ref, masked):
        k0 = kt * tk
        for h in range(2):
            s = s_ref[h]
            if masked:
                kr = lax.broadcasted_iota(jnp.int32, (tk, tq), 0)
                qc = lax.broadcasted_iota(jnp.int32, (tk, tq), 1)
                s = jnp.where(kr <= qc, s, -jnp.inf)
            m_old = m_ref[h]
            m_new = jnp.maximum(m_old, jnp.max(s, axis=0, keepdims=True))
            alpha = jnp.exp2(m_old - m_new)
            p = jnp.exp2(s - m_new)
            l_ref[h] = l_ref[h] * alpha + jnp.sum(p, axis=0, keepdims=True)
            m_ref[h] = m_new
            rows = pl.ds(h * HEAD_DIM, HEAD_DIM)
            pv = jnp.dot(vt_ref[rows, pl.ds(k0, tk)], p.astype(BF16), preferred_element_type=F32)
            acc_ref[rows, :] = acc_ref[rows, :] * alpha + pv

    bufs = (sa_ref, sb_ref)
    for k in range(seq // tq):
        @pl.when(qi == k)
        def _(k=k):
            scores(0, bufs[0])
            for t in range(k):
                scores(t + 1, bufs[(t + 1) % 2])
                update(t, bufs[t % 2], False)
            update(k, bufs[k % 2], True)

    out_t = jnp.concatenate([acc_ref[pl.ds(h * HEAD_DIM, HEAD_DIM), :] / l_ref[h] for h in range(2)],
                            axis=0)
    o_ref[...] = out_t.T.astype(o_ref.dtype)


def _attn(proj, fcol, bsz, seq, q_col0, k_col0, v_col0):
    tq = tk = 512
    nq = seq // tq
    n_pairs = N_HEADS * HEAD_DIM // LANES
    qb, kb, vb = q_col0 // LANES, k_col0 // LANES, v_col0 // LANES
    return pl.pallas_call(
        functools.partial(_attn_kernel, tq, tk),
        grid=(bsz, n_pairs, nq),
        in_specs=[pl.BlockSpec((seq, LANES), lambda b, hp, qi: (b, qb + hp)),
                  pl.BlockSpec((seq, LANES), lambda b, hp, qi: (b, kb + hp)),
                  pl.BlockSpec((seq, LANES), lambda b, hp, qi: (b, vb + hp)),
                  pl.BlockSpec((seq, LANES), lambda b, hp, qi: (b, 0))],
        out_specs=pl.BlockSpec((tq, LANES), lambda b, hp, qi: (b * nq + qi, hp)),
        out_shape=jax.ShapeDtypeStruct((bsz * seq, N_HEADS * HEAD_DIM), BF16),
        scratch_shapes=[pltpu.VMEM((2, seq, LANES), BF16), pltpu.VMEM((2, seq, LANES), BF16),
                        pltpu.VMEM((LANES, seq), BF16),
                        pltpu.VMEM((2, 1, tq), F32), pltpu.VMEM((2, 1, tq), F32),
                        pltpu.VMEM((LANES, tq), F32),
                        pltpu.VMEM((2, tk, tq), F32), pltpu.VMEM((2, tk, tq), F32)],
        compiler_params=_cparams(("arbitrary", "arbitrary", "arbitrary")),
        name="attn",
    )(proj, proj, proj, fcol)


def _mix_kernel(tiles_per_seq, xc_ref, cb_ref, cc_ref, hxc_ref, hcc_ref, att_ref, gc_ref, ga_ref,
                x_ref, mod_ref, g2_ref, cw_ref, wc_ref, wa_ref, wo_ref, rw_ref, rb_ref,
                x1_ref, h2_ref, lg_ref):
    i = pl.program_id(0)
    u = cc_ref[...].astype(F32) * xc_ref[...].astype(F32)
    hu = hcc_ref[...].astype(F32) * hxc_ref[...].astype(F32)
    hu = jnp.where(i % tiles_per_seq == 0, 0.0, hu)
    nh = hu.shape[0]
    row = lax.broadcasted_iota(jnp.int32, u.shape, 0)
    u1 = jnp.where(row == 0, hu[nh - 1:nh, :], pltpu.roll(u, 1, axis=0))
    u2 = jnp.where(row == 0, hu[nh - 2:nh - 1, :],
                   jnp.where(row == 1, hu[nh - 1:nh, :], pltpu.roll(u, 2, axis=0)))
    conv = cw_ref[0:1, :] * u2 + cw_ref[1:2, :] * u1 + cw_ref[2:3, :] * u
    z = (cb_ref[...].astype(F32) * conv).astype(BF16)
    y_conv = jnp.dot(z, wc_ref[...], preferred_element_type=F32)
    y_attn = jnp.dot(att_ref[...], wa_ref[...], preferred_element_type=F32)
    merged = (jax.nn.sigmoid(gc_ref[...].astype(F32)) * y_conv
              + jax.nn.sigmoid(ga_ref[...].astype(F32)) * y_attn)
    o = jnp.dot(merged.astype(BF16), wo_ref[...], preferred_element_type=F32)
    x1 = x_ref[...] + mod_ref[0, 2:3, :] * o
    x1_ref[...] = x1
    h2 = _rmsnorm_mod(x1, g2_ref[...], mod_ref[0, 4:5, :], mod_ref[0, 3:4, :])
    h2_ref[...] = h2
    h_hi, h_lo, _ = _split3(h2)
    w_hi, w_lo, _ = _split3(rw_ref[...])
    lg_ref[...] = (jnp.dot(h_hi, w_hi, preferred_element_type=F32)
                   + (jnp.dot(h_hi, w_lo, preferred_element_type=F32)
                      + jnp.dot(h_lo, w_hi, preferred_element_type=F32))) + rb_ref[...]


def _mix(proj, gates, att, x2, mod3, g2, conv_w, wc, wa, wo, rw, rb, seq, d_conv):
    n, d = x2.shape
    tm = 256
    halo = 16
    tiles_per_seq = seq // tm
    d_attn = att.shape[1]

    def resident(shape):
        return pl.BlockSpec(shape, lambda i: (0,) * len(shape), pipeline_mode=pl.Buffered(1))

    return pl.pallas_call(
        functools.partial(_mix_kernel, tiles_per_seq),
        grid=(n // tm,),
        in_specs=[pl.BlockSpec((tm, d_conv), lambda i: (i, 0)),
                  pl.BlockSpec((tm, d_conv), lambda i: (i, 1)),
                  pl.BlockSpec((tm, d_conv), lambda i: (i, 2)),
                  pl.BlockSpec((halo, d_conv), lambda i: (jnp.maximum(i * (tm // halo) - 1, 0), 0)),
                  pl.BlockSpec((halo, d_conv), lambda i: (jnp.maximum(i * (tm // halo) - 1, 0), 2)),
                  pl.BlockSpec((tm, d_attn), lambda i: (i, 0)),
                  pl.BlockSpec((tm, d), lambda i: (i, 0)),
                  pl.BlockSpec((tm, d), lambda i: (i, 1)),
                  pl.BlockSpec((tm, d), lambda i: (i, 0)),
                  pl.BlockSpec((1, 6, d), lambda i: (i // tiles_per_seq, 0, 0)),
                  resident((1, d)),
                  resident((CONV_WIDTH, d_conv)),
                  resident((d_conv, d)),
                  resident((d_attn, d)),
                  resident((d, d)),
                  resident((d, LANES)),
                  resident((1, LANES))],
        out_specs=[pl.BlockSpec((tm, d), lambda i: (i, 0)),
                   pl.BlockSpec((tm, d), lambda i: (i, 0)),
                   pl.BlockSpec((tm, LANES), lambda i: (i, 0))],
        out_shape=[jax.ShapeDtypeStruct((n, d), F32),
                   jax.ShapeDtypeStruct((n, d), F32),
                   jax.ShapeDtypeStruct((n, LANES), F32)],
        compiler_params=_cparams(("arbitrary",)),
        name="mix",
    )(proj, proj, proj, proj, proj, att, gates, gates, x2, mod3, g2.reshape(1, d), conv_w,
      wc, wa, wo, rw, rb)


def _route_kernel(lg_ref, idx_ref, prob_ref, cnt_ref, carry_ref):
    i = pl.program_id(0)

    @pl.when(i == 0)
    def _():
        carry_ref[...] = jnp.zeros_like(carry_ref)

    tt = lg_ref.shape[0]
    lane = lax.broadcasted_iota(jnp.int32, (tt, LANES), 1)
    lane_f = lane.astype(F32)
    l = jnp.where(lane < N_EXPERTS, lg_ref[...], -jnp.inf)
    onehot = jnp.zeros((tt, LANES), F32)
    vals, ids = [], []
    for _ in range(TOP_K):
        m = jnp.max(l, axis=1, keepdims=True)
        idx = jnp.min(jnp.where(l == m, lane_f, float(LANES)), axis=1, keepdims=True)
        sel = lane_f == idx
        vals.append(m)
        ids.append(idx)
        onehot = jnp.where(sel, 1.0, onehot)
        l = jnp.where(sel, -jnp.inf, l)
    es = [jnp.exp(v - vals[0]) for v in vals]
    denom = es[0] + es[1] + es[2] + es[3]
    r = lax.broadcasted_iota(jnp.int32, (tt, tt), 0)
    c = lax.broadcasted_iota(jnp.int32, (tt, tt), 1)
    before = jnp.where(c < r, 1.0, 0.0).astype(BF16)
    cnt_before = jnp.dot(before, onehot.astype(BF16), preferred_element_type=F32) + carry_ref[...]
    idx_out = jnp.zeros((tt, LANES), jnp.int32)
    prob_out = jnp.zeros((tt, LANES), F32)
    for k in range(TOP_K):
        rank = jnp.sum(jnp.where(lane_f == ids[k], cnt_before, 0.0), axis=1, keepdims=True)
        idx_out = jnp.where(lane == k, ids[k].astype(jnp.int32), idx_out)
        idx_out = jnp.where(lane == TOP_K + k, rank.astype(jnp.int32), idx_out)
        prob_out = jnp.where(lane == k, es[k] / denom, prob_out)
    idx_ref[...] = idx_out
    prob_ref[...] = prob_out
    carry_ref[...] = carry_ref[...] + jnp.sum(onehot, axis=0, keepdims=True)
    cnt_ref[...] = carry_ref[...].astype(jnp.int32)


def _route(logits):
    n = logits.shape[0]
    tt = 512
    return pl.pallas_call(
        _route_kernel,
        grid=(n // tt,),
        in_specs=[pl.BlockSpec((tt, LANES), lambda i: (i, 0))],
        out_specs=[pl.BlockSpec((tt, LANES), lambda i: (i, 0)),
                   pl.BlockSpec((tt, LANES), lambda i: (i, 0)),
                   pl.BlockSpec((1, LANES), lambda i: (0, 0))],
        out_shape=[jax.ShapeDtypeStruct((n, LANES), jnp.int32),
                   jax.ShapeDtypeStruct((n, LANES), F32),
                   jax.ShapeDtypeStruct((1, LANES), jnp.int32)],
        scratch_shapes=[pltpu.VMEM((1, LANES), F32)],
        compiler_params=_cparams(("arbitrary",)),
        name="route",
    )(logits)


def _zero_fill_rows(zero_rows, dst_hbm, row0, n_rows, sem, start):
    off = row0
    size = 1 << (zero_rows.shape[0].bit_length() - 1)
    while size >= ROW_ALIGN:
        @pl.when((n_rows & size) != 0)
        def _(off=off, size=size):
            cp = pltpu.make_async_copy(zero_rows.at[pl.ds(0, size), :],
                                       dst_hbm.at[pl.ds(pl.multiple_of(off, ROW_ALIGN), size), :], sem)
            if start:
                cp.start()
            else:
                cp.wait()
        off = off + (n_rows & size)
        size //= 2


def _dispatch_kernel(gs_ref, cnt_ref, gend_ref, dest_ref, h_ref, xs_hbm, zbuf, sem):
    j = pl.program_id(0)
    tt = h_ref.shape[0]
    n_total = xs_hbm.shape[0]

    @pl.when(j == 0)
    def _():
        zbuf[...] = jnp.zeros(zbuf.shape, F32)
        tail0 = gend_ref[0]

        def pad_rows(start):
            def group(e, carry):
                cnt = cnt_ref[e]

                def row(r, c2):
                    cp = pltpu.make_async_copy(zbuf.at[pl.ds(0, 1), :],
                                               xs_hbm.at[pl.ds(gs_ref[e] + r, 1), :], sem.at[2])
                    if start:
                        cp.start()
                    else:
                        cp.wait()
                    return c2
                lax.fori_loop(cnt, (cnt + ROW_ALIGN - 1) & (-ROW_ALIGN), row, 0)
                return carry
            lax.fori_loop(0, N_EXPERTS, group, 0)

        for start in (True, False):
            _zero_fill_rows(zbuf, xs_hbm, tail0, n_total - tail0, sem.at[2], start)
            pad_rows(start)

    def tok(r, carry):
        for k in range(TOP_K):
            d = dest_ref[0, 0, r * TOP_K + k]
            pltpu.make_async_copy(h_ref.at[pl.ds(r, 1), :], xs_hbm.at[pl.ds(d, 1), :],
                                  sem.at[0]).start(priority=k % 2)
        return carry
    lax.fori_loop(0, tt, tok, 0, unroll=True)

    for k in range(TOP_K):
        pltpu.make_async_copy(h_ref, xs_hbm.at[pl.ds(0, tt), :], sem.at[0]).wait()


def _dispatch(h2, dest, g_start, cnt, g_end, n_rows_total):
    n, d = h2.shape
    tt = DISP_TOKENS
    dest3 = dest.reshape(n // tt, 1, tt * TOP_K)
    grid_spec = pltpu.PrefetchScalarGridSpec(
        num_scalar_prefetch=3,
        grid=(n // tt,),
        in_specs=[pl.BlockSpec((1, 1, tt * TOP_K), lambda j, gs, ct, ge: (j, 0, 0),
                               memory_space=pltpu.SMEM),
                  pl.BlockSpec((tt, d), lambda j, gs, ct, ge: (j, 0))],
        out_specs=pl.BlockSpec(memory_space=pl.ANY),
        scratch_shapes=[pltpu.VMEM((ZERO_ROWS, d), F32), pltpu.SemaphoreType.DMA((3,))],
    )
    return pl.pallas_call(
        _dispatch_kernel,
        grid_spec=grid_spec,
        out_shape=jax.ShapeDtypeStruct((n_rows_total, d), F32),
        compiler_params=_cparams(("arbitrary",)),
        name="dispatch",
    )(g_start, cnt, g_end, dest3, h2)


def _for_blocks(n_blk, n_max, pre, body):
    for n in range(4, n_max + 1):
        @pl.when(n_blk == n)
        def _(n=n):
            pre()
            for rb in range(n):
                body(rb, n)

    @pl.when(n_blk < 4)
    def _():
        pre()
        for rb in range(3):
            @pl.when(rb < n_blk)
            def _(rb=rb):
                body(rb, None)


def _expert_kernel(n_act, n_down, sbe_ref, sbr_ref, sbn_ref, ns_ref, gend_ref,
                   xs_hbm, wg_ref, wu_ref, wd_ref, bg_ref, bu_ref, bd_ref, ys_hbm,
                   stage, xbf, act, out, wgu_ref, wdb_ref, ld_sem, st_sem):
    s = pl.program_id(0)
    c = pl.program_id(1)
    th = EXP_TH
    n_super = ns_ref[0]
    n_blocks = SB_ROWS // EXP_TM

    def slot_ref(slot):
        return stage.at[pl.ds(slot * EXP_TM, EXP_TM), :]

    def x_copy(sb, rb, slot):
        r0 = pl.multiple_of(sbr_ref[sb] + rb * EXP_TM, ROW_ALIGN)
        return pltpu.make_async_copy(xs_hbm.at[pl.ds(r0, EXP_TM), :], slot_ref(slot), ld_sem.at[slot])

    def y_copy(sb, rb):
        r0 = pl.multiple_of(sbr_ref[sb] + rb * EXP_TM, ROW_ALIGN)
        return pltpu.make_async_copy(out.at[pl.ds(rb * EXP_TM, EXP_TM), :],
                                     ys_hbm.at[pl.ds(r0, EXP_TM), :], st_sem.at[rb])

    @pl.when(jnp.logical_and(s == 0, c == 0))
    def _():
        stage[...] = jnp.zeros(stage.shape, F32)
        tail0 = gend_ref[0]
        for start in (True, False):
            _zero_fill_rows(stage, ys_hbm, tail0, ys_hbm.shape[0] - tail0, st_sem.at[0], start)
        x_copy(0, 0, 0).start()

    @pl.when(s < n_super)
    def _():
        n_blk = sbn_ref[s]
        n_blk_prev = sbn_ref[jnp.maximum(s - 1, 0)]
        expert = sbe_ref[s]

        @pl.when(c < n_act)
        def _():
            cols = pl.ds(pl.multiple_of(c * th, th), th)
            bg = bg_ref[pl.ds(expert, 1), cols]
            bu = bu_ref[pl.ds(expert, 1), cols]

            def cast_weights():
                wgu_ref[:, :th] = wg_ref[0].astype(BF16)
                wgu_ref[:, th:] = wu_ref[0].astype(BF16)

            def act_block(rb, n):
                rows = pl.ds(rb * EXP_TM, EXP_TM)
                gu = jnp.dot(xbf[rows, :], wgu_ref[...], preferred_element_type=F32)
                g = jnp.minimum(gu[:, :th] + bg, SWIGLU_LIMIT)
                u = jnp.clip(gu[:, th:] + bu, -SWIGLU_LIMIT, SWIGLU_LIMIT)
                a = (u + 1.0) * g * jax.nn.sigmoid(SWIGLU_ALPHA * g)
                act[rows, cols] = a.astype(BF16)

            def first_block(rb, n):
                if n is None:
                    @pl.when(rb + 1 < n_blk)
                    def _():
                        x_copy(s, rb + 1, (rb + 1) % 2).start()
                elif rb + 1 < n:
                    x_copy(s, rb + 1, (rb + 1) % 2).start()
                x_copy(s, rb, rb % 2).wait()
                xbf[pl.ds(rb * EXP_TM, EXP_TM), :] = slot_ref(rb % 2)[...].astype(BF16)
                act_block(rb, n)

            @pl.when(c == 0)
            def _():
                _for_blocks(n_blk, n_blocks, cast_weights, first_block)

            @pl.when(c > 0)
            def _():
                _for_blocks(n_blk, n_blocks, cast_weights, act_block)

        @pl.when(c >= n_act)
        def _():
            cols = pl.ds(pl.multiple_of((c - n_act) * EXP_TN, EXP_TN), EXP_TN)
            bd = bd_ref[pl.ds(expert, 1), cols]
            for rb in range(n_blocks):
                @pl.when(jnp.logical_and(c == n_act, jnp.logical_and(s > 0, rb < n_blk_prev)))
                def _(rb=rb):
                    y_copy(s - 1, rb).wait()

            def cast_weights():
                wdb_ref[...] = wd_ref[0].astype(BF16)

            def down_block(rb, n):
                rows = pl.ds(rb * EXP_TM, EXP_TM)
                out[rows, cols] = jnp.dot(act[rows, :], wdb_ref[...],
                                          preferred_element_type=F32) + bd

            _for_blocks(n_blk, n_blocks, cast_weights, down_block)

            @pl.when(c == n_act + n_down - 1)
            def _():
                for rb in range(n_blocks):
                    @pl.when(rb < n_blk)
                    def _(rb=rb):
                        y_copy(s, rb).start()

                @pl.when(s + 1 < n_super)
                def _():
                    x_copy(s + 1, 0, 0).start()

                @pl.when(s + 1 == n_super)
                def _():
                    for rb in range(n_blocks):
                        @pl.when(rb < n_blk)
                        def _(rb=rb):
                            y_copy(s, rb).wait()


def _expert(xs, sb_expert, sb_row0, sb_nblk, n_super, g_end, wg, wu, wd, bg, bu, bd):
    n_rows_total, d = xs.shape
    s_max = sb_expert.shape[0]
    d_exp = wg.shape[2]
    assert d_exp % EXP_TH == 0 and d % EXP_TN == 0 and ZERO_ROWS == 2 * EXP_TM
    n_act = d_exp // EXP_TH
    n_down = d // EXP_TN
    ne = wg.shape[0]

    def e_eff(s, sbe, ns):
        return sbe[jnp.minimum(s, ns[0] - 1)]

    def c_act(s, c, ns):
        return jnp.where(s < ns[0], jnp.minimum(c, n_act - 1), n_act - 1)

    def c_down(s, c, ns):
        return jnp.where(s < ns[0], jnp.maximum(c - n_act, 0), n_down - 1)

    act_w = pl.BlockSpec((1, d, EXP_TH),
                         lambda s, c, sbe, sbr, sbn, ns, ge: (e_eff(s, sbe, ns), 0, c_act(s, c, ns)))
    bias = pl.BlockSpec((ne, d_exp), lambda s, c, sbe, sbr, sbn, ns, ge: (0, 0))
    grid_spec = pltpu.PrefetchScalarGridSpec(
        num_scalar_prefetch=5,
        grid=(s_max, n_act + n_down),
        in_specs=[
            pl.BlockSpec(memory_space=pl.ANY),
            act_w, act_w,
            pl.BlockSpec((1, d_exp, EXP_TN),
                         lambda s, c, sbe, sbr, sbn, ns, ge: (e_eff(s, sbe, ns), 0, c_down(s, c, ns))),
            bias, bias,
            pl.BlockSpec((ne, d), lambda s, c, sbe, sbr, sbn, ns, ge: (0, 0)),
        ],
        out_specs=pl.BlockSpec(memory_space=pl.ANY),
        scratch_shapes=[pltpu.VMEM((ZERO_ROWS, d), F32), pltpu.VMEM((SB_ROWS, d), BF16),
                        pltpu.VMEM((SB_ROWS, d_exp), BF16), pltpu.VMEM((SB_ROWS, d), F32),
                        pltpu.VMEM((d, 2 * EXP_TH), BF16), pltpu.VMEM((d_exp, EXP_TN), BF16),
                        pltpu.SemaphoreType.DMA((2,)),
                        pltpu.SemaphoreType.DMA((SB_ROWS // EXP_TM,))],
    )
    return pl.pallas_call(
        functools.partial(_expert_kernel, n_act, n_down),
        grid_spec=grid_spec,
        out_shape=jax.ShapeDtypeStruct((n_rows_total, d), F32),
        compiler_params=_cparams(("arbitrary", "arbitrary")),
        name="expert",
    )(sb_expert, sb_row0, sb_nblk, n_super, g_end, xs, wg, wu, wd, bg, bu, bd)


def _combine_start(y_hbm, pos_ref, stage_slot, sem):
    tt = stage_slot.shape[1]

    def body(r, carry):
        for k in range(TOP_K):
            p = pos_ref[0, 0, r * TOP_K + k]
            pltpu.make_async_copy(y_hbm.at[pl.ds(p, 1), :], stage_slot.at[k, pl.ds(r, 1), :],
                                  sem).start(priority=k % 2)
        return carry
    lax.fori_loop(0, tt, body, 0, unroll=True)


def _combine_kernel(final_norm, pos_cur_ref, pos_nxt_ref, y_hbm, prob_ref, x1_ref, mod_ref, g_ref,
                    o_ref, stage, sem):
    j = pl.program_id(0)
    nj = pl.num_programs(0)
    tt = stage.shape[2]

    @pl.when(j == 0)
    def _():
        _combine_start(y_hbm, pos_cur_ref, stage.at[0], sem.at[0])

    @pl.when(j + 1 < nj)
    def _():
        nxt = (j + 1) % 2
        _combine_start(y_hbm, pos_nxt_ref, stage.at[nxt], sem.at[nxt])

    cur = j % 2
    for k in range(TOP_K):
        pltpu.make_async_copy(y_hbm.at[pl.ds(0, tt), :], stage.at[cur, k], sem.at[cur]).wait()
    prob = prob_ref[...]
    y = prob[:, 0:1] * stage[cur, 0]
    for k in range(1, TOP_K):
        y = y + prob[:, k:k + 1] * stage[cur, k]
    x2 = x1_ref[...] + mod_ref[0, 5:6, :] * y
    if final_norm:
        r = lax.rsqrt(jnp.mean(x2 * x2, axis=-1, keepdims=True) + RMS_EPS)
        x2 = (x2 * r) * g_ref[...]
    o_ref[...] = x2


def _combine(yb, pos, probs, x1, mod3, g, seq, final_norm):
    n, d = x1.shape
    tt = COMB_ROWS
    nt = n // tt
    tiles_per_seq = seq // tt
    pos3 = pos.reshape(nt, 1, tt * TOP_K)
    return pl.pallas_call(
        functools.partial(_combine_kernel, final_norm),
        grid=(nt,),
        in_specs=[pl.BlockSpec((1, 1, tt * TOP_K), lambda j: (j, 0, 0), memory_space=pltpu.SMEM),
                  pl.BlockSpec((1, 1, tt * TOP_K), lambda j: (jnp.minimum(j + 1, nt - 1), 0, 0),
                               memory_space=pltpu.SMEM),
                  pl.BlockSpec(memory_space=pl.ANY),
                  pl.BlockSpec((tt, LANES), lambda j: (j, 0)),
                  pl.BlockSpec((tt, d), lambda j: (j, 0)),
                  pl.BlockSpec((1, 6, d), lambda j: (j // tiles_per_seq, 0, 0)),
                  pl.BlockSpec((1, d), lambda j: (0, 0))],
        out_specs=pl.BlockSpec((tt, d), lambda j: (j, 0)),
        out_shape=jax.ShapeDtypeStruct((n, d), F32),
        scratch_shapes=[pltpu.VMEM((2, TOP_K, tt, d), F32), pltpu.SemaphoreType.DMA((2,))],
        compiler_params=_cparams(("arbitrary",)),
        name="combine",
    )(pos3, pos3, yb, probs, x1, mod3, g.reshape(1, d))


def _count_le(sorted_ends, q):
    return jnp.sum((sorted_ends[None, :] <= q[:, None]).astype(jnp.int32), axis=1)


def _plan(counts, n_tok):
    s_max = -(-n_tok * TOP_K // SB_ROWS) + N_EXPERTS
    cnt = counts[0, :N_EXPERTS]
    cnt_al = (cnt + ROW_ALIGN - 1) // ROW_ALIGN * ROW_ALIGN
    g_end = jnp.cumsum(cnt_al)
    g_start = g_end - cnt_al
    nsb_e = (cnt + SB_ROWS - 1) // SB_ROWS
    sb_end = jnp.cumsum(nsb_e)
    sb_start = sb_end - nsb_e
    n_super = sb_end[-1]
    s_ids = jnp.arange(s_max, dtype=jnp.int32)
    sb_expert = jnp.minimum(_count_le(sb_end, s_ids), N_EXPERTS - 1)
    j_in = s_ids - sb_start[sb_expert]
    live = s_ids < n_super
    sb_rows = jnp.where(live, jnp.clip(cnt[sb_expert] - j_in * SB_ROWS, 0, SB_ROWS), 0)
    sb_row0 = jnp.where(live, g_start[sb_expert] + j_in * SB_ROWS, 0)
    sb_nblk = (sb_rows + EXP_TM - 1) // EXP_TM
    i32 = lambda a: a.astype(jnp.int32)
    return (i32(g_start), i32(cnt), i32(g_end[-1:]), i32(sb_expert), i32(sb_row0), i32(sb_nblk),
            i32(n_super.reshape(1)))


def _dest_kernel(ir_ref, gs_ref, o_ref):
    ir = ir_ref[...].astype(F32)
    lane = lax.broadcasted_iota(jnp.int32, ir.shape, 1)
    lane_f = lane.astype(F32)
    out = jnp.zeros(ir.shape, F32)
    for k in range(TOP_K):
        e = jnp.sum(jnp.where(lane == k, ir, 0.0), axis=1, keepdims=True)
        r = jnp.sum(jnp.where(lane == TOP_K + k, ir, 0.0), axis=1, keepdims=True)
        g = jnp.sum(jnp.where(lane_f == e, gs_ref[...], 0.0), axis=1, keepdims=True)
        out = jnp.where(lane == k, g + r, out)
    o_ref[...] = out.astype(jnp.int32)


def _dest(idx_rank, g_start):
    n = idx_rank.shape[0]
    tt = 1024
    gs = jnp.pad(g_start.astype(F32), (0, LANES - N_EXPERTS)).reshape(1, LANES)
    return pl.pallas_call(
        _dest_kernel,
        grid=(n // tt,),
        in_specs=[pl.BlockSpec((tt, LANES), lambda i: (i, 0)),
                  pl.BlockSpec((1, LANES), lambda i: (0, 0))],
        out_specs=pl.BlockSpec((tt, LANES), lambda i: (i, 0)),
        out_shape=jax.ShapeDtypeStruct((n, LANES), jnp.int32),
        compiler_params=_cparams(("arbitrary",)),
        name="dest",
    )(idx_rank, gs)


def kernel(x, c, ada_w, ada_b, norm_mix_g, w_in, b_forget, conv_w, w_conv_out, w_attn_out, w_out,
           norm_ffn_g, router_w, router_b, exp_w_gate, exp_b_gate, exp_w_up, exp_b_up,
           exp_w_down, exp_b_down, final_norm_g):
    bsz, seq, d = x.shape
    n = bsz * seq
    d_conv = conv_w.shape[2]
    d_attn = w_attn_out.shape[1]
    n_main = 3 * d_conv + 3 * d_attn
    x2 = x.reshape(n, d)
    out = x2
    for l in range(ada_w.shape[0]):
        mod3 = _mod(c, ada_w[l], ada_b[l]).reshape(bsz, 6, d)
        proj, gates, f_logit = _inproj(out, mod3, norm_mix_g[l], w_in[l].T, n_main, N_HEADS,
                                       w_in.shape[2] - n_main - N_HEADS, seq,
                                       (3 * d_conv, 3 * d_conv + d_attn))
        fcol = _cumf(f_logit, b_forget[l], seq)
        att = _attn(proj, fcol, bsz, seq, 3 * d_conv, 3 * d_conv + d_attn, 3 * d_conv + 2 * d_attn)
        rw = jnp.pad(router_w[l], ((0, 0), (0, LANES - N_EXPERTS)))
        rb = jnp.pad(router_b[l], (0, LANES - N_EXPERTS)).reshape(1, LANES)
        x1, h2, logits = _mix(proj, gates, att, out, mod3, norm_ffn_g[l], conv_w[l],
                              _cast_bf16(w_conv_out[l]), _cast_bf16(w_attn_out[l]), _cast_bf16(w_out[l]),
                              rw, rb, seq, d_conv)
        idx_rank, probs, counts = _route(logits)
        g_start, cnt, g_end, sb_expert, sb_row0, sb_nblk, n_super = _plan(counts, n)
        dest = _dest(idx_rank, g_start)[:, :TOP_K].reshape(-1)
        n_rows_total = n * TOP_K + N_EXPERTS * ROW_ALIGN + EXP_TM
        xs = _dispatch(h2, dest, g_start, cnt, g_end, n_rows_total)
        ys = _expert(xs, sb_expert, sb_row0, sb_nblk, n_super, g_end, exp_w_gate[l], exp_w_up[l],
                     exp_w_down[l], exp_b_gate[l], exp_b_up[l], exp_b_down[l])
        out = _combine(ys, dest, probs, x1, mod3, final_norm_g, seq,
                       final_norm=(l == ada_w.shape[0] - 1))
    return out.reshape(bsz, seq, d)
```

```python
import functools

import jax
import jax.numpy as jnp
from jax import lax
from jax.experimental import pallas as pl
from jax.experimental.pallas import tpu as pltpu

F32 = jnp.float32
BF16 = jnp.bfloat16

N_HEADS = 16
HEAD_DIM = 64
N_EXPERTS = 32
TOP_K = 4
SWIGLU_LIMIT = 7.0
SWIGLU_ALPHA = 1.702
RMS_EPS = 1e-6
CONV_WIDTH = 3

LOG2E = 1.4426950408889634
QK_SCALE = HEAD_DIM ** -0.5 * LOG2E

LANES = 128
VMEM_LIMIT = 58 * 1024 * 1024

ROW_ALIGN = 8

EXP_TM = 272
SB_ROWS = 6 * EXP_TM
EXP_TH = 256
EXP_TN = 512
ZERO_ROWS = 2 * EXP_TM
DISP_TOKENS = 256
COMB_ROWS = 128


def _cparams(sem, vmem=VMEM_LIMIT):
    return pltpu.CompilerParams(dimension_semantics=sem, vmem_limit_bytes=vmem)


def _mod_kernel(cb_ref, w_ref, b_ref, o_ref):
    w = w_ref[...]
    tn = w.shape[1]
    rows = []
    for b in range(cb_ref.shape[0]):
        cv = cb_ref[b]
        ca = cv * jax.nn.sigmoid(cv)
        cols = [jnp.sum(w[:, j * LANES:(j + 1) * LANES] * ca, axis=0, keepdims=True)
                for j in range(tn // LANES)]
        rows.append(jnp.concatenate(cols, axis=1))
    o_ref[...] = jnp.concatenate(rows, axis=0) + b_ref[...]


def _mod(c, w, bias):
    bsz, d = c.shape
    n_out = w.shape[1]
    tn = 1024
    cb = jnp.broadcast_to(c[:, :, None], (bsz, d, LANES))
    return pl.pallas_call(
        _mod_kernel,
        grid=(n_out // tn,),
        in_specs=[pl.BlockSpec((bsz, d, LANES), lambda j: (0, 0, 0)),
                  pl.BlockSpec((d, tn), lambda j: (0, j)),
                  pl.BlockSpec((1, tn), lambda j: (0, j))],
        out_specs=pl.BlockSpec((bsz, tn), lambda j: (0, j)),
        out_shape=jax.ShapeDtypeStruct((bsz, n_out), F32),
        compiler_params=_cparams(("arbitrary",)),
        name="mod",
    )(cb, w, bias.reshape(1, n_out))


def _cast_kernel(x_ref, o_ref):
    o_ref[...] = x_ref[...].astype(o_ref.dtype)


def _cast_bf16(w):
    r, c = w.shape
    tr = 512
    return pl.pallas_call(
        _cast_kernel,
        grid=(r // tr,),
        in_specs=[pl.BlockSpec((tr, c), lambda i: (i, 0))],
        out_specs=pl.BlockSpec((tr, c), lambda i: (i, 0)),
        out_shape=jax.ShapeDtypeStruct((r, c), BF16),
        compiler_params=_cparams(("arbitrary",)),
        name="cast_bf16",
    )(w)


def _rmsnorm_mod(x, g, scale, shift):
    r = lax.rsqrt(jnp.mean(x * x, axis=-1, keepdims=True) + RMS_EPS)
    return (x * r) * g * (1.0 + scale) + shift


def _inproj_kernel(nj_main, jq, x_ref, mod_ref, g_ref, w_ref, wf_ref,
                   om_ref, og_ref, of_ref, h_ref, wfp_ref, wb_ref):
    j = pl.program_id(1)

    @pl.when(j == 0)
    def _():
        wfp_ref[...] = jnp.zeros(wfp_ref.shape, BF16)
        wfp_ref[0:wf_ref.shape[0], :] = wf_ref[...].astype(BF16)
        wb_ref[...] = w_ref[...].astype(BF16)
        rows = 256
        assert jq[0] > 0

        for rb in range(x_ref.shape[0] // rows):
            r = pl.ds(rb * rows, rows)
            h = _rmsnorm_mod(x_ref[r, :], g_ref[...], mod_ref[0, 1:2, :], mod_ref[0, 0:1, :])
            hb = h.astype(BF16)
            h_ref[r, :] = hb
            of_ref[r, :] = _dot_nt(hb, wfp_ref[...])
            om_ref[r, :] = _dot_nt(hb, wb_ref[...]).astype(om_ref.dtype)

    @pl.when(jnp.logical_and(j > 0, j < nj_main))
    def _():
        scale = jnp.where(jnp.logical_and(j >= jq[0], j < jq[1]), QK_SCALE, 1.0)
        om_ref[...] = (_dot_nt(h_ref[...], w_ref[...].astype(BF16)) * scale).astype(om_ref.dtype)

    @pl.when(j >= nj_main)
    def _():
        og_ref[...] = _dot_nt(h_ref[...], w_ref[...].astype(BF16)).astype(og_ref.dtype)


def _inproj(x2, mod3, g, w_in_t, n_main, n_f, n_gates, seq, q_cols):
    n, d = x2.shape
    tm, tn = 1024, 1024
    nj_main = n_main // tn
    assert n_main % tn == 0 and n_gates % tn == 0 and q_cols[0] % tn == 0 and q_cols[1] % tn == 0
    jq = (q_cols[0] // tn, q_cols[1] // tn)
    nj_g = n_gates // tn
    gate_row0 = n_main + n_f
    tiles_per_seq = seq // tm
    return pl.pallas_call(
        functools.partial(_inproj_kernel, nj_main, jq),
        grid=(n // tm, nj_main + nj_g),
        in_specs=[pl.BlockSpec((tm, d), lambda i, j: (i, 0)),
                  pl.BlockSpec((1, 6, d), lambda i, j: (i // tiles_per_seq, 0, 0)),
                  pl.BlockSpec((1, d), lambda i, j: (0, 0)),
                  pl.BlockSpec((pl.Element(tn), pl.Element(d)),
                               lambda i, j: (pl.multiple_of(
                                   jnp.where(j < nj_main, tn * j, gate_row0 + tn * (j - nj_main)), n_f), 0)),
                  pl.BlockSpec((n_f, d), lambda i, j: (n_main // n_f, 0))],
        out_specs=[pl.BlockSpec((tm, tn), lambda i, j: (i, jnp.minimum(j, nj_main - 1))),
                   pl.BlockSpec((tm, tn), lambda i, j: (i, jnp.maximum(j - nj_main, 0))),
                   pl.BlockSpec((tm, LANES), lambda i, j: (i, 0))],
        out_shape=[jax.ShapeDtypeStruct((n, n_main), BF16),
                   jax.ShapeDtypeStruct((n, n_gates), BF16),
                   jax.ShapeDtypeStruct((n, LANES), F32)],
        scratch_shapes=[pltpu.VMEM((tm, d), BF16), pltpu.VMEM((LANES, d), BF16),
                        pltpu.VMEM((tn, d), BF16)],
        compiler_params=_cparams(("arbitrary", "arbitrary")),
        name="inproj",
    )(x2, mod3, g.reshape(1, d), w_in_t, w_in_t)


def _cumf_kernel(blocks_per_seq, f_ref, bf_ref, fc_ref, carry_ref):
    i = pl.program_id(0)

    @pl.when(i % blocks_per_seq == 0)
    def _():
        carry_ref[...] = jnp.zeros_like(carry_ref)

    z = f_ref[...] + bf_ref[...]
    lf = -(jnp.maximum(-z, 0.0) + jnp.log1p(jnp.exp(-jnp.abs(z))))
    t = z.shape[0]
    row = lax.broadcasted_iota(jnp.int32, (t, t), 0)
    col = lax.broadcasted_iota(jnp.int32, (t, t), 1)
    tri = jnp.where(col <= row, 1.0, 0.0).astype(BF16)
    parts = [jnp.dot(tri, p, preferred_element_type=F32) for p in _split3(lf)]
    fb = (parts[0] + (parts[1] + parts[2])) + carry_ref[...]
    fc_ref[...] = fb
    carry_ref[...] = fb[t - 1:t, :]


def _cumf(f_logit, b_forget, seq):
    n = f_logit.shape[0]
    t = 512
    bf = jnp.pad(b_forget, (0, LANES - b_forget.shape[0])).reshape(1, LANES)
    return pl.pallas_call(
        functools.partial(_cumf_kernel, seq // t),
        grid=(n // t,),
        in_specs=[pl.BlockSpec((t, LANES), lambda i: (i, 0)),
                  pl.BlockSpec((1, LANES), lambda i: (0, 0))],
        out_specs=pl.BlockSpec((t, LANES), lambda i: (i, 0)),
        out_shape=jax.ShapeDtypeStruct((n, LANES), F32),
        scratch_shapes=[pltpu.VMEM((1, LANES), F32)],
        compiler_params=_cparams(("arbitrary",)),
        name="cumf",
    )(f_logit, bf)


def _dot_nt(a, b):
    return lax.dot_general(a, b, (((1,), (1,)), ((), ())), preferred_element_type=F32)


def _split3(x):
    hi = x.astype(BF16)
    r1 = x - hi.astype(F32)
    mid = r1.astype(BF16)
    lo = (r1 - mid.astype(F32)).astype(BF16)
    return hi, mid, lo


def _bias_lanes(parts, head, lane0):
    r = lax.broadcasted_iota(jnp.int32, (LANES, LANES), 0)
    c = lax.broadcasted_iota(jnp.int32, (LANES, LANES), 1)
    out = None
    for j, part in enumerate(parts):
        sel = jnp.where(r == head, jnp.where(c == lane0 + j, -1.0, jnp.where(c == lane0 + 3 + j, 1.0, 0.0)),
                        0.0).astype(BF16)
        term = jnp.dot(part, sel, preferred_element_type=F32)
        out = term if out is None else out + term
    return out


def _attn_kernel(tq, tk, q_ref, k_ref, v_ref, fc_ref, o_ref, kp_ref, qp_ref, vt_ref, m_ref, l_ref,
                 acc_ref, sa_ref, sb_ref):
    hp = pl.program_id(1)
    qi = pl.program_id(2)
    seq = k_ref.shape[0]
    lane = lax.broadcasted_iota(jnp.int32, (1, LANES), 1)
    own = (lane < HEAD_DIM, lane >= HEAD_DIM)
    spare = (HEAD_DIM, 0)
    ones_k = tuple(jnp.logical_and(lane >= spare[h] + 3, lane < spare[h] + 6) for h in range(2))
    ones_q = tuple(jnp.logical_and(lane >= spare[h], lane < spare[h] + 3) for h in range(2))

    @pl.when(qi == 0)
    def _():
        for ci in range(seq // tk):
            r0 = ci * tk
            f = fc_ref[pl.ds(r0, tk), :] * LOG2E
            k = k_ref[pl.ds(r0, tk), :]
            q = q_ref[pl.ds(r0, tk), :]
            parts = _split3(f)
            for h in range(2):
                both = _bias_lanes(parts, 2 * hp + h, spare[h])
                kp_ref[h, pl.ds(r0, tk), :] = jnp.where(own[h], k, jnp.where(ones_k[h], 1.0, both).astype(BF16))
                qp_ref[h, pl.ds(r0, tk), :] = jnp.where(own[h], q, jnp.where(ones_q[h], 1.0, both).astype(BF16))
            vt_ref[:, pl.ds(r0, tk)] = v_ref[pl.ds(r0, tk), :].astype(F32).T.astype(BF16)

    q0 = pl.multiple_of(qi * tq, tq)
    qp = [qp_ref[h, pl.ds(q0, tq), :] for h in range(2)]

    m_ref[...] = jnp.full(m_ref.shape, -jnp.inf, F32)
    l_ref[...] = jnp.zeros(l_ref.shape, F32)
    acc_ref[...] = jnp.zeros(acc_ref.shape, F32)

    def scores(kt, s_ref):
        k0 = kt * tk
        for h in range(2):
            s_ref[h] = _dot_nt(kp_ref[h, pl.ds(k0, tk), :], qp[h])

    def update(kt, s_ref, masked):
        k0 = kt * tk
        for h in range(2):
            s = s_ref[h]
            if masked:
                kr = lax.broadcasted_iota(jnp.int32, (tk, tq), 0)
                qc = lax.broadcasted_iota(jnp.int32, (tk, tq), 1)
                s = jnp.where(kr <= qc, s, -jnp.inf)
            m_old = m_ref[h]
            m_new = jnp.maximum(m_old, jnp.max(s, axis=0, keepdims=True))
            alpha = jnp.exp2(m_old - m_new)
            p = jnp.exp2(s - m_new)
            l_ref[h] = l_ref[h] * alpha + jnp.sum(p, axis=0, keepdims=True)
            m_ref[h] = m_new
            rows = pl.ds(h * HEAD_DIM, HEAD_DIM)
            pv = jnp.dot(vt_ref[rows, pl.ds(k0, tk)], p.astype(BF16), preferred_element_type=F32)
            acc_ref[rows, :] = acc_ref[rows, :] * alpha + pv

    bufs = (sa_ref, sb_ref)
    for k in range(seq // tq):
        @pl.when(qi == k)
        def _(k=k):
            scores(0, bufs[0])
            for t in range(k):
                scores(t + 1, bufs[(t + 1) % 2])
                update(t, bufs[t % 2], False)
            update(k, bufs[k % 2], True)

    out_t = jnp.concatenate([acc_ref[pl.ds(h * HEAD_DIM, HEAD_DIM), :] / l_ref[h] for h in range(2)],
                            axis=0)
    o_ref[...] = out_t.T.astype(o_ref.dtype)


def _attn(proj, fcol, bsz, seq, q_col0, k_col0, v_col0):
    tq = tk = 512
    nq = seq // tq
    n_pairs = N_HEADS * HEAD_DIM // LANES
    qb, kb, vb = q_col0 // LANES, k_col0 // LANES, v_col0 // LANES
    return pl.pallas_call(
        functools.partial(_attn_kernel, tq, tk),
        grid=(bsz, n_pairs, nq),
        in_specs=[pl.BlockSpec((seq, LANES), lambda b, hp, qi: (b, qb + hp)),
                  pl.BlockSpec((seq, LANES), lambda b, hp, qi: (b, kb + hp)),
                  pl.BlockSpec((seq, LANES), lambda b, hp, qi: (b, vb + hp)),
                  pl.BlockSpec((seq, LANES), lambda b, hp, qi: (b, 0))],
        out_specs=pl.BlockSpec((tq, LANES), lambda b, hp, qi: (b * nq + qi, hp)),
        out_shape=jax.ShapeDtypeStruct((bsz * seq, N_HEADS * HEAD_DIM), BF16),
        scratch_shapes=[pltpu.VMEM((2, seq, LANES), BF16), pltpu.VMEM((2, seq, LANES), BF16),
                        pltpu.VMEM((LANES, seq), BF16),
                        pltpu.VMEM((2, 1, tq), F32), pltpu.VMEM((2, 1, tq), F32),
                        pltpu.VMEM((LANES, tq), F32),
                        pltpu.VMEM((2, tk, tq), F32), pltpu.VMEM((2, tk, tq), F32)],
        compiler_params=_cparams(("arbitrary", "arbitrary", "arbitrary")),
        name="attn",
    )(proj, proj, proj, fcol)


def _mix_kernel(tiles_per_seq, xc_ref, cb_ref, cc_ref, hxc_ref, hcc_ref, att_ref, gc_ref, ga_ref,
                x_ref, mod_ref, g2_ref, cw_ref, wc_ref, wa_ref, wo_ref, rw_ref, rb_ref,
                x1_ref, h2_ref, lg_ref):
    i = pl.program_id(0)
    u = cc_ref[...].astype(F32) * xc_ref[...].astype(F32)
    hu = hcc_ref[...].astype(F32) * hxc_ref[...].astype(F32)
    hu = jnp.where(i % tiles_per_seq == 0, 0.0, hu)
    nh = hu.shape[0]
    row = lax.broadcasted_iota(jnp.int32, u.shape, 0)
    u1 = jnp.where(row == 0, hu[nh - 1:nh, :], pltpu.roll(u, 1, axis=0))
    u2 = jnp.where(row == 0, hu[nh - 2:nh - 1, :],
                   jnp.where(row == 1, hu[nh - 1:nh, :], pltpu.roll(u, 2, axis=0)))
    conv = cw_ref[0:1, :] * u2 + cw_ref[1:2, :] * u1 + cw_ref[2:3, :] * u
    z = (cb_ref[...].astype(F32) * conv).astype(BF16)
    y_conv = jnp.dot(z, wc_ref[...], preferred_element_type=F32)
    y_attn = jnp.dot(att_ref[...], wa_ref[...], preferred_element_type=F32)
    merged = (jax.nn.sigmoid(gc_ref[...].astype(F32)) * y_conv
              + jax.nn.sigmoid(ga_ref[...].astype(F32)) * y_attn)
    o = jnp.dot(merged.astype(BF16), wo_ref[...], preferred_element_type=F32)
    x1 = x_ref[...] + mod_ref[0, 2:3, :] * o
    x1_ref[...] = x1
    h2 = _rmsnorm_mod(x1, g2_ref[...], mod_ref[0, 4:5, :], mod_ref[0, 3:4, :])
    h2_ref[...] = h2
    h_hi, h_lo, _ = _split3(h2)
    w_hi, w_lo, _ = _split3(rw_ref[...])
    lg_ref[...] = (jnp.dot(h_hi, w_hi, preferred_element_type=F32)
                   + (jnp.dot(h_hi, w_lo, preferred_element_type=F32)
                      + jnp.dot(h_lo, w_hi, preferred_element_type=F32))) + rb_ref[...]


def _mix(proj, gates, att, x2, mod3, g2, conv_w, wc, wa, wo, rw, rb, seq, d_conv):
    n, d = x2.shape
    tm = 256
    halo = 16
    tiles_per_seq = seq // tm
    d_attn = att.shape[1]

    def resident(shape):
        return pl.BlockSpec(shape, lambda i: (0,) * len(shape), pipeline_mode=pl.Buffered(1))

    return pl.pallas_call(
        functools.partial(_mix_kernel, tiles_per_seq),
        grid=(n // tm,),
        in_specs=[pl.BlockSpec((tm, d_conv), lambda i: (i, 0)),
                  pl.BlockSpec((tm, d_conv), lambda i: (i, 1)),
                  pl.BlockSpec((tm, d_conv), lambda i: (i, 2)),
                  pl.BlockSpec((halo, d_conv), lambda i: (jnp.maximum(i * (tm // halo) - 1, 0), 0)),
                  pl.BlockSpec((halo, d_conv), lambda i: (jnp.maximum(i * (tm // halo) - 1, 0), 2)),
                  pl.BlockSpec((tm, d_attn), lambda i: (i, 0)),
                  pl.BlockSpec((tm, d), lambda i: (i, 0)),
                  pl.BlockSpec((tm, d), lambda i: (i, 1)),
                  pl.BlockSpec((tm, d), lambda i: (i, 0)),
                  pl.BlockSpec((1, 6, d), lambda i: (i // tiles_per_seq, 0, 0)),
                  resident((1, d)),
                  resident((CONV_WIDTH, d_conv)),
                  resident((d_conv, d)),
                  resident((d_attn, d)),
                  resident((d, d)),
                  resident((d, LANES)),
                  resident((1, LANES))],
        out_specs=[pl.BlockSpec((tm, d), lambda i: (i, 0)),
                   pl.BlockSpec((tm, d), lambda i: (i, 0)),
                   pl.BlockSpec((tm, LANES), lambda i: (i, 0))],
        out_shape=[jax.ShapeDtypeStruct((n, d), F32),
                   jax.ShapeDtypeStruct((n, d), F32),
                   jax.ShapeDtypeStruct((n, LANES), F32)],
        compiler_params=_cparams(("arbitrary",)),
        name="mix",
    )(proj, proj, proj, proj, proj, att, gates, gates, x2, mod3, g2.reshape(1, d), conv_w,
      wc, wa, wo, rw, rb)


def _route_kernel(lg_ref, idx_ref, prob_ref, cnt_ref, carry_ref):
    i = pl.program_id(0)

    @pl.when(i == 0)
    def _():
        carry_ref[...] = jnp.zeros_like(carry_ref)

    tt = lg_ref.shape[0]
    lane = lax.broadcasted_iota(jnp.int32, (tt, LANES), 1)
    lane_f = lane.astype(F32)
    l = jnp.where(lane < N_EXPERTS, lg_ref[...], -jnp.inf)
    onehot = jnp.zeros((tt, LANES), F32)
    vals, ids = [], []
    for _ in range(TOP_K):
        m = jnp.max(l, axis=1, keepdims=True)
        idx = jnp.min(jnp.where(l == m, lane_f, float(LANES)), axis=1, keepdims=True)
        sel = lane_f == idx
        vals.append(m)
        ids.append(idx)
        onehot = jnp.where(sel, 1.0, onehot)
        l = jnp.where(sel, -jnp.inf, l)
    es = [jnp.exp(v - vals[0]) for v in vals]
    denom = es[0] + es[1] + es[2] + es[3]
    r = lax.broadcasted_iota(jnp.int32, (tt, tt), 0)
    c = lax.broadcasted_iota(jnp.int32, (tt, tt), 1)
    before = jnp.where(c < r, 1.0, 0.0).astype(BF16)
    cnt_before = jnp.dot(before, onehot.astype(BF16), preferred_element_type=F32) + carry_ref[...]
    idx_out = jnp.zeros((tt, LANES), jnp.int32)
    prob_out = jnp.zeros((tt, LANES), F32)
    for k in range(TOP_K):
        rank = jnp.sum(jnp.where(lane_f == ids[k], cnt_before, 0.0), axis=1, keepdims=True)
        idx_out = jnp.where(lane == k, ids[k].astype(jnp.int32), idx_out)
        idx_out = jnp.where(lane == TOP_K + k, rank.astype(jnp.int32), idx_out)
        prob_out = jnp.where(lane == k, es[k] / denom, prob_out)
    idx_ref[...] = idx_out
    prob_ref[...] = prob_out
    carry_ref[...] = carry_ref[...] + jnp.sum(onehot, axis=0, keepdims=True)
    cnt_ref[...] = carry_ref[...].astype(jnp.int32)


def _route(logits):
    n = logits.shape[0]
    tt = 512
    return pl.pallas_call(
        _route_kernel,
        grid=(n // tt,),
        in_specs=[pl.BlockSpec((tt, LANES), lambda i: (i, 0))],
        out_specs=[pl.BlockSpec((tt, LANES), lambda i: (i, 0)),
                   pl.BlockSpec((tt, LANES), lambda i: (i, 0)),
                   pl.BlockSpec((1, LANES), lambda i: (0, 0))],
        out_shape=[jax.ShapeDtypeStruct((n, LANES), jnp.int32),
                   jax.ShapeDtypeStruct((n, LANES), F32),
                   jax.ShapeDtypeStruct((1, LANES), jnp.int32)],
        scratch_shapes=[pltpu.VMEM((1, LANES), F32)],
        compiler_params=_cparams(("arbitrary",)),
        name="route",
    )(logits)


def _zero_fill_rows(zero_rows, dst_hbm, row0, n_rows, sem, start):
    off = row0
    size = 1 << (zero_rows.shape[0].bit_length() - 1)
    while size >= ROW_ALIGN:
        @pl.when((n_rows & size) != 0)
        def _(off=off, size=size):
            cp = pltpu.make_async_copy(zero_rows.at[pl.ds(0, size), :],
                                       dst_hbm.at[pl.ds(pl.multiple_of(off, ROW_ALIGN), size), :], sem)
            if start:
                cp.start()
            else:
                cp.wait()
        off = off + (n_rows & size)
        size //= 2


def _dispatch_kernel(gs_ref, cnt_ref, gend_ref, dest_ref, h_ref, xs_hbm, zbuf, sem):
    j = pl.program_id(0)
    tt = h_ref.shape[0]
    n_total = xs_hbm.shape[0]

    @pl.when(j == 0)
    def _():
        zbuf[...] = jnp.zeros(zbuf.shape, F32)
        tail0 = gend_ref[0]

        def pad_rows(start):
            def group(e, carry):
                cnt = cnt_ref[e]

                def row(r, c2):
                    cp = pltpu.make_async_copy(zbuf.at[pl.ds(0, 1), :],
                                               xs_hbm.at[pl.ds(gs_ref[e] + r, 1), :], sem.at[2])
                    if start:
                        cp.start()
                    else:
                        cp.wait()
                    return c2
                lax.fori_loop(cnt, (cnt + ROW_ALIGN - 1) & (-ROW_ALIGN), row, 0)
                return carry
            lax.fori_loop(0, N_EXPERTS, group, 0)

        for start in (True, False):
            _zero_fill_rows(zbuf, xs_hbm, tail0, n_total - tail0, sem.at[2], start)
            pad_rows(start)

    def tok(r, carry):
        for k in range(TOP_K):
            d = dest_ref[0, 0, r * TOP_K + k]
            pltpu.make_async_copy(h_ref.at[pl.ds(r, 1), :], xs_hbm.at[pl.ds(d, 1), :],
                                  sem.at[0]).start(priority=k % 2)
        return carry
    lax.fori_loop(0, tt, tok, 0, unroll=True)

    for k in range(TOP_K):
        pltpu.make_async_copy(h_ref, xs_hbm.at[pl.ds(0, tt), :], sem.at[0]).wait()


def _dispatch(h2, dest, g_start, cnt, g_end, n_rows_total):
    n, d = h2.shape
    tt = DISP_TOKENS
    dest3 = dest.reshape(n // tt, 1, tt * TOP_K)
    grid_spec = pltpu.PrefetchScalarGridSpec(
        num_scalar_prefetch=3,
        grid=(n // tt,),
        in_specs=[pl.BlockSpec((1, 1, tt * TOP_K), lambda j, gs, ct, ge: (j, 0, 0),
                               memory_space=pltpu.SMEM),
                  pl.BlockSpec((tt, d), lambda j, gs, ct, ge: (j, 0))],
        out_specs=pl.BlockSpec(memory_space=pl.ANY),
        scratch_shapes=[pltpu.VMEM((ZERO_ROWS, d), F32), pltpu.SemaphoreType.DMA((3,))],
    )
    return pl.pallas_call(
        _dispatch_kernel,
        grid_spec=grid_spec,
        out_shape=jax.ShapeDtypeStruct((n_rows_total, d), F32),
        compiler_params=_cparams(("arbitrary",)),
        name="dispatch",
    )(g_start, cnt, g_end, dest3, h2)


def _for_blocks(n_blk, n_max, pre, body):
    for n in range(4, n_max + 1):
        @pl.when(n_blk == n)
        def _(n=n):
            pre()
            for rb in range(n):
                body(rb, n)

    @pl.when(n_blk < 4)
    def _():
        pre()
        for rb in range(3):
            @pl.when(rb < n_blk)
            def _(rb=rb):
                body(rb, None)


def _expert_kernel(n_act, n_down, sbe_ref, sbr_ref, sbn_ref, ns_ref, gend_ref,
                   xs_hbm, wg_ref, wu_ref, wd_ref, bg_ref, bu_ref, bd_ref, ys_hbm,
                   stage, xbf, act, out, wgu_ref, wdb_ref, ld_sem, st_sem):
    s = pl.program_id(0)
    c = pl.program_id(1)
    th = EXP_TH
    n_super = ns_ref[0]
    n_blocks = SB_ROWS // EXP_TM

    def slot_ref(slot):
        return stage.at[pl.ds(slot * EXP_TM, EXP_TM), :]

    def x_copy(sb, rb, slot):
        r0 = pl.multiple_of(sbr_ref[sb] + rb * EXP_TM, ROW_ALIGN)
        return pltpu.make_async_copy(xs_hbm.at[pl.ds(r0, EXP_TM), :], slot_ref(slot), ld_sem.at[slot])

    def y_copy(sb, rb):
        r0 = pl.multiple_of(sbr_ref[sb] + rb * EXP_TM, ROW_ALIGN)
        return pltpu.make_async_copy(out.at[pl.ds(rb * EXP_TM, EXP_TM), :],
                                     ys_hbm.at[pl.ds(r0, EXP_TM), :], st_sem.at[rb])

    @pl.when(jnp.logical_and(s == 0, c == 0))
    def _():
        stage[...] = jnp.zeros(stage.shape, F32)
        tail0 = gend_ref[0]
        for start in (True, False):
            _zero_fill_rows(stage, ys_hbm, tail0, ys_hbm.shape[0] - tail0, st_sem.at[0], start)
        x_copy(0, 0, 0).start()

    @pl.when(s < n_super)
    def _():
        n_blk = sbn_ref[s]
        n_blk_prev = sbn_ref[jnp.maximum(s - 1, 0)]
        expert = sbe_ref[s]

        @pl.when(c < n_act)
        def _():
            cols = pl.ds(pl.multiple_of(c * th, th), th)
            bg = bg_ref[pl.ds(expert, 1), cols]
            bu = bu_ref[pl.ds(expert, 1), cols]

            def cast_weights():
                wgu_ref[:, :th] = wg_ref[0].astype(BF16)
                wgu_ref[:, th:] = wu_ref[0].astype(BF16)

            def act_block(rb, n):
                rows = pl.ds(rb * EXP_TM, EXP_TM)
                gu = jnp.dot(xbf[rows, :], wgu_ref[...], preferred_element_type=F32)
                g = jnp.minimum(gu[:, :th] + bg, SWIGLU_LIMIT)
                u = jnp.clip(gu[:, th:] + bu, -SWIGLU_LIMIT, SWIGLU_LIMIT)
                a = (u + 1.0) * g * jax.nn.sigmoid(SWIGLU_ALPHA * g)
                act[rows, cols] = a.astype(BF16)

            def first_block(rb, n):
                if n is None:
                    @pl.when(rb + 1 < n_blk)
                    def _():
                        x_copy(s, rb + 1, (rb + 1) % 2).start()
                elif rb + 1 < n:
                    x_copy(s, rb + 1, (rb + 1) % 2).start()
                x_copy(s, rb, rb % 2).wait()
                xbf[pl.ds(rb * EXP_TM, EXP_TM), :] = slot_ref(rb % 2)[...].astype(BF16)
                act_block(rb, n)

            @pl.when(c == 0)
            def _():
                _for_blocks(n_blk, n_blocks, cast_weights, first_block)

            @pl.when(c > 0)
            def _():
                _for_blocks(n_blk, n_blocks, cast_weights, act_block)

        @pl.when(c >= n_act)
        def _():
            cols = pl.ds(pl.multiple_of((c - n_act) * EXP_TN, EXP_TN), EXP_TN)
            bd = bd_ref[pl.ds(expert, 1), cols]
            for rb in range(n_blocks):
                @pl.when(jnp.logical_and(c == n_act, jnp.logical_and(s > 0, rb < n_blk_prev)))
                def _(rb=rb):
                    y_copy(s - 1, rb).wait()

            def cast_weights():
                wdb_ref[...] = wd_ref[0].astype(BF16)

            def down_block(rb, n):
                rows = pl.ds(rb * EXP_TM, EXP_TM)
                out[rows, cols] = jnp.dot(act[rows, :], wdb_ref[...],
                                          preferred_element_type=F32) + bd

            _for_blocks(n_blk, n_blocks, cast_weights, down_block)

            @pl.when(c == n_act + n_down - 1)
            def _():
                for rb in range(n_blocks):
                    @pl.when(rb < n_blk)
                    def _(rb=rb):
                        y_copy(s, rb).start()

                @pl.when(s + 1 < n_super)
                def _():
                    x_copy(s + 1, 0, 0).start()

                @pl.when(s + 1 == n_super)
                def _():
                    for rb in range(n_blocks):
                        @pl.when(rb < n_blk)
                        def _(rb=rb):
                            y_copy(s, rb).wait()


def _expert(xs, sb_expert, sb_row0, sb_nblk, n_super, g_end, wg, wu, wd, bg, bu, bd):
    n_rows_total, d = xs.shape
    s_max = sb_expert.shape[0]
    d_exp = wg.shape[2]
    assert d_exp % EXP_TH == 0 and d % EXP_TN == 0 and ZERO_ROWS == 2 * EXP_TM
    n_act = d_exp // EXP_TH
    n_down = d // EXP_TN
    ne = wg.shape[0]

    def e_eff(s, sbe, ns):
        return sbe[jnp.minimum(s, ns[0] - 1)]

    def c_act(s, c, ns):
        return jnp.where(s < ns[0], jnp.minimum(c, n_act - 1), n_act - 1)

    def c_down(s, c, ns):
        return jnp.where(s < ns[0], jnp.maximum(c - n_act, 0), n_down - 1)

    act_w = pl.BlockSpec((1, d, EXP_TH),
                         lambda s, c, sbe, sbr, sbn, ns, ge: (e_eff(s, sbe, ns), 0, c_act(s, c, ns)))
    bias = pl.BlockSpec((ne, d_exp), lambda s, c, sbe, sbr, sbn, ns, ge: (0, 0))
    grid_spec = pltpu.PrefetchScalarGridSpec(
        num_scalar_prefetch=5,
        grid=(s_max, n_act + n_down),
        in_specs=[
            pl.BlockSpec(memory_space=pl.ANY),
            act_w, act_w,
            pl.BlockSpec((1, d_exp, EXP_TN),
                         lambda s, c, sbe, sbr, sbn, ns, ge: (e_eff(s, sbe, ns), 0, c_down(s, c, ns))),
            bias, bias,
            pl.BlockSpec((ne, d), lambda s, c, sbe, sbr, sbn, ns, ge: (0, 0)),
        ],
        out_specs=pl.BlockSpec(memory_space=pl.ANY),
        scratch_shapes=[pltpu.VMEM((ZERO_ROWS, d), F32), pltpu.VMEM((SB_ROWS, d), BF16),
                        pltpu.VMEM((SB_ROWS, d_exp), BF16), pltpu.VMEM((SB_ROWS, d), F32),
                        pltpu.VMEM((d, 2 * EXP_TH), BF16), pltpu.VMEM((d_exp, EXP_TN), BF16),
                        pltpu.SemaphoreType.DMA((2,)),
                        pltpu.SemaphoreType.DMA((SB_ROWS // EXP_TM,))],
    )
    return pl.pallas_call(
        functools.partial(_expert_kernel, n_act, n_down),
        grid_spec=grid_spec,
        out_shape=jax.ShapeDtypeStruct((n_rows_total, d), F32),
        compiler_params=_cparams(("arbitrary", "arbitrary")),
        name="expert",
    )(sb_expert, sb_row0, sb_nblk, n_super, g_end, xs, wg, wu, wd, bg, bu, bd)


def _combine_start(y_hbm, pos_ref, stage_slot, sem):
    tt = stage_slot.shape[1]

    def body(r, carry):
        for k in range(TOP_K):
            p = pos_ref[0, 0, r * TOP_K + k]
            pltpu.make_async_copy(y_hbm.at[pl.ds(p, 1), :], stage_slot.at[k, pl.ds(r, 1), :],
                                  sem).start(priority=k % 2)
        return carry
    lax.fori_loop(0, tt, body, 0, unroll=True)


def _combine_kernel(final_norm, pos_cur_ref, pos_nxt_ref, y_hbm, prob_ref, x1_ref, mod_ref, g_ref,
                    o_ref, stage, sem):
    j = pl.program_id(0)
    nj = pl.num_programs(0)
    tt = stage.shape[2]

    @pl.when(j == 0)
    def _():
        _combine_start(y_hbm, pos_cur_ref, stage.at[0], sem.at[0])

    @pl.when(j + 1 < nj)
    def _():
        nxt = (j + 1) % 2
        _combine_start(y_hbm, pos_nxt_ref, stage.at[nxt], sem.at[nxt])

    cur = j % 2
    for k in range(TOP_K):
        pltpu.make_async_copy(y_hbm.at[pl.ds(0, tt), :], stage.at[cur, k], sem.at[cur]).wait()
    prob = prob_ref[...]
    y = prob[:, 0:1] * stage[cur, 0]
    for k in range(1, TOP_K):
        y = y + prob[:, k:k + 1] * stage[cur, k]
    x2 = x1_ref[...] + mod_ref[0, 5:6, :] * y
    if final_norm:
        r = lax.rsqrt(jnp.mean(x2 * x2, axis=-1, keepdims=True) + RMS_EPS)
        x2 = (x2 * r) * g_ref[...]
    o_ref[...] = x2


def _combine(yb, pos, probs, x1, mod3, g, seq, final_norm):
    n, d = x1.shape
    tt = COMB_ROWS
    nt = n // tt
    tiles_per_seq = seq // tt
    pos3 = pos.reshape(nt, 1, tt * TOP_K)
    return pl.pallas_call(
        functools.partial(_combine_kernel, final_norm),
        grid=(nt,),
        in_specs=[pl.BlockSpec((1, 1, tt * TOP_K), lambda j: (j, 0, 0), memory_space=pltpu.SMEM),
                  pl.BlockSpec((1, 1, tt * TOP_K), lambda j: (jnp.minimum(j + 1, nt - 1), 0, 0),
                               memory_space=pltpu.SMEM),
                  pl.BlockSpec(memory_space=pl.ANY),
                  pl.BlockSpec((tt, LANES), lambda j: (j, 0)),
                  pl.BlockSpec((tt, d), lambda j: (j, 0)),
                  pl.BlockSpec((1, 6, d), lambda j: (j // tiles_per_seq, 0, 0)),
                  pl.BlockSpec((1, d), lambda j: (0, 0))],
        out_specs=pl.BlockSpec((tt, d), lambda j: (j, 0)),
        out_shape=jax.ShapeDtypeStruct((n, d), F32),
        scratch_shapes=[pltpu.VMEM((2, TOP_K, tt, d), F32), pltpu.SemaphoreType.DMA((2,))],
        compiler_params=_cparams(("arbitrary",)),
        name="combine",
    )(pos3, pos3, yb, probs, x1, mod3, g.reshape(1, d))


def _count_le(sorted_ends, q):
    return jnp.sum((sorted_ends[None, :] <= q[:, None]).astype(jnp.int32), axis=1)


def _plan(counts, n_tok):
    s_max = -(-n_tok * TOP_K // SB_ROWS) + N_EXPERTS
    cnt = counts[0, :N_EXPERTS]
    cnt_al = (cnt + ROW_ALIGN - 1) // ROW_ALIGN * ROW_ALIGN
    g_end = jnp.cumsum(cnt_al)
    g_start = g_end - cnt_al
    nsb_e = (cnt + SB_ROWS - 1) // SB_ROWS
    sb_end = jnp.cumsum(nsb_e)
    sb_start = sb_end - nsb_e
    n_super = sb_end[-1]
    s_ids = jnp.arange(s_max, dtype=jnp.int32)
    sb_expert = jnp.minimum(_count_le(sb_end, s_ids), N_EXPERTS - 1)
    j_in = s_ids - sb_start[sb_expert]
    live = s_ids < n_super
    sb_rows = jnp.where(live, jnp.clip(cnt[sb_expert] - j_in * SB_ROWS, 0, SB_ROWS), 0)
    sb_row0 = jnp.where(live, g_start[sb_expert] + j_in * SB_ROWS, 0)
    sb_nblk = (sb_rows + EXP_TM - 1) // EXP_TM
    i32 = lambda a: a.astype(jnp.int32)
    return (i32(g_start), i32(cnt), i32(g_end[-1:]), i32(sb_expert), i32(sb_row0), i32(sb_nblk),
            i32(n_super.reshape(1)))


def _dest_kernel(ir_ref, gs_ref, o_ref):
    ir = ir_ref[...].astype(F32)
    lane = lax.broadcasted_iota(jnp.int32, ir.shape, 1)
    lane_f = lane.astype(F32)
    out = jnp.zeros(ir.shape, F32)
    for k in range(TOP_K):
        e = jnp.sum(jnp.where(lane == k, ir, 0.0), axis=1, keepdims=True)
        r = jnp.sum(jnp.where(lane == TOP_K + k, ir, 0.0), axis=1, keepdims=True)
        g = jnp.sum(jnp.where(lane_f == e, gs_ref[...], 0.0), axis=1, keepdims=True)
        out = jnp.where(lane == k, g + r, out)
    o_ref[...] = out.astype(jnp.int32)


def _dest(idx_rank, g_start):
    n = idx_rank.shape[0]
    tt = 1024
    gs = jnp.pad(g_start.astype(F32), (0, LANES - N_EXPERTS)).reshape(1, LANES)
    return pl.pallas_call(
        _dest_kernel,
        grid=(n // tt,),
        in_specs=[pl.BlockSpec((tt, LANES), lambda i: (i, 0)),
                  pl.BlockSpec((1, LANES), lambda i: (0, 0))],
        out_specs=pl.BlockSpec((tt, LANES), lambda i: (i, 0)),
        out_shape=jax.ShapeDtypeStruct((n, LANES), jnp.int32),
        compiler_params=_cparams(("arbitrary",)),
        name="dest",
    )(idx_rank, gs)


def kernel(x, c, ada_w, ada_b, norm_mix_g, w_in, b_forget, conv_w, w_conv_out, w_attn_out, w_out,
           norm_ffn_g, router_w, router_b, exp_w_gate, exp_b_gate, exp_w_up, exp_b_up,
           exp_w_down, exp_b_down, final_norm_g):
    bsz, seq, d = x.shape
    n = bsz * seq
    d_conv = conv_w.shape[2]
    d_attn = w_attn_out.shape[1]
    n_main = 3 * d_conv + 3 * d_attn
    x2 = x.reshape(n, d)
    out = x2
    for l in range(ada_w.shape[0]):
        mod3 = _mod(c, ada_w[l], ada_b[l]).reshape(bsz, 6, d)
        proj, gates, f_logit = _inproj(out, mod3, norm_mix_g[l], w_in[l].T, n_main, N_HEADS,
                                       w_in.shape[2] - n_main - N_HEADS, seq,
                                       (3 * d_conv, 3 * d_conv + d_attn))
        fcol = _cumf(f_logit, b_forget[l], seq)
        att = _attn(proj, fcol, bsz, seq, 3 * d_conv, 3 * d_conv + d_attn, 3 * d_conv + 2 * d_attn)
        rw = jnp.pad(router_w[l], ((0, 0), (0, LANES - N_EXPERTS)))
        rb = jnp.pad(router_b[l], (0, LANES - N_EXPERTS)).reshape(1, LANES)
        x1, h2, logits = _mix(proj, gates, att, out, mod3, norm_ffn_g[l], conv_w[l],
                              _cast_bf16(w_conv_out[l]), _cast_bf16(w_attn_out[l]), _cast_bf16(w_out[l]),
                              rw, rb, seq, d_conv)
        idx_rank, probs, counts = _route(logits)
        g_start, cnt, g_end, sb_expert, sb_row0, sb_nblk, n_super = _plan(counts, n)
        dest = _dest(idx_rank, g_start)[:, :TOP_K].reshape(-1)
        n_rows_total = n * TOP_K + N_EXPERTS * ROW_ALIGN + EXP_TM
        xs = _dispatch(h2, dest, g_start, cnt, g_end, n_rows_total)
        ys = _expert(xs, sb_expert, sb_row0, sb_nblk, n_super, g_end, exp_w_gate[l], exp_w_up[l],
                     exp_w_down[l], exp_b_gate[l], exp_b_up[l], exp_b_down[l])
        out = _combine(ys, dest, probs, x1, mod3, final_norm_g, seq,
                       final_norm=(l == ada_w.shape[0] - 1))
    return out.reshape(bsz, seq, d)
```
